```python
import jax, jax.numpy as jnp
from jax import lax
import numpy as np

D_MODEL = 1024
BATCH = 16
SEQ = 256
DEPTH = 1
DEC_BATCH = 8
DEC_SEQ = 4096
PAST_LEN = 256

GRID_W = 64
HEAD_DIM = 64
ATTN_WIDTH = D_MODEL // 2
N_Q_HEADS = ATTN_WIDTH // HEAD_DIM
N_KV_HEADS = N_Q_HEADS // 4
KV_GROUP = N_Q_HEADS // N_KV_HEADS
AXIS_DIM = HEAD_DIM // 2
ROPE_THETA = 10000.0
ATTN_SCALE = HEAD_DIM ** -0.5
Q_BLOCK = 128
MLSTM_WIDTH = D_MODEL - ATTN_WIDTH
MLSTM_HEADS = 4
MLSTM_DK = MLSTM_WIDTH // MLSTM_HEADS
MLSTM_DV = MLSTM_DK
CHUNK = 128
D_FF = -(-8 * D_MODEL // (3 * 256)) * 256
EPS = 1e-6
F32 = jnp.float32

SPLIT_WIDTHS = [N_Q_HEADS * HEAD_DIM, N_KV_HEADS * HEAD_DIM, N_KV_HEADS * HEAD_DIM,
                MLSTM_WIDTH, MLSTM_WIDTH, MLSTM_WIDTH, MLSTM_WIDTH, 4 * MLSTM_HEADS]
N_IN = sum(SPLIT_WIDTHS)
SPLIT_POINTS = [int(v) for v in np.cumsum(SPLIT_WIDTHS)[:-1]]

kernel_name = 'hybrid_gqa_mlstm_diffusion_step'


def rmsnorm(x, w):
    xf = x.astype(F32)
    y = xf * lax.rsqrt(jnp.mean(xf * xf, axis=-1, keepdims=True) + EPS)
    return (y * w.astype(F32)).astype(x.dtype)


def ada_mods(cond, w_ada_l, b_ada_l):
    m = jax.nn.silu(cond) @ w_ada_l + b_ada_l
    m = m.reshape(-1, 1, 6 * D_MODEL)
    return jnp.split(m, 6, axis=-1)


def axial_rope(n_tok):
    rows = n_tok // GRID_W
    row_ids = jnp.repeat(jnp.arange(rows, dtype=F32), GRID_W)
    col_ids = jnp.tile(jnp.arange(GRID_W, dtype=F32), rows)
    inv = ROPE_THETA ** (-jnp.arange(0, AXIS_DIM, 2, dtype=F32) / AXIS_DIM)
    ang = jnp.concatenate([row_ids[:, None] * inv, col_ids[:, None] * inv], axis=-1)
    return jnp.cos(ang), jnp.sin(ang)


def apply_rope(x, cos, sin):
    B, T, H, _ = x.shape
    xr = x.astype(F32).reshape(B, T, H, 2, 2, AXIS_DIM // 2)
    x1, x2 = xr[..., 0, :], xr[..., 1, :]
    c = cos.reshape(T, 1, 2, AXIS_DIM // 2)
    s = sin.reshape(T, 1, 2, AXIS_DIM // 2)
    out = jnp.stack([x1 * c - x2 * s, x2 * c + x1 * s], axis=-2)
    return out.reshape(x.shape).astype(x.dtype)


def attention(q, k, v):
    B, Tq, _, _ = q.shape
    nb = Tq // Q_BLOCK
    qb = q.reshape(B, nb, Q_BLOCK, N_KV_HEADS, KV_GROUP, HEAD_DIM).transpose(1, 0, 2, 3, 4, 5)

    def one_block(qi):
        s = jnp.einsum('bqhgd,bkhd->bhgqk', qi, k).astype(F32) * ATTN_SCALE
        p = jax.nn.softmax(s, axis=-1).astype(v.dtype)
        return jnp.einsum('bhgqk,bkhd->bqhgd', p, v)

    o = lax.map(one_block, qb)
    return o.transpose(1, 0, 2, 3, 4, 5).reshape(B, Tq, N_Q_HEADS * HEAD_DIM)


def mlstm_scan(q, k, v, ig, lf, C0, n0, m0):
    B, H, T, _ = q.shape
    nc = T // CHUNK

    def to_chunks(a):
        return jnp.moveaxis(a.reshape(B, H, nc, CHUNK, *a.shape[3:]), 2, 0)

    mask = jnp.tril(jnp.ones((CHUNK, CHUNK), dtype=bool))

    def step(carry, inp):
        C, n, m = carry
        qc, kc, vc, ic, fc = inp
        b = jnp.cumsum(fc, axis=-1)
        log_d = jnp.where(mask, b[..., :, None] - b[..., None, :] + ic[..., None, :], -jnp.inf)
        m_inter = b + m[..., None]
        m_t = jnp.maximum(jnp.max(log_d, axis=-1), m_inter)
        s = jnp.einsum('bhld,bhsd->bhls', qc, kc) * jnp.exp(log_d - m_t[..., None])
        inter = jnp.exp(m_inter - m_t)
        num = jnp.einsum('bhls,bhse->bhle', s, vc) + inter[..., None] * jnp.einsum('bhld,bhde->bhle', qc, C)
        den = jnp.sum(s, axis=-1) + inter * jnp.einsum('bhld,bhd->bhl', qc, n)
        h = num / jnp.maximum(jnp.abs(den), jnp.exp(-m_t))[..., None]
        m_new = m_t[..., -1]
        w = jnp.exp(b[..., -1:] - b + ic - m_new[..., None])
        decay = jnp.exp(b[..., -1] + m - m_new)
        C_new = decay[..., None, None] * C + jnp.einsum('bhs,bhsd,bhse->bhde', w, kc, vc)
        n_new = decay[..., None] * n + jnp.einsum('bhs,bhsd->bhd', w, kc)
        return (C_new, n_new, m_new), h

    inputs = (to_chunks(q), to_chunks(k), to_chunks(v), to_chunks(ig), to_chunks(lf))
    (C, n, m), hs = lax.scan(step, (C0, n0, m0), inputs)
    h = jnp.moveaxis(hs, 0, 2).reshape(B, H, T, MLSTM_DV)
    return h, (C, n, m)


def mlstm_bidir(q, k, v, gates, init_f, init_b):
    h_f, st_f = mlstm_scan(q, k, v, gates[0], jax.nn.log_sigmoid(gates[1]), *init_f)
    fl = lambda a: jnp.flip(a, axis=2)
    h_b, st_b = mlstm_scan(fl(q), fl(k), fl(v), fl(gates[2]), fl(jax.nn.log_sigmoid(gates[3])), *init_b)
    return h_f + fl(h_b), st_f, st_b


def swiglu(h, w_gu, w_down):
    g, u = jnp.split(h @ w_gu, 2, axis=-1)
    return (jax.nn.silu(g) * u) @ w_down


def trunk_layer(x, mods, norm1_w, w_in, gate_bias, q_norm_w, k_norm_w, mlstm_norm_w, w_out,
                norm2_w, w_gu, w_down, rope, ctx_kv, init_f, init_b):
    sh1, sc1, g1, sh2, sc2, g2 = mods
    B, T, _ = x.shape
    h = rmsnorm(x, norm1_w) * (1 + sc1) + sh1
    aq, ak, av, mq, mk, mv, mo, mg = jnp.split(h @ w_in, SPLIT_POINTS, axis=-1)
    aq = rmsnorm(aq.reshape(B, T, N_Q_HEADS, HEAD_DIM), q_norm_w)
    ak = rmsnorm(ak.reshape(B, T, N_KV_HEADS, HEAD_DIM), k_norm_w)
    av = av.reshape(B, T, N_KV_HEADS, HEAD_DIM)
    if rope is None:
        keys, vals = ak, av
    else:
        aq = apply_rope(aq, *rope)
        keys = jnp.concatenate([apply_rope(ak, *rope), ctx_kv[0].astype(ak.dtype)], axis=1)
        vals = jnp.concatenate([av, ctx_kv[1].astype(av.dtype)], axis=1)
    attn_o = attention(aq, keys, vals)
    heads = lambda a: a.reshape(B, T, MLSTM_HEADS, -1).transpose(0, 2, 1, 3).astype(F32)
    gates = (mg.reshape(B, T, 4, MLSTM_HEADS).astype(F32) + gate_bias.astype(F32)).transpose(2, 0, 3, 1)
    hm, st_f, st_b = mlstm_bidir(heads(mq), heads(mk) * (MLSTM_DK ** -0.5), heads(mv), gates, init_f, init_b)
    hm = rmsnorm(hm.transpose(0, 2, 1, 3), mlstm_norm_w.reshape(MLSTM_HEADS, MLSTM_DV))
    hm = (hm.reshape(B, T, MLSTM_WIDTH) * jax.nn.sigmoid(mo.astype(F32))).astype(x.dtype)
    mix = jnp.concatenate([attn_o, hm], axis=-1) @ w_out
    x = x + g1 * mix
    h2 = rmsnorm(x, norm2_w) * (1 + sc2) + sh2
    x = x + g2 * swiglu(h2, w_gu, w_down)
    return x, ak, av, st_f, st_b


def setup_inputs(seed: int = 0) -> dict:
    key = jax.random.key(seed)
    ks = jax.random.split(key, 24)
    nrm = lambda k, shape, s=1.0: jax.random.normal(k, shape, F32) * s
    base_gate = jnp.array([0.0, 1.0, 0.0, 1.0], F32)[:, None] * jnp.linspace(3.0, 6.0, MLSTM_HEADS, dtype=F32)[None, :]
    return {
        'x_prompt': nrm(ks[0], (BATCH, SEQ, D_MODEL)),
        'x_sample': nrm(ks[1], (DEC_BATCH, DEC_SEQ, D_MODEL)),
        'cache_k': nrm(ks[2], (DEC_BATCH, DEPTH, PAST_LEN, N_KV_HEADS, HEAD_DIM)),
        'cache_v': nrm(ks[3], (DEC_BATCH, DEPTH, PAST_LEN, N_KV_HEADS, HEAD_DIM)),
        'state_C': nrm(ks[4], (DEC_BATCH, DEPTH, 2, MLSTM_HEADS, MLSTM_DK, MLSTM_DV), 0.1),
        'state_n': nrm(ks[5], (DEC_BATCH, DEPTH, 2, MLSTM_HEADS, MLSTM_DK), 0.1),
        'state_m': nrm(ks[6], (DEC_BATCH, DEPTH, 2, MLSTM_HEADS), 0.5),
        'c': nrm(ks[7], (DEC_BATCH, D_MODEL)),
        'c_ctx': nrm(ks[8], (D_MODEL,)),
        'w_ada': nrm(ks[9], (DEPTH, D_MODEL, 6 * D_MODEL), 0.5 * D_MODEL ** -0.5),
        'b_ada': nrm(ks[10], (DEPTH, 6 * D_MODEL), 0.02),
        'norm1_w': 1.0 + nrm(ks[11], (DEPTH, D_MODEL), 0.05),
        'w_in': nrm(ks[12], (DEPTH, D_MODEL, N_IN), D_MODEL ** -0.5),
        'gate_bias': base_gate + nrm(ks[13], (DEPTH, 4, MLSTM_HEADS), 0.1),
        'q_norm_w': 1.0 + nrm(ks[14], (DEPTH, HEAD_DIM), 0.05),
        'k_norm_w': 1.0 + nrm(ks[15], (DEPTH, HEAD_DIM), 0.05),
        'mlstm_norm_w': 1.0 + nrm(ks[16], (DEPTH, MLSTM_WIDTH), 0.05),
        'w_out': nrm(ks[17], (DEPTH, D_MODEL, D_MODEL), D_MODEL ** -0.5),
        'norm2_w': 1.0 + nrm(ks[18], (DEPTH, D_MODEL), 0.05),
        'w_gu': nrm(ks[19], (DEPTH, D_MODEL, 2 * D_FF), D_MODEL ** -0.5),
        'w_down': nrm(ks[20], (DEPTH, D_FF, D_MODEL), D_FF ** -0.5),
        'final_norm_w': 1.0 + nrm(ks[21], (D_MODEL,), 0.05),
    }


def reference(x_prompt, x_sample, cache_k, cache_v, state_C, state_n, state_m, c, c_ctx,
              w_ada, b_ada, norm1_w, w_in, gate_bias, q_norm_w, k_norm_w, mlstm_norm_w, w_out,
              norm2_w, w_gu, w_down, final_norm_w):
    xp = x_prompt
    B = xp.shape[0]
    zero_state = (jnp.zeros((B, MLSTM_HEADS, MLSTM_DK, MLSTM_DV), F32),
                  jnp.zeros((B, MLSTM_HEADS, MLSTM_DK), F32),
                  jnp.zeros((B, MLSTM_HEADS), F32))
    ks_, vs_, Cs_, ns_, ms_ = [], [], [], [], []
    for l in range(DEPTH):
        mods = ada_mods(c_ctx, w_ada[l], b_ada[l])
        xp, k_l, v_l, st_f, st_b = trunk_layer(
            xp, mods, norm1_w[l], w_in[l], gate_bias[l], q_norm_w[l], k_norm_w[l], mlstm_norm_w[l],
            w_out[l], norm2_w[l], w_gu[l], w_down[l], None, None, zero_state, zero_state)
        ks_.append(k_l)
        vs_.append(v_l)
        Cs_.append(jnp.stack([st_f[0], st_b[0]], axis=1))
        ns_.append(jnp.stack([st_f[1], st_b[1]], axis=1))
        ms_.append(jnp.stack([st_f[2], st_b[2]], axis=1))
    y_prompt = rmsnorm(xp, final_norm_w)
    new_cache_k = jnp.stack(ks_, axis=1)
    new_cache_v = jnp.stack(vs_, axis=1)
    new_state_C = jnp.stack(Cs_, axis=1)
    new_state_n = jnp.stack(ns_, axis=1)
    new_state_m = jnp.stack(ms_, axis=1)

    xs = x_sample
    rope = axial_rope(xs.shape[1])
    for l in range(DEPTH):
        mods = ada_mods(c, w_ada[l], b_ada[l])
        init_f = (state_C[:, l, 0].astype(F32), state_n[:, l, 0].astype(F32), state_m[:, l, 0].astype(F32))
        init_b = (state_C[:, l, 1].astype(F32), state_n[:, l, 1].astype(F32), state_m[:, l, 1].astype(F32))
        xs, _, _, _, _ = trunk_layer(
            xs, mods, norm1_w[l], w_in[l], gate_bias[l], q_norm_w[l], k_norm_w[l], mlstm_norm_w[l],
            w_out[l], norm2_w[l], w_gu[l], w_down[l], rope, (cache_k[:, l], cache_v[:, l]), init_f, init_b)
    y_sample = rmsnorm(xs, final_norm_w)
    return (y_prompt, y_sample, new_cache_k, new_cache_v, new_state_C, new_state_n, new_state_m)
```

```python
import functools

import jax
import jax.numpy as jnp
from jax import lax
from jax.experimental import pallas as pl
from jax.experimental.pallas import tpu as pltpu

F32 = jnp.float32
BF16 = jnp.bfloat16

D_MODEL = 1024
HEAD_DIM = 64
N_Q_HEADS = 8
N_KV_HEADS = 2
KV_GROUP = N_Q_HEADS // N_KV_HEADS
ATTN_WIDTH = N_Q_HEADS * HEAD_DIM
KV_WIDTH = N_KV_HEADS * HEAD_DIM
AXIS_DIM = HEAD_DIM // 2
GRID_W = 64
ROPE_THETA = 10000.0
MLSTM_HEADS = 4
MLSTM_DK = 128
MLSTM_WIDTH = MLSTM_HEADS * MLSTM_DK
CHUNK = 128
D_FF = 2816
N_GATES = 4 * MLSTM_HEADS
EPS = 1e-6
LOG2E = 1.4426950408889634
NEG_BIG = -1e30

TOKEN_TILE = 512
Q_TILE = 256
KEY_CHUNK = 128
FF_CHUNK = 1408
ADA_TILE = 1536
VMEM_LIMIT = 56 * 1024 * 1024

_R_Q, _R_K, _R_V, _R_MK, _R_G, _R_END = 0, 512, 640, 768, 1280, 1296


def _resident(shape):
    nd = len(shape)
    return pl.BlockSpec(shape, lambda *_: (0,) * nd, pipeline_mode=pl.Buffered(1))


def _mods_kernel(cond_ref, w_ref, b_ref, o_ref):
    c = cond_ref[...]
    s = c * jax.nn.sigmoid(c)
    o_ref[...] = jnp.dot(s, w_ref[...], preferred_element_type=F32) + b_ref[...]


def _mods_call(cond, w_ada, b_ada):
    rows, n_out = cond.shape[0], w_ada.shape[1]
    return pl.pallas_call(
        _mods_kernel,
        out_shape=jax.ShapeDtypeStruct((rows, n_out), F32),
        grid=(n_out // ADA_TILE,),
        in_specs=[pl.BlockSpec((rows, D_MODEL), lambda j: (0, 0)),
                  pl.BlockSpec((D_MODEL, ADA_TILE), lambda j: (0, j)),
                  pl.BlockSpec((1, ADA_TILE), lambda j: (0, j))],
        out_specs=pl.BlockSpec((rows, ADA_TILE), lambda j: (0, j)),
        compiler_params=pltpu.CompilerParams(dimension_semantics=("parallel",),
                                             vmem_limit_bytes=VMEM_LIMIT),
        name="adaln_mods",
    )(cond, w_ada, b_ada.reshape(1, n_out))


def _rms(x):
    return x * lax.rsqrt(jnp.mean(x * x, axis=-1, keepdims=True) + EPS)


def _proj_kernel(x_ref, mods_ref, n1w_ref, wt_ref, wn_ref, gb_ref, tqa_ref, tqb_ref, tka_ref, tkb_ref,
                 qT_ref, k_ref, vT_ref, mkT_ref, g_ref, mq_ref, mv_ref, og_ref, *cache_refs):
    tm = x_ref.shape[0]
    x = x_ref[...]
    sh = mods_ref[0, 0:1, :]
    sc = mods_ref[0, 1:2, :]
    h = (_rms(x) * n1w_ref[...] * (1.0 + sc) + sh).astype(BF16)

    def proj_t(r0, r1):
        return lax.dot_general(wt_ref[r0:r1, :], h, (((1,), (1,)), ((), ())), preferred_element_type=F32)

    n_qk = N_Q_HEADS + N_KV_HEADS
    qk = proj_t(_R_Q, _R_V).reshape(n_qk, HEAD_DIM, tm)
    qk = qk * lax.rsqrt(jnp.sum(qk * qk, axis=1, keepdims=True) * (1.0 / HEAD_DIM) + EPS)
    sw = jnp.concatenate([qk[:, 16:32], qk[:, 0:16], qk[:, 48:64], qk[:, 32:48]], axis=1)
    q = qk[:N_Q_HEADS] * tqa_ref[...][None] + sw[:N_Q_HEADS] * tqb_ref[...][None]
    q = q.reshape(ATTN_WIDTH, tm).astype(BF16)
    for j in range(tm // Q_TILE):
        qT_ref[j] = q[:, j * Q_TILE:(j + 1) * Q_TILE]
    k = qk[N_Q_HEADS:] * tka_ref[...][None] + sw[N_Q_HEADS:] * tkb_ref[...][None]
    kt = k.reshape(KV_WIDTH, tm).T
    k_ref[...] = kt.astype(BF16)

    vt = proj_t(_R_V, _R_MK)
    vtb = vt.astype(BF16)
    for j in range(tm // CHUNK):
        vT_ref[j] = vtb[:, j * CHUNK:(j + 1) * CHUNK]
    if cache_refs:
        kc_ref, vc_ref = cache_refs
        kc_ref[...] = kt
        vc_ref[...] = vt.T

    mk = (proj_t(_R_MK, _R_G) * (MLSTM_DK ** -0.5)).astype(BF16)
    gates = proj_t(_R_G, _R_END) + gb_ref[...]
    for j in range(tm // CHUNK):
        mkT_ref[j] = mk[:, j * CHUNK:(j + 1) * CHUNK]
        g_ref[j] = gates[:, j * CHUNK:(j + 1) * CHUNK]

    mq_ref[...] = jnp.dot(h, wn_ref[:, 0:512], preferred_element_type=F32).astype(BF16)
    mv_ref[...] = jnp.dot(h, wn_ref[:, 512:1024], preferred_element_type=F32).astype(BF16)
    mo = jnp.dot(h, wn_ref[:, 1024:1536], preferred_element_type=F32)
    og_ref[...] = jax.nn.sigmoid(mo).astype(BF16)


def _proj_call(x2d, mods, tiles_per_mod, n1w, wt, wn, gbias, tables, tiles_per_table, emit_cache):
    T = x2d.shape[0]
    tm = TOKEN_TILE
    n_tiles = T // tm
    if tiles_per_mod is None:
        mod_map = lambda i: (0, 0, 0)
    else:
        mod_map = lambda i: (i // tiles_per_mod, 0, 0)
    if tiles_per_table is None:
        tab_map = lambda i: (0, 0)
    else:
        tab_map = lambda i: (0, i % tiles_per_table)
    tab_spec = pl.BlockSpec((HEAD_DIM, tm), tab_map)
    out_shape = [
        jax.ShapeDtypeStruct((T // Q_TILE, ATTN_WIDTH, Q_TILE), BF16),
        jax.ShapeDtypeStruct((T, KV_WIDTH), BF16),
        jax.ShapeDtypeStruct((T // CHUNK, KV_WIDTH, CHUNK), BF16),
        jax.ShapeDtypeStruct((T // CHUNK, MLSTM_WIDTH, CHUNK), BF16),
        jax.ShapeDtypeStruct((T // CHUNK, N_GATES, CHUNK), F32),
        jax.ShapeDtypeStruct((T, MLSTM_WIDTH), BF16),
        jax.ShapeDtypeStruct((T, MLSTM_WIDTH), BF16),
        jax.ShapeDtypeStruct((T, MLSTM_WIDTH), BF16),
    ]
    out_specs = [
        pl.BlockSpec((tm // Q_TILE, ATTN_WIDTH, Q_TILE), lambda i: (i, 0, 0)),
        pl.BlockSpec((tm, KV_WIDTH), lambda i: (i, 0)),
        pl.BlockSpec((tm // CHUNK, KV_WIDTH, CHUNK), lambda i: (i, 0, 0)),
        pl.BlockSpec((tm // CHUNK, MLSTM_WIDTH, CHUNK), lambda i: (i, 0, 0)),
        pl.BlockSpec((tm // CHUNK, N_GATES, CHUNK), lambda i: (i, 0, 0)),
        pl.BlockSpec((tm, MLSTM_WIDTH), lambda i: (i, 0)),
        pl.BlockSpec((tm, MLSTM_WIDTH), lambda i: (i, 0)),
        pl.BlockSpec((tm, MLSTM_WIDTH), lambda i: (i, 0)),
    ]
    if emit_cache:
        out_shape += [jax.ShapeDtypeStruct((T, KV_WIDTH), F32)] * 2
        out_specs += [pl.BlockSpec((tm, KV_WIDTH), lambda i: (i, 0))] * 2
    return pl.pallas_call(
        _proj_kernel,
        out_shape=out_shape,
        grid=(n_tiles,),
        in_specs=[pl.BlockSpec((tm, D_MODEL), lambda i: (i, 0)),
                  pl.BlockSpec((1, 6, D_MODEL), mod_map),
                  _resident((1, D_MODEL)),
                  _resident(wt.shape),
                  _resident(wn.shape),
                  _resident((N_GATES, 1)),
                  tab_spec, tab_spec, tab_spec, tab_spec],
        out_specs=out_specs,
        compiler_params=pltpu.CompilerParams(dimension_semantics=("parallel",),
                                             vmem_limit_bytes=VMEM_LIMIT),
        name="norm1_in_proj",
    )(x2d, mods, n1w, wt, wn, gbias, *tables)


def _attn_kernel(*refs, n_new_chunks, has_cache):
    if has_cache:
        qT_ref, k_ref, vT_ref, ck_ref, cvT_ref, o_ref, oT_scr = refs
    else:
        qT_ref, k_ref, vT_ref, o_ref, oT_scr = refs
    tq = qT_ref.shape[2]

    for hd in range(N_Q_HEADS):
        kvh = hd // KV_GROUP
        r0 = kvh * HEAD_DIM
        qg = qT_ref[0, hd * HEAD_DIM:(hd + 1) * HEAD_DIM, :]
        zero = jnp.zeros_like(qg)
        qp = jnp.concatenate([qg, zero] if kvh == 0 else [zero, qg], axis=0)

        def step(kch, vch, carry, qp=qp):
            m, l, acc = carry
            s = jnp.dot(kch, qp, preferred_element_type=F32)
            m_new = jnp.maximum(m, jnp.max(s, axis=0, keepdims=True))
            p = jnp.exp2(s - m_new)
            alpha = jnp.exp2(m - m_new)
            l = alpha * l + jnp.sum(p, axis=0, keepdims=True)
            acc = alpha * acc + jnp.dot(vch, p.astype(BF16), preferred_element_type=F32)
            return m_new, l, acc

        def body(c, carry, r0=r0, step=step):
            off = pl.multiple_of(c * KEY_CHUNK, KEY_CHUNK)
            kch = k_ref[pl.ds(off, KEY_CHUNK), :]
            vch = vT_ref[c, r0:r0 + HEAD_DIM, :]
            return step(kch, vch, carry)

        carry = (jnp.full((1, tq), NEG_BIG, F32), jnp.zeros((1, tq), F32), jnp.zeros((HEAD_DIM, tq), F32))
        carry = lax.fori_loop(0, n_new_chunks, body, carry)
        if has_cache:
            for c in range(ck_ref.shape[1] // KEY_CHUNK):
                kch = ck_ref[0, c * KEY_CHUNK:(c + 1) * KEY_CHUNK, :]
                vch = cvT_ref[0, r0:r0 + HEAD_DIM, c * KEY_CHUNK:(c + 1) * KEY_CHUNK]
                carry = step(kch, vch, carry)
        _, l, acc = carry
        oT_scr[hd * HEAD_DIM:(hd + 1) * HEAD_DIM, :] = acc / l

    o_ref[...] = oT_scr[...].T.astype(BF16)


def _attn_call(qT, k, vT, n_batch, seq, cache=None):
    nq = seq // Q_TILE
    nch = seq // CHUNK
    in_specs = [pl.BlockSpec((1, ATTN_WIDTH, Q_TILE), lambda b, i: (b * nq + i, 0, 0)),
                pl.BlockSpec((seq, KV_WIDTH), lambda b, i: (b, 0)),
                pl.BlockSpec((nch, KV_WIDTH, CHUNK), lambda b, i: (b, 0, 0))]
    args = [qT, k, vT]
    if cache is not None:
        ck, cvT = cache
        past = ck.shape[1]
        in_specs += [pl.BlockSpec((1, past, KV_WIDTH), lambda b, i: (b, 0, 0)),
                     pl.BlockSpec((1, KV_WIDTH, past), lambda b, i: (b, 0, 0))]
        args += [ck, cvT]
    return pl.pallas_call(
        functools.partial(_attn_kernel, n_new_chunks=seq // KEY_CHUNK, has_cache=cache is not None),
        out_shape=jax.ShapeDtypeStruct((n_batch * seq, ATTN_WIDTH), BF16),
        grid=(n_batch, nq),
        in_specs=in_specs,
        out_specs=pl.BlockSpec((Q_TILE, ATTN_WIDTH), lambda b, i: (b * nq + i, 0)),
        scratch_shapes=[pltpu.VMEM((ATTN_WIDTH, Q_TILE), F32)],
        compiler_params=pltpu.CompilerParams(dimension_semantics=("parallel", "parallel"),
                                             vmem_limit_bytes=VMEM_LIMIT),
        name="gqa_attention",
    )(*args)


def _log_sigmoid(x):
    return jnp.minimum(x, 0.0) - jnp.log1p(jnp.exp(-jnp.abs(x)))


def _cumsum_lanes(x, tri):
    hi = x.astype(BF16)
    r1 = x - hi.astype(F32)
    mid = r1.astype(BF16)
    lo = (r1 - mid.astype(F32)).astype(BF16)
    return (jnp.dot(hi, tri, preferred_element_type=F32) + jnp.dot(mid, tri, preferred_element_type=F32)
            + jnp.dot(lo, tri, preferred_element_type=F32))


def _tri_masks():
    ri = lax.broadcasted_iota(jnp.int32, (CHUNK, CHUNK), 0)
    ci = lax.broadcasted_iota(jnp.int32, (CHUNK, CHUNK), 1)
    return ri <= ci, ri >= ci


def _gate_rows(g, d, le, ge):
    ig = g[8 * d:8 * d + 4]
    lf = _log_sigmoid(g[8 * d + 4:8 * d + 8])
    tri = (le if d == 0 else ge).astype(BF16)
    return lf, _cumsum_lanes(lf, tri), ig


def _scan_kernel(kTf_ref, vf_ref, gf_ref, kTb_ref, vb_ref, gb_ref, c0_ref, m0_ref,
                 cxf_ref, cxb_ref, mf_ref, mb_ref, cfin_ref, mfin_ref):
    @pl.when(pl.program_id(1) == 0)
    def _():
        cfin_ref[...] = c0_ref[...]
        mfin_ref[...] = m0_ref[...]

    le, ge = _tri_masks()
    ones = jnp.ones((CHUNK, MLSTM_DK), BF16)
    for d, (kT_ref, v_ref, g_ref, cx_ref, mo_ref) in enumerate(
            ((kTf_ref, vf_ref, gf_ref, cxf_ref, mf_ref), (kTb_ref, vb_ref, gb_ref, cxb_ref, mb_ref))):
        lf, b, ig = _gate_rows(g_ref[0], d, le, ge)
        b_last = jnp.sum(lf, axis=1, keepdims=True)
        gt = b_last - b + ig
        a = jnp.max(gt, axis=1, keepdims=True)
        wt = jnp.exp(gt - a)
        m_prev = mfin_ref[0, d][:, 0:1]
        m_new = jnp.maximum(a, b_last + m_prev)
        decay = jnp.exp(b_last + m_prev - m_new)
        gam = jnp.exp(a - m_new)
        mo_ref[0, 0] = jnp.broadcast_to(m_prev, (MLSTM_HEADS, CHUNK))
        mfin_ref[0, d] = jnp.broadcast_to(m_new, (MLSTM_HEADS, CHUNK))
        for h in range(MLSTM_HEADS):
            sl = slice(h * MLSTM_DK, (h + 1) * MLSTM_DK)
            c_prev = cfin_ref[0, d, h]
            cx_ref[0, 0, h] = c_prev.astype(BF16)
            ktw = (kT_ref[0, sl, :].astype(F32) * wt[h:h + 1, :]).astype(BF16)
            vx = jnp.concatenate([v_ref[:, sl], ones], axis=1)
            u = jnp.dot(ktw, vx, preferred_element_type=F32)
            cfin_ref[0, d, h] = decay[h:h + 1, :] * c_prev + gam[h:h + 1, :] * u


def _scan_call(mkT, mv, gates, c0, m0, n_batch, seq):
    nc = seq // CHUNK
    fwd3 = lambda b, c: (b * nc + c, 0, 0)
    bwd3 = lambda b, c: (b * nc + nc - 1 - c, 0, 0)
    fwd2 = lambda b, c: (b * nc + c, 0)
    bwd2 = lambda b, c: (b * nc + nc - 1 - c, 0)
    kT_blk, v_blk, g_blk = (1, MLSTM_WIDTH, CHUNK), (CHUNK, MLSTM_WIDTH), (1, N_GATES, CHUNK)
    cx_blk = (1, 1, MLSTM_HEADS, MLSTM_DK, 2 * MLSTM_DK)
    m_blk = (1, 1, MLSTM_HEADS, CHUNK)
    st_blk = (1, 2, MLSTM_HEADS, MLSTM_DK, 2 * MLSTM_DK)
    sm_blk = (1, 2, MLSTM_HEADS, CHUNK)
    return pl.pallas_call(
        _scan_kernel,
        out_shape=[jax.ShapeDtypeStruct((n_batch, nc) + cx_blk[2:], BF16),
                   jax.ShapeDtypeStruct((n_batch, nc) + cx_blk[2:], BF16),
                   jax.ShapeDtypeStruct((n_batch, nc) + m_blk[2:], F32),
                   jax.ShapeDtypeStruct((n_batch, nc) + m_blk[2:], F32),
                   jax.ShapeDtypeStruct((n_batch,) + st_blk[1:], F32),
                   jax.ShapeDtypeStruct((n_batch,) + sm_blk[1:], F32)],
        grid=(n_batch, nc),
        in_specs=[pl.BlockSpec(kT_blk, fwd3), pl.BlockSpec(v_blk, fwd2), pl.BlockSpec(g_blk, fwd3),
                  pl.BlockSpec(kT_blk, bwd3), pl.BlockSpec(v_blk, bwd2), pl.BlockSpec(g_blk, bwd3),
                  pl.BlockSpec(st_blk, lambda b, c: (b, 0, 0, 0, 0)),
                  pl.BlockSpec(sm_blk, lambda b, c: (b, 0, 0, 0))],
        out_specs=[pl.BlockSpec(cx_blk, lambda b, c: (b, c, 0, 0, 0)),
                   pl.BlockSpec(cx_blk, lambda b, c: (b, nc - 1 - c, 0, 0, 0)),
                   pl.BlockSpec(m_blk, lambda b, c: (b, c, 0, 0)),
                   pl.BlockSpec(m_blk, lambda b, c: (b, nc - 1 - c, 0, 0)),
                   pl.BlockSpec(st_blk, lambda b, c: (b, 0, 0, 0, 0)),
                   pl.BlockSpec(sm_blk, lambda b, c: (b, 0, 0, 0))],
        compiler_params=pltpu.CompilerParams(dimension_semantics=("parallel", "arbitrary"),
                                             vmem_limit_bytes=VMEM_LIMIT),
        name="mlstm_state_scan",
    )(mkT, mv, gates, mkT, mv, gates, c0, m0)


def _mlstm_out_kernel(q_ref, kT_ref, v_ref, g_ref, cxf_ref, cxb_ref, mf_ref, mb_ref, og_ref, nw_ref, o_ref):
    le, ge = _tri_masks()
    g = g_ref[0]
    ones = jnp.ones((CHUNK, MLSTM_DK), BF16)
    rows = []
    for d in range(2):
        lf, b, ig = _gate_rows(g, d, le, ge)
        rows.append((lf, ig - b))
    for h in range(MLSTM_HEADS):
        sl = slice(h * MLSTM_DK, (h + 1) * MLSTM_DK)
        q = q_ref[:, sl]
        vx = jnp.concatenate([v_ref[:, sl], ones], axis=1)
        s = jnp.dot(q, kT_ref[0, sl, :], preferred_element_type=F32)
        h_sum = jnp.zeros((CHUNK, MLSTM_DK), F32)
        for d, (cx_ref, m_ref) in enumerate(((cxf_ref, mf_ref), (cxb_ref, mb_ref))):
            mask = ge if d == 0 else le
            lf, r = rows[d]
            rm = jnp.where(mask, r[h:h + 1, :], NEG_BIG)
            cm = jnp.max(rm, axis=1, keepdims=True)
            dt = jnp.exp(rm - cm)
            b_col = jnp.sum(jnp.where(mask, lf[h:h + 1, :], 0.0), axis=1, keepdims=True)
            m_prev = m_ref[0, 0, h:h + 1, 0:1]
            mm = jnp.maximum(cm, m_prev)
            scale = jnp.exp(cm - mm)
            inter = jnp.exp(m_prev - mm)
            r_sv = jnp.dot((s * dt).astype(BF16), vx, preferred_element_type=F32)
            r_qc = jnp.dot(q, cx_ref[0, 0, h], preferred_element_type=F32)
            mix = scale * r_sv + inter * r_qc
            num = mix[:, :MLSTM_DK]
            den = mix[:, MLSTM_DK:]
            h_sum = h_sum + num / jnp.maximum(jnp.abs(den), jnp.exp(-(b_col + mm)))
        hn = _rms(h_sum) * nw_ref[:, sl]
        o_ref[:, sl] = (hn * og_ref[:, sl].astype(F32)).astype(BF16)


def _mlstm_out_call(mq, mkT, mv, gates, cxf, cxb, mf, mb, og, nw, n_batch, seq):
    nc = seq // CHUNK
    tok = lambda b, c: (b * nc + c, 0)
    chk = lambda b, c: (b * nc + c, 0, 0)
    cx_blk = (1, 1, MLSTM_HEADS, MLSTM_DK, 2 * MLSTM_DK)
    m_blk = (1, 1, MLSTM_HEADS, CHUNK)
    return pl.pallas_call(
        _mlstm_out_kernel,
        out_shape=jax.ShapeDtypeStruct((n_batch * seq, MLSTM_WIDTH), BF16),
        grid=(n_batch, nc),
        in_specs=[pl.BlockSpec((CHUNK, MLSTM_WIDTH), tok),
                  pl.BlockSpec((1, MLSTM_WIDTH, CHUNK), chk),
                  pl.BlockSpec((CHUNK, MLSTM_WIDTH), tok),
                  pl.BlockSpec((1, N_GATES, CHUNK), chk),
                  pl.BlockSpec(cx_blk, lambda b, c: (b, c, 0, 0, 0)),
                  pl.BlockSpec(cx_blk, lambda b, c: (b, c, 0, 0, 0)),
                  pl.BlockSpec(m_blk, lambda b, c: (b, c, 0, 0)),
                  pl.BlockSpec(m_blk, lambda b, c: (b, c, 0, 0)),
                  pl.BlockSpec((CHUNK, MLSTM_WIDTH), tok),
                  pl.BlockSpec((1, MLSTM_WIDTH), lambda b, c: (0, 0))],
        out_specs=pl.BlockSpec((CHUNK, MLSTM_WIDTH), tok),
        compiler_params=pltpu.CompilerParams(dimension_semantics=("parallel", "parallel"),
                                             vmem_limit_bytes=VMEM_LIMIT),
        name="mlstm_chunk_out",
    )(mq, mkT, mv, gates, cxf, cxb, mf, mb, og, nw)


def _ffn_kernel(x_ref, a_ref, hm_ref, mods_ref, wo_ref, n2w_ref, wgu_ref, wd_ref, fw_ref, o_ref):
    mix = (jnp.dot(a_ref[...], wo_ref[0:ATTN_WIDTH, :], preferred_element_type=F32)
           + jnp.dot(hm_ref[...], wo_ref[ATTN_WIDTH:D_MODEL, :], preferred_element_type=F32))
    g1 = mods_ref[0, 2:3, :]
    sh2 = mods_ref[0, 3:4, :]
    sc2 = mods_ref[0, 4:5, :]
    g2 = mods_ref[0, 5:6, :]
    x1 = x_ref[...] + g1 * mix
    h2 = (_rms(x1) * n2w_ref[...] * (1.0 + sc2) + sh2).astype(BF16)
    acc = jnp.zeros(x1.shape, F32)
    for j in range(D_FF // FF_CHUNK):
        c0 = j * FF_CHUNK
        gg = jnp.dot(h2, wgu_ref[:, c0:c0 + FF_CHUNK], preferred_element_type=F32)
        uu = jnp.dot(h2, wgu_ref[:, D_FF + c0:D_FF + c0 + FF_CHUNK], preferred_element_type=F32)
        act = (gg * jax.nn.sigmoid(gg) * uu).astype(BF16)
        acc = acc + jnp.dot(act, wd_ref[c0:c0 + FF_CHUNK, :], preferred_element_type=F32)
    x2 = x1 + g2 * acc
    o_ref[...] = _rms(x2) * fw_ref[...]


def _ffn_call(x2d, attn_o, hm, mods, tiles_per_mod, wo, n2w, wgu, wd, fw):
    T = x2d.shape[0]
    tm = TOKEN_TILE
    if tiles_per_mod is None:
        mod_map = lambda i: (0, 0, 0)
    else:
        mod_map = lambda i: (i // tiles_per_mod, 0, 0)
    return pl.pallas_call(
        _ffn_kernel,
        out_shape=jax.ShapeDtypeStruct((T, D_MODEL), F32),
        grid=(T // tm,),
        in_specs=[pl.BlockSpec((tm, D_MODEL), lambda i: (i, 0)),
                  pl.BlockSpec((tm, ATTN_WIDTH), lambda i: (i, 0)),
                  pl.BlockSpec((tm, MLSTM_WIDTH), lambda i: (i, 0)),
                  pl.BlockSpec((1, 6, D_MODEL), mod_map),
                  _resident(wo.shape), _resident((1, D_MODEL)), _resident(wgu.shape),
                  _resident(wd.shape), _resident((1, D_MODEL))],
        out_specs=pl.BlockSpec((tm, D_MODEL), lambda i: (i, 0)),
        compiler_params=pltpu.CompilerParams(dimension_semantics=("parallel",),
                                             vmem_limit_bytes=VMEM_LIMIT),
        name="out_proj_ffn",
    )(x2d, attn_o, hm, mods, wo, n2w, wgu, wd, fw)


def _rope_tables(seq, gain, scale):
    pos = jnp.arange(seq, dtype=jnp.int32)
    row_ids = (pos // GRID_W).astype(F32)
    col_ids = (pos % GRID_W).astype(F32)
    inv = ROPE_THETA ** (-jnp.arange(0, AXIS_DIM, 2, dtype=F32) / AXIS_DIM)
    ang_r = inv[:, None] * row_ids[None, :]
    ang_c = inv[:, None] * col_ids[None, :]
    cos = jnp.concatenate([jnp.cos(ang_r)] * 2 + [jnp.cos(ang_c)] * 2, axis=0)
    sin = jnp.concatenate([-jnp.sin(ang_r), jnp.sin(ang_r), -jnp.sin(ang_c), jnp.sin(ang_c)], axis=0)
    g = gain.astype(F32) * scale
    g_sw = jnp.concatenate([g[16:32], g[0:16], g[48:64], g[32:48]])
    return g[:, None] * cos, g_sw[:, None] * sin


def _flat_tables(gain, scale, width):
    g = gain.astype(F32) * scale
    return jnp.broadcast_to(g[:, None], (HEAD_DIM, width)), jnp.zeros((HEAD_DIM, width), F32)


def _group(x, mods, tiles_per_mod, seq, weights, tables, tiles_per_table, cache, c0, m0, emit_cache):
    n_batch = x.shape[0]
    x2d = x.reshape(n_batch * seq, D_MODEL)
    outs = _proj_call(x2d, mods, tiles_per_mod, weights["n1w"], weights["wt"], weights["wn"],
                      weights["gbias"], tables, tiles_per_table, emit_cache)
    qT, k, vT, mkT, gates, mq, mv, og = outs[:8]
    attn_o = _attn_call(qT, k, vT, n_batch, seq, cache)
    cxf, cxb, mf, mb, cfin, mfin = _scan_call(mkT, mv, gates, c0, m0, n_batch, seq)
    hm = _mlstm_out_call(mq, mkT, mv, gates, cxf, cxb, mf, mb, og, weights["mnw"], n_batch, seq)
    y = _ffn_call(x2d, attn_o, hm, mods, tiles_per_mod, weights["wo"], weights["n2w"], weights["wgu"],
                  weights["wd"], weights["fw"])
    return y.reshape(n_batch, seq, D_MODEL), outs[8:], cfin, mfin


def kernel(x_prompt, x_sample, cache_k, cache_v, state_C, state_n, state_m, c, c_ctx, w_ada, b_ada,
           norm1_w, w_in, gate_bias, q_norm_w, k_norm_w, mlstm_norm_w, w_out, norm2_w, w_gu, w_down,
           final_norm_w):
    depth = w_ada.shape[0]
    assert depth == 1, "single-layer trunk"
    n_ctx, ctx_len, _ = x_prompt.shape
    n_dec, dec_len, _ = x_sample.shape
    l = 0

    cond = jnp.zeros((16, D_MODEL), F32).at[:n_dec].set(c).at[n_dec].set(c_ctx)
    mods = _mods_call(cond, w_ada[l], b_ada[l]).reshape(16, 6, D_MODEL)

    wi = w_in[l]
    aq, ak, av, mq_w, mk_w, mv_w, mo_w, mg_w = jnp.split(
        wi, [512, 640, 768, 1280, 1792, 2304, 2816], axis=1)
    weights = {
        "wt": jnp.concatenate([aq, ak, av, mk_w, mg_w], axis=1).T.astype(BF16),
        "wn": jnp.concatenate([mq_w, mv_w, mo_w], axis=1).astype(BF16),
        "gbias": gate_bias[l].reshape(N_GATES, 1).astype(F32),
        "n1w": norm1_w[l].reshape(1, D_MODEL),
        "n2w": norm2_w[l].reshape(1, D_MODEL),
        "mnw": mlstm_norm_w[l].reshape(1, MLSTM_WIDTH),
        "wo": w_out[l].astype(BF16),
        "wgu": w_gu[l].astype(BF16),
        "wd": w_down[l].astype(BF16),
        "fw": final_norm_w.reshape(1, D_MODEL),
    }
    q_scale = (HEAD_DIM ** -0.5) * LOG2E

    tabs_ctx = _flat_tables(q_norm_w[l], q_scale, TOKEN_TILE) + _flat_tables(k_norm_w[l], 1.0, TOKEN_TILE)
    c0 = jnp.zeros((n_ctx, 2, MLSTM_HEADS, MLSTM_DK, 2 * MLSTM_DK), F32)
    m0 = jnp.zeros((n_ctx, 2, MLSTM_HEADS, CHUNK), F32)
    y_prompt, (kc, vc), cfin, mfin = _group(x_prompt, mods[n_dec:n_dec + 1], None, ctx_len, weights,
                                            tabs_ctx, None, None, c0, m0, True)
    new_cache_k = kc.reshape(n_ctx, 1, ctx_len, N_KV_HEADS, HEAD_DIM)
    new_cache_v = vc.reshape(n_ctx, 1, ctx_len, N_KV_HEADS, HEAD_DIM)
    new_state_C = cfin[..., :MLSTM_DK][:, None]
    new_state_n = cfin[..., MLSTM_DK][:, None]
    new_state_m = mfin[..., 0][:, None]

    tabs_dec = _rope_tables(dec_len, q_norm_w[l], q_scale) + _rope_tables(dec_len, k_norm_w[l], 1.0)
    past = cache_k.shape[2]
    ck = cache_k[:, l].reshape(n_dec, past, KV_WIDTH).astype(BF16)
    cvT = cache_v[:, l].reshape(n_dec, past, KV_WIDTH).transpose(0, 2, 1).astype(BF16)
    sC = state_C[:, l].astype(F32)
    sn = state_n[:, l].astype(F32)
    c0 = jnp.concatenate([sC, jnp.broadcast_to(sn[..., None], sC.shape)], axis=-1)
    m0 = jnp.broadcast_to(state_m[:, l].astype(F32)[..., None], (n_dec, 2, MLSTM_HEADS, CHUNK))
    y_sample, _, _, _ = _group(x_sample, mods[:n_dec], dec_len // TOKEN_TILE, dec_len, weights,
                               tabs_dec, dec_len // TOKEN_TILE, (ck, cvT), c0, m0, False)

    return (y_prompt, y_sample, new_cache_k, new_cache_v, new_state_C, new_state_n, new_state_m)
```

```python
import functools

import jax
import jax.numpy as jnp
from jax import lax
from jax.experimental import pallas as pl
from jax.experimental.pallas import tpu as pltpu

F32 = jnp.float32
BF16 = jnp.bfloat16

D_MODEL = 1024
HEAD_DIM = 64
N_Q_HEADS = 8
N_KV_HEADS = 2
KV_GROUP = N_Q_HEADS // N_KV_HEADS
ATTN_WIDTH = N_Q_HEADS * HEAD_DIM
KV_WIDTH = N_KV_HEADS * HEAD_DIM
AXIS_DIM = HEAD_DIM // 2
GRID_W = 64
ROPE_THETA = 10000.0
MLSTM_HEADS = 4
MLSTM_DK = 128
MLSTM_WIDTH = MLSTM_HEADS * MLSTM_DK
CHUNK = 128
D_FF = 2816
N_GATES = 4 * MLSTM_HEADS
EPS = 1e-6
LOG2E = 1.4426950408889634
NEG_BIG = -1e30

TOKEN_TILE = 512
Q_TILE = 256
KEY_BLOCK = 256
QK_LOOKAHEAD = 4
FF_CHUNK = 1408
ADA_TILE = 1536
VMEM_LIMIT = 56 * 1024 * 1024

_R_Q, _R_K, _R_V, _R_MK, _R_G, _R_END = 0, 512, 640, 768, 1280, 1296


def _resident(shape):
    nd = len(shape)
    return pl.BlockSpec(shape, lambda *_: (0,) * nd, pipeline_mode=pl.Buffered(1))


def _mods_kernel(cond_ref, w_ref, b_ref, o_ref):
    c = cond_ref[...]
    s = c * jax.nn.sigmoid(c)
    o_ref[...] = jnp.dot(s, w_ref[...], preferred_element_type=F32) + b_ref[...]


def _mods_call(cond, w_ada, b_ada):
    rows, n_out = cond.shape[0], w_ada.shape[1]
    return pl.pallas_call(
        _mods_kernel,
        out_shape=jax.ShapeDtypeStruct((rows, n_out), F32),
        grid=(n_out // ADA_TILE,),
        in_specs=[pl.BlockSpec((rows, D_MODEL), lambda j: (0, 0)),
                  pl.BlockSpec((D_MODEL, ADA_TILE), lambda j: (0, j)),
                  pl.BlockSpec((1, ADA_TILE), lambda j: (0, j))],
        out_specs=pl.BlockSpec((rows, ADA_TILE), lambda j: (0, j)),
        compiler_params=pltpu.CompilerParams(dimension_semantics=("parallel",),
                                             vmem_limit_bytes=VMEM_LIMIT),
        name="adaln_mods",
    )(cond, w_ada, b_ada.reshape(1, n_out))


def _rms(x):
    return x * lax.rsqrt(jnp.mean(x * x, axis=-1, keepdims=True) + EPS)


def _proj_kernel(x_ref, mods_ref, n1w_ref, wt_ref, wn_ref, gb_ref, tqa_ref, tqb_ref, tka_ref, tkb_ref,
                 qT_ref, k_ref, vT_ref, mkT_ref, g_ref, mq_ref, mv_ref, og_ref, *cache_refs):
    tm = x_ref.shape[0]
    x = x_ref[...]
    sh = mods_ref[0, 0:1, :]
    sc = mods_ref[0, 1:2, :]
    h = (_rms(x) * n1w_ref[...] * (1.0 + sc) + sh).astype(BF16)

    def proj_t(r0, r1):
        return lax.dot_general(wt_ref[r0:r1, :], h, (((1,), (1,)), ((), ())), preferred_element_type=F32)

    n_qk = N_Q_HEADS + N_KV_HEADS
    qk = proj_t(_R_Q, _R_V).reshape(n_qk, HEAD_DIM, tm)
    qk = qk * lax.rsqrt(jnp.sum(qk * qk, axis=1, keepdims=True) * (1.0 / HEAD_DIM) + EPS)
    sw = jnp.concatenate([qk[:, 16:32], qk[:, 0:16], qk[:, 48:64], qk[:, 32:48]], axis=1)
    q = qk[:N_Q_HEADS] * tqa_ref[...][None] + sw[:N_Q_HEADS] * tqb_ref[...][None]
    q = q.reshape(ATTN_WIDTH, tm).astype(BF16)
    for j in range(tm // Q_TILE):
        qT_ref[j] = q[:, j * Q_TILE:(j + 1) * Q_TILE]
    k = qk[N_Q_HEADS:] * tka_ref[...][None] + sw[N_Q_HEADS:] * tkb_ref[...][None]
    kt = k.reshape(KV_WIDTH, tm).T
    k_ref[...] = kt.astype(BF16)

    vt = proj_t(_R_V, _R_MK)
    vtb = vt.astype(BF16)
    for j in range(tm // KEY_BLOCK):
        vT_ref[j] = vtb[:, j * KEY_BLOCK:(j + 1) * KEY_BLOCK]
    if cache_refs:
        kc_ref, vc_ref = cache_refs
        kc_ref[...] = kt
        vc_ref[...] = vt.T

    mk = (proj_t(_R_MK, _R_G) * (MLSTM_DK ** -0.5)).astype(BF16)
    gates = proj_t(_R_G, _R_END) + gb_ref[...]
    for j in range(tm // CHUNK):
        mkT_ref[j] = mk[:, j * CHUNK:(j + 1) * CHUNK]
        g_ref[j] = gates[:, j * CHUNK:(j + 1) * CHUNK]

    mq_ref[...] = jnp.dot(h, wn_ref[:, 0:512], preferred_element_type=F32).astype(BF16)
    mv_ref[...] = jnp.dot(h, wn_ref[:, 512:1024], preferred_element_type=F32).astype(BF16)
    mo = jnp.dot(h, wn_ref[:, 1024:1536], preferred_element_type=F32)
    og_ref[...] = jax.nn.sigmoid(mo).astype(BF16)


def _proj_call(x2d, mods, tiles_per_mod, n1w, wt, wn, gbias, tables, tiles_per_table, emit_cache):
    T = x2d.shape[0]
    tm = TOKEN_TILE
    n_tiles = T // tm
    if tiles_per_mod is None:
        mod_map = lambda i: (0, 0, 0)
    else:
        mod_map = lambda i: (i // tiles_per_mod, 0, 0)
    if tiles_per_table is None:
        tab_map = lambda i: (0, 0)
    else:
        tab_map = lambda i: (0, i % tiles_per_table)
    tab_spec = pl.BlockSpec((HEAD_DIM, tm), tab_map)
    out_shape = [
        jax.ShapeDtypeStruct((T // Q_TILE, ATTN_WIDTH, Q_TILE), BF16),
        jax.ShapeDtypeStruct((T, KV_WIDTH), BF16),
        jax.ShapeDtypeStruct((T // KEY_BLOCK, KV_WIDTH, KEY_BLOCK), BF16),
        jax.ShapeDtypeStruct((T // CHUNK, MLSTM_WIDTH, CHUNK), BF16),
        jax.ShapeDtypeStruct((T // CHUNK, N_GATES, CHUNK), F32),
        jax.ShapeDtypeStruct((T, MLSTM_WIDTH), BF16),
        jax.ShapeDtypeStruct((T, MLSTM_WIDTH), BF16),
        jax.ShapeDtypeStruct((T, MLSTM_WIDTH), BF16),
    ]
    out_specs = [
        pl.BlockSpec((tm // Q_TILE, ATTN_WIDTH, Q_TILE), lambda i: (i, 0, 0)),
        pl.BlockSpec((tm, KV_WIDTH), lambda i: (i, 0)),
        pl.BlockSpec((tm // KEY_BLOCK, KV_WIDTH, KEY_BLOCK), lambda i: (i, 0, 0)),
        pl.BlockSpec((tm // CHUNK, MLSTM_WIDTH, CHUNK), lambda i: (i, 0, 0)),
        pl.BlockSpec((tm // CHUNK, N_GATES, CHUNK), lambda i: (i, 0, 0)),
        pl.BlockSpec((tm, MLSTM_WIDTH), lambda i: (i, 0)),
        pl.BlockSpec((tm, MLSTM_WIDTH), lambda i: (i, 0)),
        pl.BlockSpec((tm, MLSTM_WIDTH), lambda i: (i, 0)),
    ]
    if emit_cache:
        out_shape += [jax.ShapeDtypeStruct((T, KV_WIDTH), F32)] * 2
        out_specs += [pl.BlockSpec((tm, KV_WIDTH), lambda i: (i, 0))] * 2
    return pl.pallas_call(
        _proj_kernel,
        out_shape=out_shape,
        grid=(n_tiles,),
        in_specs=[pl.BlockSpec((tm, D_MODEL), lambda i: (i, 0)),
                  pl.BlockSpec((1, 6, D_MODEL), mod_map),
                  _resident((1, D_MODEL)),
                  _resident(wt.shape),
                  _resident(wn.shape),
                  _resident((N_GATES, 1)),
                  tab_spec, tab_spec, tab_spec, tab_spec],
        out_specs=out_specs,
        compiler_params=pltpu.CompilerParams(dimension_semantics=("parallel",),
                                             vmem_limit_bytes=VMEM_LIMIT),
        name="norm1_in_proj",
    )(x2d, mods, n1w, wt, wn, gbias, *tables)


def _attn_kernel(qT_ref, k_ref, vT_ref, o_ref, qp_scr, m_scr, l_scr, acc_scr, s_scr, *, n_blocks):
    tq = qT_ref.shape[2]

    for hd in range(N_Q_HEADS):
        qg = qT_ref[0, hd * HEAD_DIM:(hd + 1) * HEAD_DIM, :]
        zero = jnp.zeros_like(qg)
        qp_scr[hd] = jnp.concatenate([qg, zero] if hd < KV_GROUP else [zero, qg], axis=0)
    m_scr[...] = jnp.full(m_scr.shape, NEG_BIG, F32)
    l_scr[...] = jnp.zeros(l_scr.shape, F32)
    acc_scr[...] = jnp.zeros(acc_scr.shape, F32)

    def scores(c, hd):
        off = pl.multiple_of(c * KEY_BLOCK, KEY_BLOCK)
        return jnp.dot(k_ref[pl.ds(off, KEY_BLOCK), :], qp_scr[hd], preferred_element_type=F32)

    for hd in range(QK_LOOKAHEAD):
        s_scr[hd] = scores(0, hd)

    def body(c, _):
        c_next = jnp.minimum(c + 1, n_blocks - 1)
        s_local = {}
        for hd in range(N_Q_HEADS):
            s = s_scr[hd] if hd < QK_LOOKAHEAD else s_local.pop(hd)
            ahead = hd + QK_LOOKAHEAD
            if ahead < N_Q_HEADS:
                s_local[ahead] = scores(c, ahead)
            else:
                s_scr[ahead - N_Q_HEADS] = scores(c_next, ahead - N_Q_HEADS)
            r0 = (hd // KV_GROUP) * HEAD_DIM
            m = m_scr[hd]
            m_new = jnp.maximum(m, jnp.max(s, axis=0, keepdims=True))
            p = jnp.exp2(s - m_new)
            alpha = jnp.exp2(m - m_new)
            l_scr[hd] = alpha * l_scr[hd] + jnp.sum(p, axis=0, keepdims=True)
            pv = jnp.dot(vT_ref[c, r0:r0 + HEAD_DIM, :], p.astype(BF16), preferred_element_type=F32)
            acc_scr[hd] = alpha * acc_scr[hd] + pv
            m_scr[hd] = m_new
        return 0

    lax.fori_loop(0, n_blocks, body, 0)
    out = acc_scr[...] / l_scr[...]
    o_ref[...] = out.reshape(ATTN_WIDTH, tq).T.astype(BF16)


def _attn_call(qT, k, vT, n_batch, seq, n_keys):
    nq = seq // Q_TILE
    nkb = n_keys // KEY_BLOCK
    return pl.pallas_call(
        functools.partial(_attn_kernel, n_blocks=nkb),
        out_shape=jax.ShapeDtypeStruct((n_batch * seq, ATTN_WIDTH), BF16),
        grid=(n_batch, nq),
        in_specs=[pl.BlockSpec((1, ATTN_WIDTH, Q_TILE), lambda b, i: (b * nq + i, 0, 0)),
                  pl.BlockSpec((n_keys, KV_WIDTH), lambda b, i: (b, 0)),
                  pl.BlockSpec((nkb, KV_WIDTH, KEY_BLOCK), lambda b, i: (b, 0, 0))],
        out_specs=pl.BlockSpec((Q_TILE, ATTN_WIDTH), lambda b, i: (b * nq + i, 0)),
        scratch_shapes=[pltpu.VMEM((N_Q_HEADS, KV_WIDTH, Q_TILE), BF16),
                        pltpu.VMEM((N_Q_HEADS, 1, Q_TILE), F32),
                        pltpu.VMEM((N_Q_HEADS, 1, Q_TILE), F32),
                        pltpu.VMEM((N_Q_HEADS, HEAD_DIM, Q_TILE), F32),
                        pltpu.VMEM((QK_LOOKAHEAD, KEY_BLOCK, Q_TILE), F32)],
        compiler_params=pltpu.CompilerParams(dimension_semantics=("parallel", "parallel"),
                                             vmem_limit_bytes=VMEM_LIMIT),
        name="gqa_attention",
    )(qT, k, vT)


def _log_sigmoid(x):
    return jnp.minimum(x, 0.0) - jnp.log1p(jnp.exp(-jnp.abs(x)))


def _cumsum_lanes(x, tri):
    hi = x.astype(BF16)
    r1 = x - hi.astype(F32)
    mid = r1.astype(BF16)
    lo = (r1 - mid.astype(F32)).astype(BF16)
    return (jnp.dot(hi, tri, preferred_element_type=F32) + jnp.dot(mid, tri, preferred_element_type=F32)
            + jnp.dot(lo, tri, preferred_element_type=F32))


def _tri_masks():
    ri = lax.broadcasted_iota(jnp.int32, (CHUNK, CHUNK), 0)
    ci = lax.broadcasted_iota(jnp.int32, (CHUNK, CHUNK), 1)
    return ri <= ci, ri >= ci


def _gate_rows(g, d, le, ge):
    ig = g[8 * d:8 * d + 4]
    lf = _log_sigmoid(g[8 * d + 4:8 * d + 8])
    tri = (le if d == 0 else ge).astype(BF16)
    return lf, _cumsum_lanes(lf, tri), ig


def _scan_kernel(kTf_ref, vf_ref, gf_ref, kTb_ref, vb_ref, gb_ref, c0_ref, m0_ref,
                 cxf_ref, cxb_ref, mf_ref, mb_ref, cfin_ref, mfin_ref):
    @pl.when(pl.program_id(1) == 0)
    def _():
        cfin_ref[...] = c0_ref[...]
        mfin_ref[...] = m0_ref[...]

    le, ge = _tri_masks()
    ones = jnp.ones((CHUNK, MLSTM_DK), BF16)
    for d, (kT_ref, v_ref, g_ref, cx_ref, mo_ref) in enumerate(
            ((kTf_ref, vf_ref, gf_ref, cxf_ref, mf_ref), (kTb_ref, vb_ref, gb_ref, cxb_ref, mb_ref))):
        lf, b, ig = _gate_rows(g_ref[0], d, le, ge)
        b_last = jnp.sum(lf, axis=1, keepdims=True)
        gt = b_last - b + ig
        a = jnp.max(gt, axis=1, keepdims=True)
        wt = jnp.exp(gt - a)
        m_prev = mfin_ref[0, d][:, 0:1]
        m_new = jnp.maximum(a, b_last + m_prev)
        decay = jnp.exp(b_last + m_prev - m_new)
        gam = jnp.exp(a - m_new)
        mo_ref[0, 0] = jnp.broadcast_to(m_prev, (MLSTM_HEADS, CHUNK))
        mfin_ref[0, d] = jnp.broadcast_to(m_new, (MLSTM_HEADS, CHUNK))
        for h in range(MLSTM_HEADS):
            sl = slice(h * MLSTM_DK, (h + 1) * MLSTM_DK)
            c_prev = cfin_ref[0, d, h]
            cx_ref[0, 0, h] = c_prev.astype(BF16)
            ktw = (kT_ref[0, sl, :].astype(F32) * wt[h:h + 1, :]).astype(BF16)
            vx = jnp.concatenate([v_ref[:, sl], ones], axis=1)
            u = jnp.dot(ktw, vx, preferred_element_type=F32)
            cfin_ref[0, d, h] = decay[h:h + 1, :] * c_prev + gam[h:h + 1, :] * u


def _scan_call(mkT, mv, gates, c0, m0, n_batch, seq):
    nc = seq // CHUNK
    fwd3 = lambda b, c: (b * nc + c, 0, 0)
    bwd3 = lambda b, c: (b * nc + nc - 1 - c, 0, 0)
    fwd2 = lambda b, c: (b * nc + c, 0)
    bwd2 = lambda b, c: (b * nc + nc - 1 - c, 0)
    kT_blk, v_blk, g_blk = (1, MLSTM_WIDTH, CHUNK), (CHUNK, MLSTM_WIDTH), (1, N_GATES, CHUNK)
    cx_blk = (1, 1, MLSTM_HEADS, MLSTM_DK, 2 * MLSTM_DK)
    m_blk = (1, 1, MLSTM_HEADS, CHUNK)
    st_blk = (1, 2, MLSTM_HEADS, MLSTM_DK, 2 * MLSTM_DK)
    sm_blk = (1, 2, MLSTM_HEADS, CHUNK)
    return pl.pallas_call(
        _scan_kernel,
        out_shape=[jax.ShapeDtypeStruct((n_batch, nc) + cx_blk[2:], BF16),
                   jax.ShapeDtypeStruct((n_batch, nc) + cx_blk[2:], BF16),
                   jax.ShapeDtypeStruct((n_batch, nc) + m_blk[2:], F32),
                   jax.ShapeDtypeStruct((n_batch, nc) + m_blk[2:], F32),
                   jax.ShapeDtypeStruct((n_batch,) + st_blk[1:], F32),
                   jax.ShapeDtypeStruct((n_batch,) + sm_blk[1:], F32)],
        grid=(n_batch, nc),
        in_specs=[pl.BlockSpec(kT_blk, fwd3), pl.BlockSpec(v_blk, fwd2), pl.BlockSpec(g_blk, fwd3),
                  pl.BlockSpec(kT_blk, bwd3), pl.BlockSpec(v_blk, bwd2), pl.BlockSpec(g_blk, bwd3),
                  pl.BlockSpec(st_blk, lambda b, c: (b, 0, 0, 0, 0)),
                  pl.BlockSpec(sm_blk, lambda b, c: (b, 0, 0, 0))],
        out_specs=[pl.BlockSpec(cx_blk, lambda b, c: (b, c, 0, 0, 0)),
                   pl.BlockSpec(cx_blk, lambda b, c: (b, nc - 1 - c, 0, 0, 0)),
                   pl.BlockSpec(m_blk, lambda b, c: (b, c, 0, 0)),
                   pl.BlockSpec(m_blk, lambda b, c: (b, nc - 1 - c, 0, 0)),
                   pl.BlockSpec(st_blk, lambda b, c: (b, 0, 0, 0, 0)),
                   pl.BlockSpec(sm_blk, lambda b, c: (b, 0, 0, 0))],
        compiler_params=pltpu.CompilerParams(dimension_semantics=("parallel", "arbitrary"),
                                             vmem_limit_bytes=VMEM_LIMIT),
        name="mlstm_state_scan",
    )(mkT, mv, gates, mkT, mv, gates, c0, m0)


def _mlstm_out_kernel(q_ref, kT_ref, v_ref, g_ref, cxf_ref, cxb_ref, mf_ref, mb_ref, og_ref, nw_ref, o_ref):
    le, ge = _tri_masks()
    g = g_ref[0]
    ones = jnp.ones((CHUNK, MLSTM_DK), BF16)
    rows = []
    for d in range(2):
        lf, b, ig = _gate_rows(g, d, le, ge)
        rows.append((lf, ig - b))
    for h in range(MLSTM_HEADS):
        sl = slice(h * MLSTM_DK, (h + 1) * MLSTM_DK)
        q = q_ref[:, sl]
        vx = jnp.concatenate([v_ref[:, sl], ones], axis=1)
        s = jnp.dot(q, kT_ref[0, sl, :], preferred_element_type=F32)
        h_sum = jnp.zeros((CHUNK, MLSTM_DK), F32)
        for d, (cx_ref, m_ref) in enumerate(((cxf_ref, mf_ref), (cxb_ref, mb_ref))):
            mask = ge if d == 0 else le
            lf, r = rows[d]
            rm = jnp.where(mask, r[h:h + 1, :], NEG_BIG)
            cm = jnp.max(rm, axis=1, keepdims=True)
            dt = jnp.exp(rm - cm)
            b_col = jnp.sum(jnp.where(mask, lf[h:h + 1, :], 0.0), axis=1, keepdims=True)
            m_prev = m_ref[0, 0, h:h + 1, 0:1]
            mm = jnp.maximum(cm, m_prev)
            scale = jnp.exp(cm - mm)
            inter = jnp.exp(m_prev - mm)
            r_sv = jnp.dot((s * dt).astype(BF16), vx, preferred_element_type=F32)
            r_qc = jnp.dot(q, cx_ref[0, 0, h], preferred_element_type=F32)
            mix = scale * r_sv + inter * r_qc
            num = mix[:, :MLSTM_DK]
            den = mix[:, MLSTM_DK:]
            h_sum = h_sum + num / jnp.maximum(jnp.abs(den), jnp.exp(-(b_col + mm)))
        hn = _rms(h_sum) * nw_ref[:, sl]
        o_ref[:, sl] = (hn * og_ref[:, sl].astype(F32)).astype(BF16)


def _mlstm_out_call(mq, mkT, mv, gates, cxf, cxb, mf, mb, og, nw, n_batch, seq):
    nc = seq // CHUNK
    tok = lambda b, c: (b * nc + c, 0)
    chk = lambda b, c: (b * nc + c, 0, 0)
    cx_blk = (1, 1, MLSTM_HEADS, MLSTM_DK, 2 * MLSTM_DK)
    m_blk = (1, 1, MLSTM_HEADS, CHUNK)
    return pl.pallas_call(
        _mlstm_out_kernel,
        out_shape=jax.ShapeDtypeStruct((n_batch * seq, MLSTM_WIDTH), BF16),
        grid=(n_batch, nc),
        in_specs=[pl.BlockSpec((CHUNK, MLSTM_WIDTH), tok),
                  pl.BlockSpec((1, MLSTM_WIDTH, CHUNK), chk),
                  pl.BlockSpec((CHUNK, MLSTM_WIDTH), tok),
                  pl.BlockSpec((1, N_GATES, CHUNK), chk),
                  pl.BlockSpec(cx_blk, lambda b, c: (b, c, 0, 0, 0)),
                  pl.BlockSpec(cx_blk, lambda b, c: (b, c, 0, 0, 0)),
                  pl.BlockSpec(m_blk, lambda b, c: (b, c, 0, 0)),
                  pl.BlockSpec(m_blk, lambda b, c: (b, c, 0, 0)),
                  pl.BlockSpec((CHUNK, MLSTM_WIDTH), tok),
                  pl.BlockSpec((1, MLSTM_WIDTH), lambda b, c: (0, 0))],
        out_specs=pl.BlockSpec((CHUNK, MLSTM_WIDTH), tok),
        compiler_params=pltpu.CompilerParams(dimension_semantics=("parallel", "parallel"),
                                             vmem_limit_bytes=VMEM_LIMIT),
        name="mlstm_chunk_out",
    )(mq, mkT, mv, gates, cxf, cxb, mf, mb, og, nw)


def _ffn_kernel(x_ref, a_ref, hm_ref, mods_ref, wo_ref, n2w_ref, wgu_ref, wd_ref, fw_ref, o_ref):
    mix = (jnp.dot(a_ref[...], wo_ref[0:ATTN_WIDTH, :], preferred_element_type=F32)
           + jnp.dot(hm_ref[...], wo_ref[ATTN_WIDTH:D_MODEL, :], preferred_element_type=F32))
    g1 = mods_ref[0, 2:3, :]
    sh2 = mods_ref[0, 3:4, :]
    sc2 = mods_ref[0, 4:5, :]
    g2 = mods_ref[0, 5:6, :]
    x1 = x_ref[...] + g1 * mix
    h2 = (_rms(x1) * n2w_ref[...] * (1.0 + sc2) + sh2).astype(BF16)
    acc = jnp.zeros(x1.shape, F32)
    for j in range(D_FF // FF_CHUNK):
        c0 = j * FF_CHUNK
        gg = jnp.dot(h2, wgu_ref[:, c0:c0 + FF_CHUNK], preferred_element_type=F32)
        uu = jnp.dot(h2, wgu_ref[:, D_FF + c0:D_FF + c0 + FF_CHUNK], preferred_element_type=F32)
        act = (gg * jax.nn.sigmoid(gg) * uu).astype(BF16)
        acc = acc + jnp.dot(act, wd_ref[c0:c0 + FF_CHUNK, :], preferred_element_type=F32)
    x2 = x1 + g2 * acc
    o_ref[...] = _rms(x2) * fw_ref[...]


def _ffn_call(x2d, attn_o, hm, mods, tiles_per_mod, wo, n2w, wgu, wd, fw):
    T = x2d.shape[0]
    tm = TOKEN_TILE
    if tiles_per_mod is None:
        mod_map = lambda i: (0, 0, 0)
    else:
        mod_map = lambda i: (i // tiles_per_mod, 0, 0)
    return pl.pallas_call(
        _ffn_kernel,
        out_shape=jax.ShapeDtypeStruct((T, D_MODEL), F32),
        grid=(T // tm,),
        in_specs=[pl.BlockSpec((tm, D_MODEL), lambda i: (i, 0)),
                  pl.BlockSpec((tm, ATTN_WIDTH), lambda i: (i, 0)),
                  pl.BlockSpec((tm, MLSTM_WIDTH), lambda i: (i, 0)),
                  pl.BlockSpec((1, 6, D_MODEL), mod_map),
                  _resident(wo.shape), _resident((1, D_MODEL)), _resident(wgu.shape),
                  _resident(wd.shape), _resident((1, D_MODEL))],
        out_specs=pl.BlockSpec((tm, D_MODEL), lambda i: (i, 0)),
        compiler_params=pltpu.CompilerParams(dimension_semantics=("parallel",),
                                             vmem_limit_bytes=VMEM_LIMIT),
        name="out_proj_ffn",
    )(x2d, attn_o, hm, mods, wo, n2w, wgu, wd, fw)


def _rope_tables(seq, gain, scale):
    pos = jnp.arange(seq, dtype=jnp.int32)
    row_ids = (pos // GRID_W).astype(F32)
    col_ids = (pos % GRID_W).astype(F32)
    inv = ROPE_THETA ** (-jnp.arange(0, AXIS_DIM, 2, dtype=F32) / AXIS_DIM)
    ang_r = inv[:, None] * row_ids[None, :]
    ang_c = inv[:, None] * col_ids[None, :]
    cos = jnp.concatenate([jnp.cos(ang_r)] * 2 + [jnp.cos(ang_c)] * 2, axis=0)
    sin = jnp.concatenate([-jnp.sin(ang_r), jnp.sin(ang_r), -jnp.sin(ang_c), jnp.sin(ang_c)], axis=0)
    g = gain.astype(F32) * scale
    g_sw = jnp.concatenate([g[16:32], g[0:16], g[48:64], g[32:48]])
    return g[:, None] * cos, g_sw[:, None] * sin


def _flat_tables(gain, scale, width):
    g = gain.astype(F32) * scale
    return jnp.broadcast_to(g[:, None], (HEAD_DIM, width)), jnp.zeros((HEAD_DIM, width), F32)


def _group(x, mods, tiles_per_mod, seq, weights, tables, tiles_per_table, cache, c0, m0, emit_cache):
    n_batch = x.shape[0]
    x2d = x.reshape(n_batch * seq, D_MODEL)
    outs = _proj_call(x2d, mods, tiles_per_mod, weights["n1w"], weights["wt"], weights["wn"],
                      weights["gbias"], tables, tiles_per_table, emit_cache)
    qT, k, vT, mkT, gates, mq, mv, og = outs[:8]
    n_keys = seq
    if cache is not None:
        ck, cvT = cache
        past = ck.shape[1]
        n_keys = seq + past
        k = jnp.concatenate([k.reshape(n_batch, seq, KV_WIDTH), ck], axis=1).reshape(n_batch * n_keys, KV_WIDTH)
        vT = jnp.concatenate([vT.reshape(n_batch, seq // KEY_BLOCK, KV_WIDTH, KEY_BLOCK), cvT], axis=1)
        vT = vT.reshape(n_batch * n_keys // KEY_BLOCK, KV_WIDTH, KEY_BLOCK)
    attn_o = _attn_call(qT, k, vT, n_batch, seq, n_keys)
    cxf, cxb, mf, mb, cfin, mfin = _scan_call(mkT, mv, gates, c0, m0, n_batch, seq)
    hm = _mlstm_out_call(mq, mkT, mv, gates, cxf, cxb, mf, mb, og, weights["mnw"], n_batch, seq)
    y = _ffn_call(x2d, attn_o, hm, mods, tiles_per_mod, weights["wo"], weights["n2w"], weights["wgu"],
                  weights["wd"], weights["fw"])
    return y.reshape(n_batch, seq, D_MODEL), outs[8:], cfin, mfin


def kernel(x_prompt, x_sample, cache_k, cache_v, state_C, state_n, state_m, c, c_ctx, w_ada, b_ada,
           norm1_w, w_in, gate_bias, q_norm_w, k_norm_w, mlstm_norm_w, w_out, norm2_w, w_gu, w_down,
           final_norm_w):
    depth = w_ada.shape[0]
    assert depth == 1, "single-layer trunk"
    n_ctx, ctx_len, _ = x_prompt.shape
    n_dec, dec_len, _ = x_sample.shape
    l = 0

    cond = jnp.zeros((16, D_MODEL), F32).at[:n_dec].set(c).at[n_dec].set(c_ctx)
    mods = _mods_call(cond, w_ada[l], b_ada[l]).reshape(16, 6, D_MODEL)

    wi = w_in[l]
    aq, ak, av, mq_w, mk_w, mv_w, mo_w, mg_w = jnp.split(
        wi, [512, 640, 768, 1280, 1792, 2304, 2816], axis=1)
    weights = {
        "wt": jnp.concatenate([aq, ak, av, mk_w, mg_w], axis=1).T.astype(BF16),
        "wn": jnp.concatenate([mq_w, mv_w, mo_w], axis=1).astype(BF16),
        "gbias": gate_bias[l].reshape(N_GATES, 1).astype(F32),
        "n1w": norm1_w[l].reshape(1, D_MODEL),
        "n2w": norm2_w[l].reshape(1, D_MODEL),
        "mnw": mlstm_norm_w[l].reshape(1, MLSTM_WIDTH),
        "wo": w_out[l].astype(BF16),
        "wgu": w_gu[l].astype(BF16),
        "wd": w_down[l].astype(BF16),
        "fw": final_norm_w.reshape(1, D_MODEL),
    }
    q_scale = (HEAD_DIM ** -0.5) * LOG2E

    tabs_ctx = _flat_tables(q_norm_w[l], q_scale, TOKEN_TILE) + _flat_tables(k_norm_w[l], 1.0, TOKEN_TILE)
    c0 = jnp.zeros((n_ctx, 2, MLSTM_HEADS, MLSTM_DK, 2 * MLSTM_DK), F32)
    m0 = jnp.zeros((n_ctx, 2, MLSTM_HEADS, CHUNK), F32)
    y_prompt, (kc, vc), cfin, mfin = _group(x_prompt, mods[n_dec:n_dec + 1], None, ctx_len, weights,
                                            tabs_ctx, None, None, c0, m0, True)
    new_cache_k = kc.reshape(n_ctx, 1, ctx_len, N_KV_HEADS, HEAD_DIM)
    new_cache_v = vc.reshape(n_ctx, 1, ctx_len, N_KV_HEADS, HEAD_DIM)
    new_state_C = cfin[..., :MLSTM_DK][:, None]
    new_state_n = cfin[..., MLSTM_DK][:, None]
    new_state_m = mfin[..., 0][:, None]

    tabs_dec = _rope_tables(dec_len, q_norm_w[l], q_scale) + _rope_tables(dec_len, k_norm_w[l], 1.0)
    past = cache_k.shape[2]
    ck = cache_k[:, l].reshape(n_dec, past, KV_WIDTH).astype(BF16)
    cvT = cache_v[:, l].reshape(n_dec, past // KEY_BLOCK, KEY_BLOCK, KV_WIDTH).transpose(0, 1, 3, 2).astype(BF16)
    sC = state_C[:, l].astype(F32)
    sn = state_n[:, l].astype(F32)
    c0 = jnp.concatenate([sC, jnp.broadcast_to(sn[..., None], sC.shape)], axis=-1)
    m0 = jnp.broadcast_to(state_m[:, l].astype(F32)[..., None], (n_dec, 2, MLSTM_HEADS, CHUNK))
    y_sample, _, _, _ = _group(x_sample, mods[:n_dec], dec_len // TOKEN_TILE, dec_len, weights,
                               tabs_dec, dec_len // TOKEN_TILE, (ck, cvT), c0, m0, False)

    return (y_prompt, y_sample, new_cache_k, new_cache_v, new_state_C, new_state_n, new_state_m)
```

```python
import functools

import jax
import jax.numpy as jnp
from jax import lax
from jax.experimental import pallas as pl
from jax.experimental.pallas import tpu as pltpu

F32 = jnp.float32
BF16 = jnp.bfloat16

D_MODEL = 1024
HEAD_DIM = 64
N_Q_HEADS = 8
N_KV_HEADS = 2
KV_GROUP = N_Q_HEADS // N_KV_HEADS
ATTN_WIDTH = N_Q_HEADS * HEAD_DIM
KV_WIDTH = N_KV_HEADS * HEAD_DIM
AXIS_DIM = HEAD_DIM // 2
GRID_W = 64
ROPE_THETA = 10000.0
MLSTM_HEADS = 4
MLSTM_DK = 128
MLSTM_WIDTH = MLSTM_HEADS * MLSTM_DK
CHUNK = 128
D_FF = 2816
N_GATES = 4 * MLSTM_HEADS
EPS = 1e-6
LOG2E = 1.4426950408889634
NEG_BIG = -1e30

TOKEN_TILE = 512
Q_TILE = 256
KEY_BLOCK = 256
QK_LOOKAHEAD = 4
Q_TILES_PER_STEP = 2
V_ROWS = 80
SCAN_CHUNKS = 4
OUT_CHUNKS = 2
FF_CHUNK = 1408
ADA_TILE = 1536
VMEM_LIMIT = 56 * 1024 * 1024

_R_Q, _R_K, _R_V, _R_MK, _R_G, _R_END = 0, 512, 640, 768, 1280, 1296


def _resident(shape):
    nd = len(shape)
    return pl.BlockSpec(shape, lambda *_: (0,) * nd, pipeline_mode=pl.Buffered(1))


def _mods_kernel(cond_ref, w_ref, b_ref, o_ref):
    c = cond_ref[...]
    s = c * jax.nn.sigmoid(c)
    o_ref[...] = jnp.dot(s, w_ref[...], preferred_element_type=F32) + b_ref[...]


def _mods_call(cond, w_ada, b_ada):
    rows, n_out = cond.shape[0], w_ada.shape[1]
    return pl.pallas_call(
        _mods_kernel,
        out_shape=jax.ShapeDtypeStruct((rows, n_out), F32),
        grid=(n_out // ADA_TILE,),
        in_specs=[pl.BlockSpec((rows, D_MODEL), lambda j: (0, 0)),
                  pl.BlockSpec((D_MODEL, ADA_TILE), lambda j: (0, j)),
                  pl.BlockSpec((1, ADA_TILE), lambda j: (0, j))],
        out_specs=pl.BlockSpec((rows, ADA_TILE), lambda j: (0, j)),
        compiler_params=pltpu.CompilerParams(dimension_semantics=("parallel",),
                                             vmem_limit_bytes=VMEM_LIMIT),
        name="adaln_mods",
    )(cond, w_ada, b_ada.reshape(1, n_out))


def _rms(x):
    return x * lax.rsqrt(jnp.mean(x * x, axis=-1, keepdims=True) + EPS)


def _proj_kernel(x_ref, mods_ref, n1w_ref, wt_ref, wn_ref, gb_ref, tqa_ref, tqb_ref, tka_ref, tkb_ref,
                 qT_ref, k_ref, vT_ref, mkT_ref, g_ref, mq_ref, mv_ref, og_ref, *cache_refs):
    tm = x_ref.shape[0]
    x = x_ref[...]
    sh = mods_ref[0, 0:1, :]
    sc = mods_ref[0, 1:2, :]
    h = (_rms(x) * n1w_ref[...] * (1.0 + sc) + sh).astype(BF16)

    def proj_t(r0, r1):
        return lax.dot_general(wt_ref[r0:r1, :], h, (((1,), (1,)), ((), ())), preferred_element_type=F32)

    n_qk = N_Q_HEADS + N_KV_HEADS
    qk = proj_t(_R_Q, _R_V).reshape(n_qk, HEAD_DIM, tm)
    qk = qk * lax.rsqrt(jnp.sum(qk * qk, axis=1, keepdims=True) * (1.0 / HEAD_DIM) + EPS)
    sw = jnp.concatenate([qk[:, 16:32], qk[:, 0:16], qk[:, 48:64], qk[:, 32:48]], axis=1)
    q = qk[:N_Q_HEADS] * tqa_ref[...][None] + sw[:N_Q_HEADS] * tqb_ref[...][None]
    q = q.reshape(ATTN_WIDTH, tm).astype(BF16)
    for j in range(tm // Q_TILE):
        qT_ref[j] = q[:, j * Q_TILE:(j + 1) * Q_TILE]
    k = qk[N_Q_HEADS:] * tka_ref[...][None] + sw[N_Q_HEADS:] * tkb_ref[...][None]
    kt = k.reshape(KV_WIDTH, tm).T
    k_ref[...] = kt.astype(BF16)

    vt = proj_t(_R_V, _R_MK)
    ones = jnp.ones((V_ROWS - HEAD_DIM, tm), F32)
    vtb = jnp.concatenate([vt[:HEAD_DIM], ones, vt[HEAD_DIM:], ones], axis=0).astype(BF16)
    for j in range(tm // KEY_BLOCK):
        vT_ref[j] = vtb[:, j * KEY_BLOCK:(j + 1) * KEY_BLOCK]
    if cache_refs:
        kc_ref, vc_ref = cache_refs
        kc_ref[...] = kt
        vc_ref[...] = vt.T

    mk = (proj_t(_R_MK, _R_G) * (MLSTM_DK ** -0.5)).astype(BF16)
    gates = proj_t(_R_G, _R_END) + gb_ref[...]
    for j in range(tm // CHUNK):
        mkT_ref[j] = mk[:, j * CHUNK:(j + 1) * CHUNK]
        g_ref[j] = gates[:, j * CHUNK:(j + 1) * CHUNK]

    mq_ref[...] = jnp.dot(h, wn_ref[:, 0:512], preferred_element_type=F32).astype(BF16)
    mv_ref[...] = jnp.dot(h, wn_ref[:, 512:1024], preferred_element_type=F32).astype(BF16)
    mo = jnp.dot(h, wn_ref[:, 1024:1536], preferred_element_type=F32)
    og_ref[...] = jax.nn.sigmoid(mo).astype(BF16)


def _proj_call(x2d, mods, tiles_per_mod, n1w, wt, wn, gbias, tables, tiles_per_table, emit_cache):
    T = x2d.shape[0]
    tm = TOKEN_TILE
    n_tiles = T // tm
    if tiles_per_mod is None:
        mod_map = lambda i: (0, 0, 0)
    else:
        mod_map = lambda i: (i // tiles_per_mod, 0, 0)
    if tiles_per_table is None:
        tab_map = lambda i: (0, 0)
    else:
        tab_map = lambda i: (0, i % tiles_per_table)
    tab_spec = pl.BlockSpec((HEAD_DIM, tm), tab_map)
    out_shape = [
        jax.ShapeDtypeStruct((T // Q_TILE, ATTN_WIDTH, Q_TILE), BF16),
        jax.ShapeDtypeStruct((T, KV_WIDTH), BF16),
        jax.ShapeDtypeStruct((T // KEY_BLOCK, N_KV_HEADS * V_ROWS, KEY_BLOCK), BF16),
        jax.ShapeDtypeStruct((T // CHUNK, MLSTM_WIDTH, CHUNK), BF16),
        jax.ShapeDtypeStruct((T // CHUNK, N_GATES, CHUNK), F32),
        jax.ShapeDtypeStruct((T, MLSTM_WIDTH), BF16),
        jax.ShapeDtypeStruct((T, MLSTM_WIDTH), BF16),
        jax.ShapeDtypeStruct((T, MLSTM_WIDTH), BF16),
    ]
    out_specs = [
        pl.BlockSpec((tm // Q_TILE, ATTN_WIDTH, Q_TILE), lambda i: (i, 0, 0)),
        pl.BlockSpec((tm, KV_WIDTH), lambda i: (i, 0)),
        pl.BlockSpec((tm // KEY_BLOCK, N_KV_HEADS * V_ROWS, KEY_BLOCK), lambda i: (i, 0, 0)),
        pl.BlockSpec((tm // CHUNK, MLSTM_WIDTH, CHUNK), lambda i: (i, 0, 0)),
        pl.BlockSpec((tm // CHUNK, N_GATES, CHUNK), lambda i: (i, 0, 0)),
        pl.BlockSpec((tm, MLSTM_WIDTH), lambda i: (i, 0)),
        pl.BlockSpec((tm, MLSTM_WIDTH), lambda i: (i, 0)),
        pl.BlockSpec((tm, MLSTM_WIDTH), lambda i: (i, 0)),
    ]
    if emit_cache:
        out_shape += [jax.ShapeDtypeStruct((T, KV_WIDTH), F32)] * 2
        out_specs += [pl.BlockSpec((tm, KV_WIDTH), lambda i: (i, 0))] * 2
    return pl.pallas_call(
        _proj_kernel,
        out_shape=out_shape,
        grid=(n_tiles,),
        in_specs=[pl.BlockSpec((tm, D_MODEL), lambda i: (i, 0)),
                  pl.BlockSpec((1, 6, D_MODEL), mod_map),
                  _resident((1, D_MODEL)),
                  _resident(wt.shape),
                  _resident(wn.shape),
                  _resident((N_GATES, 1)),
                  tab_spec, tab_spec, tab_spec, tab_spec],
        out_specs=out_specs,
        compiler_params=pltpu.CompilerParams(dimension_semantics=("parallel",),
                                             vmem_limit_bytes=VMEM_LIMIT),
        name="norm1_in_proj",
    )(x2d, mods, n1w, wt, wn, gbias, *tables)


def _attn_kernel(qT_ref, k_ref, vT_ref, o_ref, qp_scr, m_scr, acc_scr, s_scr, *, n_blocks):
    n_qt, _, tq = qT_ref.shape
    n_tasks = n_qt * N_Q_HEADS

    for t in range(n_tasks):
        g, hd = divmod(t, N_Q_HEADS)
        qg = qT_ref[g, hd * HEAD_DIM:(hd + 1) * HEAD_DIM, :]
        zero = jnp.zeros_like(qg)
        qp_scr[t] = jnp.concatenate([qg, zero] if hd < KV_GROUP else [zero, qg], axis=0)
    m_scr[...] = jnp.full(m_scr.shape, NEG_BIG, F32)
    acc_scr[...] = jnp.zeros(acc_scr.shape, F32)

    def scores(c, t):
        off = pl.multiple_of(c * KEY_BLOCK, KEY_BLOCK)
        return jnp.dot(k_ref[pl.ds(off, KEY_BLOCK), :], qp_scr[t], preferred_element_type=F32)

    for t in range(QK_LOOKAHEAD):
        s_scr[t] = scores(0, t)

    def body(c, _):
        c_next = jnp.minimum(c + 1, n_blocks - 1)
        s_local = {}
        for t in range(n_tasks):
            s = s_scr[t] if t < QK_LOOKAHEAD else s_local.pop(t)
            ahead = t + QK_LOOKAHEAD
            if ahead < n_tasks:
                s_local[ahead] = scores(c, ahead)
            else:
                s_scr[ahead - n_tasks] = scores(c_next, ahead - n_tasks)
            r0 = ((t % N_Q_HEADS) // KV_GROUP) * V_ROWS
            m = m_scr[t]
            m_new = jnp.maximum(m, jnp.max(s, axis=0, keepdims=True))
            p = jnp.exp2(s - m_new).astype(BF16)
            alpha = jnp.exp2(m - m_new)
            pv = jnp.dot(vT_ref[c, r0:r0 + V_ROWS, :], p, preferred_element_type=F32)
            acc_scr[t] = alpha * acc_scr[t] + pv
            m_scr[t] = m_new
        return 0

    lax.fori_loop(0, n_blocks, body, 0)
    for g in range(n_qt):
        acc = acc_scr[g * N_Q_HEADS:(g + 1) * N_Q_HEADS]
        out = acc[:, :HEAD_DIM] / acc[:, HEAD_DIM:HEAD_DIM + 1]
        o_ref[g * tq:(g + 1) * tq, :] = out.reshape(ATTN_WIDTH, tq).T.astype(BF16)


def _attn_call(qT, k, vT, n_batch, seq, n_keys, q_tiles):
    nq = seq // (Q_TILE * q_tiles)
    nkb = n_keys // KEY_BLOCK
    n_tasks = q_tiles * N_Q_HEADS
    return pl.pallas_call(
        functools.partial(_attn_kernel, n_blocks=nkb),
        out_shape=jax.ShapeDtypeStruct((n_batch * seq, ATTN_WIDTH), BF16),
        grid=(n_batch, nq),
        in_specs=[pl.BlockSpec((q_tiles, ATTN_WIDTH, Q_TILE), lambda b, i: (b * nq + i, 0, 0)),
                  pl.BlockSpec((n_keys, KV_WIDTH), lambda b, i: (b, 0)),
                  pl.BlockSpec((nkb, N_KV_HEADS * V_ROWS, KEY_BLOCK), lambda b, i: (b, 0, 0))],
        out_specs=pl.BlockSpec((q_tiles * Q_TILE, ATTN_WIDTH), lambda b, i: (b * nq + i, 0)),
        scratch_shapes=[pltpu.VMEM((n_tasks, KV_WIDTH, Q_TILE), BF16),
                        pltpu.VMEM((n_tasks, 1, Q_TILE), F32),
                        pltpu.VMEM((n_tasks, V_ROWS, Q_TILE), F32),
                        pltpu.VMEM((QK_LOOKAHEAD, KEY_BLOCK, Q_TILE), F32)],
        compiler_params=pltpu.CompilerParams(dimension_semantics=("parallel", "parallel"),
                                             vmem_limit_bytes=VMEM_LIMIT),
        name="gqa_attention",
    )(qT, k, vT)


def _log_sigmoid(x):
    return jnp.minimum(x, 0.0) - jnp.log1p(jnp.exp(-jnp.abs(x)))


def _cumsum_lanes(x, tri):
    hi = x.astype(BF16)
    r1 = x - hi.astype(F32)
    mid = r1.astype(BF16)
    lo = (r1 - mid.astype(F32)).astype(BF16)
    return (jnp.dot(hi, tri, preferred_element_type=F32) + jnp.dot(mid, tri, preferred_element_type=F32)
            + jnp.dot(lo, tri, preferred_element_type=F32))


def _tri_masks():
    ri = lax.broadcasted_iota(jnp.int32, (CHUNK, CHUNK), 0)
    ci = lax.broadcasted_iota(jnp.int32, (CHUNK, CHUNK), 1)
    return ri <= ci, ri >= ci


def _gate_rows(gs, d, le, ge):
    nh = MLSTM_HEADS
    lf = _log_sigmoid(jnp.concatenate([g[8 * d + nh:8 * d + 2 * nh] for g in gs], axis=0))
    tri = (le if d == 0 else ge).astype(BF16)
    b = _cumsum_lanes(lf, tri)
    return [(lf[nh * j:nh * (j + 1)], b[nh * j:nh * (j + 1)], g[8 * d:8 * d + nh]) for j, g in enumerate(gs)]


def _scan_kernel(kTf_ref, vf_ref, gf_ref, kTb_ref, vb_ref, gb_ref, c0_ref, m0_ref,
                 cxf_ref, cxb_ref, mf_ref, mb_ref, cfin_ref, mfin_ref):
    @pl.when(pl.program_id(1) == 0)
    def _():
        cfin_ref[...] = c0_ref[...]
        mfin_ref[...] = m0_ref[...]

    n_ch = gf_ref.shape[0]
    le, ge = _tri_masks()
    ones = jnp.ones((CHUNK, MLSTM_DK), BF16)
    dirs = ((kTf_ref, vf_ref, gf_ref, cxf_ref, mf_ref), (kTb_ref, vb_ref, gb_ref, cxb_ref, mb_ref))
    gate = []
    for d, (_, _, g_ref, _, _) in enumerate(dirs):
        per_chunk = []
        for lf, b, ig in _gate_rows([g_ref[j] for j in range(n_ch)], d, le, ge):
            b_last = jnp.sum(lf, axis=1, keepdims=True)
            gt = b_last - b + ig
            a = jnp.max(gt, axis=1, keepdims=True)
            per_chunk.append((b_last, a, jnp.exp(gt - a)))
        gate.append(per_chunk)
    prods = []
    for d, (kT_ref, v_ref, _, _, _) in enumerate(dirs):
        per_chunk = []
        for j in range(n_ch):
            wt = gate[d][j][2]
            us = []
            for h in range(MLSTM_HEADS):
                sl = slice(h * MLSTM_DK, (h + 1) * MLSTM_DK)
                ktw = (kT_ref[j, sl, :].astype(F32) * wt[h:h + 1, :]).astype(BF16)
                vx = jnp.concatenate([v_ref[j * CHUNK:(j + 1) * CHUNK, sl], ones], axis=1)
                us.append(jnp.dot(ktw, vx, preferred_element_type=F32))
            per_chunk.append(us)
        prods.append(per_chunk)
    for d, (_, _, _, cx_ref, mo_ref) in enumerate(dirs):
        local = [gate[d][j][:2] + (prods[d][j],) for j in range(n_ch)]
        m_run = mfin_ref[0, d][:, 0:1]
        state = [cfin_ref[0, d, h] for h in range(MLSTM_HEADS)]
        for j in (range(n_ch) if d == 0 else reversed(range(n_ch))):
            b_last, a, us = local[j]
            m_new = jnp.maximum(a, b_last + m_run)
            decay = jnp.exp(b_last + m_run - m_new)
            gam = jnp.exp(a - m_new)
            mo_ref[0, j] = jnp.broadcast_to(m_run, (MLSTM_HEADS, CHUNK))
            for h in range(MLSTM_HEADS):
                cx_ref[0, j, h] = state[h].astype(BF16)
                state[h] = decay[h:h + 1, :] * state[h] + gam[h:h + 1, :] * us[h]
            m_run = m_new
        mfin_ref[0, d] = jnp.broadcast_to(m_run, (MLSTM_HEADS, CHUNK))
        for h in range(MLSTM_HEADS):
            cfin_ref[0, d, h] = state[h]


def _scan_call(mkT, mv, gates, c0, m0, n_batch, seq):
    nc = seq // CHUNK
    g = next(n for n in (SCAN_CHUNKS, 2, 1) if nc % n == 0)
    ns = nc // g
    fwd3 = lambda b, c: (b * ns + c, 0, 0)
    bwd3 = lambda b, c: (b * ns + ns - 1 - c, 0, 0)
    fwd2 = lambda b, c: (b * ns + c, 0)
    bwd2 = lambda b, c: (b * ns + ns - 1 - c, 0)
    kT_blk, v_blk, g_blk = (g, MLSTM_WIDTH, CHUNK), (g * CHUNK, MLSTM_WIDTH), (g, N_GATES, CHUNK)
    cx_blk = (1, g, MLSTM_HEADS, MLSTM_DK, 2 * MLSTM_DK)
    m_blk = (1, g, MLSTM_HEADS, CHUNK)
    st_blk = (1, 2, MLSTM_HEADS, MLSTM_DK, 2 * MLSTM_DK)
    sm_blk = (1, 2, MLSTM_HEADS, CHUNK)
    return pl.pallas_call(
        _scan_kernel,
        out_shape=[jax.ShapeDtypeStruct((n_batch, nc) + cx_blk[2:], BF16),
                   jax.ShapeDtypeStruct((n_batch, nc) + cx_blk[2:], BF16),
                   jax.ShapeDtypeStruct((n_batch, nc) + m_blk[2:], F32),
                   jax.ShapeDtypeStruct((n_batch, nc) + m_blk[2:], F32),
                   jax.ShapeDtypeStruct((n_batch,) + st_blk[1:], F32),
                   jax.ShapeDtypeStruct((n_batch,) + sm_blk[1:], F32)],
        grid=(n_batch, ns),
        in_specs=[pl.BlockSpec(kT_blk, fwd3), pl.BlockSpec(v_blk, fwd2), pl.BlockSpec(g_blk, fwd3),
                  pl.BlockSpec(kT_blk, bwd3), pl.BlockSpec(v_blk, bwd2), pl.BlockSpec(g_blk, bwd3),
                  pl.BlockSpec(st_blk, lambda b, c: (b, 0, 0, 0, 0)),
                  pl.BlockSpec(sm_blk, lambda b, c: (b, 0, 0, 0))],
        out_specs=[pl.BlockSpec(cx_blk, lambda b, c: (b, c, 0, 0, 0)),
                   pl.BlockSpec(cx_blk, lambda b, c: (b, ns - 1 - c, 0, 0, 0)),
                   pl.BlockSpec(m_blk, lambda b, c: (b, c, 0, 0)),
                   pl.BlockSpec(m_blk, lambda b, c: (b, ns - 1 - c, 0, 0)),
                   pl.BlockSpec(st_blk, lambda b, c: (b, 0, 0, 0, 0)),
                   pl.BlockSpec(sm_blk, lambda b, c: (b, 0, 0, 0))],
        compiler_params=pltpu.CompilerParams(dimension_semantics=("parallel", "arbitrary"),
                                             vmem_limit_bytes=VMEM_LIMIT),
        name="mlstm_state_scan",
    )(mkT, mv, gates, mkT, mv, gates, c0, m0)


def _mlstm_out_kernel(q_ref, kT_ref, v_ref, g_ref, cxf_ref, cxb_ref, mf_ref, mb_ref, og_ref, nw_ref, o_ref):
    n_ch = g_ref.shape[0]
    le, ge = _tri_masks()
    ones = jnp.ones((CHUNK, MLSTM_DK), BF16)
    full = (CHUNK, MLSTM_DK)
    gs = [g_ref[j] for j in range(n_ch)]
    gate = [[(lf, ig - b) for lf, b, ig in _gate_rows(gs, d, le, ge)] for d in range(2)]
    qk = {}
    for j in range(n_ch):
        for h in range(MLSTM_HEADS):
            sl = slice(h * MLSTM_DK, (h + 1) * MLSTM_DK)
            qk[j, h] = jnp.dot(q_ref[j * CHUNK:(j + 1) * CHUNK, sl], kT_ref[j, sl, :],
                               preferred_element_type=F32)
    for j in range(n_ch):
        tok = slice(j * CHUNK, (j + 1) * CHUNK)
        rows = [gate[0][j], gate[1][j]]
        for h in range(MLSTM_HEADS):
            sl = slice(h * MLSTM_DK, (h + 1) * MLSTM_DK)
            qf = q_ref[tok, sl].astype(F32)
            vx = jnp.concatenate([v_ref[tok, sl], ones], axis=1)
            s = qk[j, h]
            h_sum = jnp.zeros(full, F32)
            for d, (cx_ref, m_ref) in enumerate(((cxf_ref, mf_ref), (cxb_ref, mb_ref))):
                mask = ge if d == 0 else le
                lf, r = rows[d]
                rm = jnp.where(mask, r[h:h + 1, :], NEG_BIG)
                cm = jnp.broadcast_to(jnp.max(rm, axis=1, keepdims=True), full)
                b_col = jnp.broadcast_to(
                    jnp.sum(jnp.where(mask, lf[h:h + 1, :], 0.0), axis=1, keepdims=True), full)
                m_prev = m_ref[0, j, h:h + 1, :]
                mm = jnp.maximum(cm, m_prev)
                dt = jnp.exp(rm - mm)
                inter = jnp.exp(m_prev - mm)
                lhs = jnp.concatenate([(s * dt).astype(BF16), (qf * inter).astype(BF16)], axis=1)
                rhs = jnp.concatenate([vx, cx_ref[0, j, h]], axis=0)
                mix = jnp.dot(lhs, rhs, preferred_element_type=F32)
                num = mix[:, :MLSTM_DK]
                den = mix[:, MLSTM_DK:]
                h_sum = h_sum + num / jnp.maximum(jnp.abs(den), jnp.exp(-(b_col + mm)))
            hn = _rms(h_sum) * nw_ref[:, sl]
            o_ref[tok, sl] = (hn * og_ref[tok, sl].astype(F32)).astype(BF16)


def _mlstm_out_call(mq, mkT, mv, gates, cxf, cxb, mf, mb, og, nw, n_batch, seq):
    g = OUT_CHUNKS if (seq // CHUNK) % OUT_CHUNKS == 0 else 1
    nc = seq // (CHUNK * g)
    tok = lambda b, c: (b * nc + c, 0)
    chk = lambda b, c: (b * nc + c, 0, 0)
    cx_blk = (1, g, MLSTM_HEADS, MLSTM_DK, 2 * MLSTM_DK)
    m_blk = (1, g, MLSTM_HEADS, CHUNK)
    return pl.pallas_call(
        _mlstm_out_kernel,
        out_shape=jax.ShapeDtypeStruct((n_batch * seq, MLSTM_WIDTH), BF16),
        grid=(n_batch, nc),
        in_specs=[pl.BlockSpec((g * CHUNK, MLSTM_WIDTH), tok),
                  pl.BlockSpec((g, MLSTM_WIDTH, CHUNK), chk),
                  pl.BlockSpec((g * CHUNK, MLSTM_WIDTH), tok),
                  pl.BlockSpec((g, N_GATES, CHUNK), chk),
                  pl.BlockSpec(cx_blk, lambda b, c: (b, c, 0, 0, 0)),
                  pl.BlockSpec(cx_blk, lambda b, c: (b, c, 0, 0, 0)),
                  pl.BlockSpec(m_blk, lambda b, c: (b, c, 0, 0)),
                  pl.BlockSpec(m_blk, lambda b, c: (b, c, 0, 0)),
                  pl.BlockSpec((g * CHUNK, MLSTM_WIDTH), tok),
                  pl.BlockSpec((1, MLSTM_WIDTH), lambda b, c: (0, 0))],
        out_specs=pl.BlockSpec((g * CHUNK, MLSTM_WIDTH), tok),
        compiler_params=pltpu.CompilerParams(dimension_semantics=("parallel", "parallel"),
                                             vmem_limit_bytes=VMEM_LIMIT),
        name="mlstm_chunk_out",
    )(mq, mkT, mv, gates, cxf, cxb, mf, mb, og, nw)


def _ffn_kernel(x_ref, a_ref, hm_ref, mods_ref, wo_ref, n2w_ref, wgu_ref, wd_ref, fw_ref, o_ref):
    mix = (jnp.dot(a_ref[...], wo_ref[0:ATTN_WIDTH, :], preferred_element_type=F32)
           + jnp.dot(hm_ref[...], wo_ref[ATTN_WIDTH:D_MODEL, :], preferred_element_type=F32))
    g1 = mods_ref[0, 2:3, :]
    sh2 = mods_ref[0, 3:4, :]
    sc2 = mods_ref[0, 4:5, :]
    g2 = mods_ref[0, 5:6, :]
    x1 = x_ref[...] + g1 * mix
    h2 = (_rms(x1) * n2w_ref[...] * (1.0 + sc2) + sh2).astype(BF16)
    acc = jnp.zeros(x1.shape, F32)
    for j in range(D_FF // FF_CHUNK):
        c0 = j * FF_CHUNK
        gg = jnp.dot(h2, wgu_ref[:, c0:c0 + FF_CHUNK], preferred_element_type=F32)
        uu = jnp.dot(h2, wgu_ref[:, D_FF + c0:D_FF + c0 + FF_CHUNK], preferred_element_type=F32)
        act = (gg * jax.nn.sigmoid(gg) * uu).astype(BF16)
        acc = acc + jnp.dot(act, wd_ref[c0:c0 + FF_CHUNK, :], preferred_element_type=F32)
    x2 = x1 + g2 * acc
    o_ref[...] = _rms(x2) * fw_ref[...]


def _ffn_call(x2d, attn_o, hm, mods, tiles_per_mod, wo, n2w, wgu, wd, fw):
    T = x2d.shape[0]
    tm = TOKEN_TILE
    if tiles_per_mod is None:
        mod_map = lambda i: (0, 0, 0)
    else:
        mod_map = lambda i: (i // tiles_per_mod, 0, 0)
    return pl.pallas_call(
        _ffn_kernel,
        out_shape=jax.ShapeDtypeStruct((T, D_MODEL), F32),
        grid=(T // tm,),
        in_specs=[pl.BlockSpec((tm, D_MODEL), lambda i: (i, 0)),
                  pl.BlockSpec((tm, ATTN_WIDTH), lambda i: (i, 0)),
                  pl.BlockSpec((tm, MLSTM_WIDTH), lambda i: (i, 0)),
                  pl.BlockSpec((1, 6, D_MODEL), mod_map),
                  _resident(wo.shape), _resident((1, D_MODEL)), _resident(wgu.shape),
                  _resident(wd.shape), _resident((1, D_MODEL))],
        out_specs=pl.BlockSpec((tm, D_MODEL), lambda i: (i, 0)),
        compiler_params=pltpu.CompilerParams(dimension_semantics=("parallel",),
                                             vmem_limit_bytes=VMEM_LIMIT),
        name="out_proj_ffn",
    )(x2d, attn_o, hm, mods, wo, n2w, wgu, wd, fw)


def _rope_tables(seq, gain, scale):
    pos = jnp.arange(seq, dtype=jnp.int32)
    row_ids = (pos // GRID_W).astype(F32)
    col_ids = (pos % GRID_W).astype(F32)
    inv = ROPE_THETA ** (-jnp.arange(0, AXIS_DIM, 2, dtype=F32) / AXIS_DIM)
    ang_r = inv[:, None] * row_ids[None, :]
    ang_c = inv[:, None] * col_ids[None, :]
    cos = jnp.concatenate([jnp.cos(ang_r)] * 2 + [jnp.cos(ang_c)] * 2, axis=0)
    sin = jnp.concatenate([-jnp.sin(ang_r), jnp.sin(ang_r), -jnp.sin(ang_c), jnp.sin(ang_c)], axis=0)
    g = gain.astype(F32) * scale
    g_sw = jnp.concatenate([g[16:32], g[0:16], g[48:64], g[32:48]])
    return g[:, None] * cos, g_sw[:, None] * sin


def _flat_tables(gain, scale, width):
    g = gain.astype(F32) * scale
    return jnp.broadcast_to(g[:, None], (HEAD_DIM, width)), jnp.zeros((HEAD_DIM, width), F32)


def _group(x, mods, tiles_per_mod, seq, weights, tables, tiles_per_table, cache, c0, m0, emit_cache):
    n_batch = x.shape[0]
    x2d = x.reshape(n_batch * seq, D_MODEL)
    outs = _proj_call(x2d, mods, tiles_per_mod, weights["n1w"], weights["wt"], weights["wn"],
                      weights["gbias"], tables, tiles_per_table, emit_cache)
    qT, k, vT, mkT, gates, mq, mv, og = outs[:8]
    n_keys = seq
    if cache is not None:
        ck, cvT = cache
        past = ck.shape[1]
        n_keys = seq + past
        k = jnp.concatenate([k.reshape(n_batch, seq, KV_WIDTH), ck], axis=1).reshape(n_batch * n_keys, KV_WIDTH)
        v_rows = N_KV_HEADS * V_ROWS
        vT = jnp.concatenate([vT.reshape(n_batch, seq // KEY_BLOCK, v_rows, KEY_BLOCK), cvT], axis=1)
        vT = vT.reshape(n_batch * n_keys // KEY_BLOCK, v_rows, KEY_BLOCK)
    q_tiles = Q_TILES_PER_STEP if seq % (Q_TILE * Q_TILES_PER_STEP) == 0 else 1
    attn_o = _attn_call(qT, k, vT, n_batch, seq, n_keys, q_tiles)
    cxf, cxb, mf, mb, cfin, mfin = _scan_call(mkT, mv, gates, c0, m0, n_batch, seq)
    hm = _mlstm_out_call(mq, mkT, mv, gates, cxf, cxb, mf, mb, og, weights["mnw"], n_batch, seq)
    y = _ffn_call(x2d, attn_o, hm, mods, tiles_per_mod, weights["wo"], weights["n2w"], weights["wgu"],
                  weights["wd"], weights["fw"])
    return y.reshape(n_batch, seq, D_MODEL), outs[8:], cfin, mfin


def kernel(x_prompt, x_sample, cache_k, cache_v, state_C, state_n, state_m, c, c_ctx, w_ada, b_ada,
           norm1_w, w_in, gate_bias, q_norm_w, k_norm_w, mlstm_norm_w, w_out, norm2_w, w_gu, w_down,
           final_norm_w):
    depth = w_ada.shape[0]
    assert depth == 1, "single-layer trunk"
    n_ctx, ctx_len, _ = x_prompt.shape
    n_dec, dec_len, _ = x_sample.shape
    l = 0

    cond = jnp.zeros((16, D_MODEL), F32).at[:n_dec].set(c).at[n_dec].set(c_ctx)
    mods = _mods_call(cond, w_ada[l], b_ada[l]).reshape(16, 6, D_MODEL)

    wi = w_in[l]
    aq, ak, av, mq_w, mk_w, mv_w, mo_w, mg_w = jnp.split(
        wi, [512, 640, 768, 1280, 1792, 2304, 2816], axis=1)
    weights = {
        "wt": jnp.concatenate([aq, ak, av, mk_w, mg_w], axis=1).T.astype(BF16),
        "wn": jnp.concatenate([mq_w, mv_w, mo_w], axis=1).astype(BF16),
        "gbias": gate_bias[l].reshape(N_GATES, 1).astype(F32),
        "n1w": norm1_w[l].reshape(1, D_MODEL),
        "n2w": norm2_w[l].reshape(1, D_MODEL),
        "mnw": mlstm_norm_w[l].reshape(1, MLSTM_WIDTH),
        "wo": w_out[l].astype(BF16),
        "wgu": w_gu[l].astype(BF16),
        "wd": w_down[l].astype(BF16),
        "fw": final_norm_w.reshape(1, D_MODEL),
    }
    q_scale = (HEAD_DIM ** -0.5) * LOG2E

    tabs_ctx = _flat_tables(q_norm_w[l], q_scale, TOKEN_TILE) + _flat_tables(k_norm_w[l], 1.0, TOKEN_TILE)
    c0 = jnp.zeros((n_ctx, 2, MLSTM_HEADS, MLSTM_DK, 2 * MLSTM_DK), F32)
    m0 = jnp.zeros((n_ctx, 2, MLSTM_HEADS, CHUNK), F32)
    y_prompt, (kc, vc), cfin, mfin = _group(x_prompt, mods[n_dec:n_dec + 1], None, ctx_len, weights,
                                            tabs_ctx, None, None, c0, m0, True)
    new_cache_k = kc.reshape(n_ctx, 1, ctx_len, N_KV_HEADS, HEAD_DIM)
    new_cache_v = vc.reshape(n_ctx, 1, ctx_len, N_KV_HEADS, HEAD_DIM)
    new_state_C = cfin[..., :MLSTM_DK][:, None]
    new_state_n = cfin[..., MLSTM_DK][:, None]
    new_state_m = mfin[..., 0][:, None]

    tabs_dec = _rope_tables(dec_len, q_norm_w[l], q_scale) + _rope_tables(dec_len, k_norm_w[l], 1.0)
    past = cache_k.shape[2]
    ck = cache_k[:, l].reshape(n_dec, past, KV_WIDTH).astype(BF16)
    cvT = cache_v[:, l].reshape(n_dec, past // KEY_BLOCK, KEY_BLOCK, KV_WIDTH).transpose(0, 1, 3, 2)
    pad_ones = jnp.ones(cvT.shape[:2] + (V_ROWS - HEAD_DIM, KEY_BLOCK), cvT.dtype)
    cvT = jnp.concatenate([cvT[:, :, :HEAD_DIM], pad_ones, cvT[:, :, HEAD_DIM:], pad_ones], axis=2).astype(BF16)
    sC = state_C[:, l].astype(F32)
    sn = state_n[:, l].astype(F32)
    c0 = jnp.concatenate([sC, jnp.broadcast_to(sn[..., None], sC.shape)], axis=-1)
    m0 = jnp.broadcast_to(state_m[:, l].astype(F32)[..., None], (n_dec, 2, MLSTM_HEADS, CHUNK))
    y_sample, _, _, _ = _group(x_sample, mods[:n_dec], dec_len // TOKEN_TILE, dec_len, weights,
                               tabs_dec, dec_len // TOKEN_TILE, (ck, cvT), c0, m0, False)

    return (y_prompt, y_sample, new_cache_k, new_cache_v, new_state_C, new_state_n, new_state_m)
```

```python
import functools

import jax
import jax.numpy as jnp
from jax import lax
from jax.experimental import pallas as pl
from jax.experimental.pallas import tpu as pltpu

F32 = jnp.float32
BF16 = jnp.bfloat16

D_MODEL = 1024
HEAD_DIM = 64
N_Q_HEADS = 8
N_KV_HEADS = 2
KV_GROUP = N_Q_HEADS // N_KV_HEADS
ATTN_WIDTH = N_Q_HEADS * HEAD_DIM
KV_WIDTH = N_KV_HEADS * HEAD_DIM
AXIS_DIM = HEAD_DIM // 2
GRID_W = 64
ROPE_THETA = 10000.0
MLSTM_HEADS = 4
MLSTM_DK = 128
MLSTM_WIDTH = MLSTM_HEADS * MLSTM_DK
CHUNK = 128
D_FF = 2816
N_GATES = 4 * MLSTM_HEADS
EPS = 1e-6
LOG2E = 1.4426950408889634
NEG_BIG = -1e30

TOKEN_TILE = 512
Q_TILE = 256
KEY_BLOCK = 256
QK_LOOKAHEAD = 6
S_SLOTS = 8
Q_TILES_PER_STEP = 8
V_ROWS = 80
SCAN_CHUNKS = 4
OUT_CHUNKS = 2
FF_CHUNK = 1024
ADA_TILE = 1536
VMEM_LIMIT = 56 * 1024 * 1024

_R_Q, _R_K, _R_V, _R_MK, _R_G, _R_END = 0, 512, 640, 768, 1280, 1296


def _resident(shape):
    nd = len(shape)
    return pl.BlockSpec(shape, lambda *_: (0,) * nd, pipeline_mode=pl.Buffered(1))


def _mods_kernel(cond_ref, w_ref, b_ref, o_ref):
    c = cond_ref[...]
    s = c * jax.nn.sigmoid(c)
    o_ref[...] = jnp.dot(s, w_ref[...], preferred_element_type=F32) + b_ref[...]


def _mods_call(cond, w_ada, b_ada):
    rows, n_out = cond.shape[0], w_ada.shape[1]
    return pl.pallas_call(
        _mods_kernel,
        out_shape=jax.ShapeDtypeStruct((rows, n_out), F32),
        grid=(n_out // ADA_TILE,),
        in_specs=[pl.BlockSpec((rows, D_MODEL), lambda j: (0, 0)),
                  pl.BlockSpec((D_MODEL, ADA_TILE), lambda j: (0, j)),
                  pl.BlockSpec((1, ADA_TILE), lambda j: (0, j))],
        out_specs=pl.BlockSpec((rows, ADA_TILE), lambda j: (0, j)),
        compiler_params=pltpu.CompilerParams(dimension_semantics=("parallel",),
                                             vmem_limit_bytes=VMEM_LIMIT),
        name="adaln_mods",
    )(cond, w_ada, b_ada.reshape(1, n_out))


def _rms(x):
    return x * lax.rsqrt(jnp.mean(x * x, axis=-1, keepdims=True) + EPS)


def _proj_kernel(x_ref, mods_ref, n1w_ref, wt_ref, wn_ref, gb_ref, tqa_ref, tqb_ref, tka_ref, tkb_ref,
                 qT_ref, k_ref, vT_ref, mkT_ref, g_ref, mq_ref, mv_ref, og_ref, *cache_refs):
    tm = x_ref.shape[0]
    x = x_ref[...]
    sh = mods_ref[0, 0:1, :]
    sc = mods_ref[0, 1:2, :]
    h = (_rms(x) * n1w_ref[...] * (1.0 + sc) + sh).astype(BF16)

    yt = lax.dot_general(wt_ref[...], h, (((1,), (1,)), ((), ())), preferred_element_type=F32)

    def proj_t(r0, r1):
        return yt[r0:r1]

    n_qk = N_Q_HEADS + N_KV_HEADS
    qk = proj_t(_R_Q, _R_V).reshape(n_qk, HEAD_DIM, tm)
    qk = qk * lax.rsqrt(jnp.sum(qk * qk, axis=1, keepdims=True) * (1.0 / HEAD_DIM) + EPS)
    sw = jnp.concatenate([qk[:, 16:32], qk[:, 0:16], qk[:, 48:64], qk[:, 32:48]], axis=1)
    q = qk[:N_Q_HEADS] * tqa_ref[...][None] + sw[:N_Q_HEADS] * tqb_ref[...][None]
    q = q.reshape(ATTN_WIDTH, tm).astype(BF16)
    for j in range(tm // Q_TILE):
        qT_ref[j] = q[:, j * Q_TILE:(j + 1) * Q_TILE]
    k = qk[N_Q_HEADS:] * tka_ref[...][None] + sw[N_Q_HEADS:] * tkb_ref[...][None]
    kt = k.reshape(KV_WIDTH, tm).T
    k_ref[...] = kt.astype(BF16)

    vt = proj_t(_R_V, _R_MK)
    ones = jnp.ones((V_ROWS - HEAD_DIM, tm), F32)
    vtb = jnp.concatenate([vt[:HEAD_DIM], ones, vt[HEAD_DIM:], ones], axis=0).astype(BF16)
    for j in range(tm // KEY_BLOCK):
        vT_ref[j] = vtb[:, j * KEY_BLOCK:(j + 1) * KEY_BLOCK]
    if cache_refs:
        kc_ref, vc_ref = cache_refs
        kc_ref[...] = kt
        vc_ref[...] = vt.T

    mk = (proj_t(_R_MK, _R_G) * (MLSTM_DK ** -0.5)).astype(BF16)
    gates = proj_t(_R_G, _R_END) + gb_ref[...]
    for j in range(tm // CHUNK):
        mkT_ref[j] = mk[:, j * CHUNK:(j + 1) * CHUNK]
        g_ref[j] = gates[:, j * CHUNK:(j + 1) * CHUNK]

    mq_ref[...] = jnp.dot(h, wn_ref[:, 0:512], preferred_element_type=F32).astype(BF16)
    mv_ref[...] = jnp.dot(h, wn_ref[:, 512:1024], preferred_element_type=F32).astype(BF16)
    mo = jnp.dot(h, wn_ref[:, 1024:1536], preferred_element_type=F32)
    og_ref[...] = jax.nn.sigmoid(mo).astype(BF16)


def _proj_call(x2d, mods, tiles_per_mod, n1w, wt, wn, gbias, tables, tiles_per_table, emit_cache):
    T = x2d.shape[0]
    tm = TOKEN_TILE
    n_tiles = T // tm
    if tiles_per_mod is None:
        mod_map = lambda i: (0, 0, 0)
    else:
        mod_map = lambda i: (i // tiles_per_mod, 0, 0)
    if tiles_per_table is None:
        tab_map = lambda i: (0, 0)
    else:
        tab_map = lambda i: (0, i % tiles_per_table)
    tab_spec = pl.BlockSpec((HEAD_DIM, tm), tab_map)
    out_shape = [
        jax.ShapeDtypeStruct((T // Q_TILE, ATTN_WIDTH, Q_TILE), BF16),
        jax.ShapeDtypeStruct((T, KV_WIDTH), BF16),
        jax.ShapeDtypeStruct((T // KEY_BLOCK, N_KV_HEADS * V_ROWS, KEY_BLOCK), BF16),
        jax.ShapeDtypeStruct((T // CHUNK, MLSTM_WIDTH, CHUNK), BF16),
        jax.ShapeDtypeStruct((T // CHUNK, N_GATES, CHUNK), F32),
        jax.ShapeDtypeStruct((T, MLSTM_WIDTH), BF16),
        jax.ShapeDtypeStruct((T, MLSTM_WIDTH), BF16),
        jax.ShapeDtypeStruct((T, MLSTM_WIDTH), BF16),
    ]
    out_specs = [
        pl.BlockSpec((tm // Q_TILE, ATTN_WIDTH, Q_TILE), lambda i: (i, 0, 0)),
        pl.BlockSpec((tm, KV_WIDTH), lambda i: (i, 0)),
        pl.BlockSpec((tm // KEY_BLOCK, N_KV_HEADS * V_ROWS, KEY_BLOCK), lambda i: (i, 0, 0)),
        pl.BlockSpec((tm // CHUNK, MLSTM_WIDTH, CHUNK), lambda i: (i, 0, 0)),
        pl.BlockSpec((tm // CHUNK, N_GATES, CHUNK), lambda i: (i, 0, 0)),
        pl.BlockSpec((tm, MLSTM_WIDTH), lambda i: (i, 0)),
        pl.BlockSpec((tm, MLSTM_WIDTH), lambda i: (i, 0)),
        pl.BlockSpec((tm, MLSTM_WIDTH), lambda i: (i, 0)),
    ]
    if emit_cache:
        out_shape += [jax.ShapeDtypeStruct((T, KV_WIDTH), F32)] * 2
        out_specs += [pl.BlockSpec((tm, KV_WIDTH), lambda i: (i, 0))] * 2
    return pl.pallas_call(
        _proj_kernel,
        out_shape=out_shape,
        grid=(n_tiles,),
        in_specs=[pl.BlockSpec((tm, D_MODEL), lambda i: (i, 0)),
                  pl.BlockSpec((1, 6, D_MODEL), mod_map),
                  _resident((1, D_MODEL)),
                  _resident(wt.shape),
                  _resident(wn.shape),
                  _resident((N_GATES, 1)),
                  tab_spec, tab_spec, tab_spec, tab_spec],
        out_specs=out_specs,
        compiler_params=pltpu.CompilerParams(dimension_semantics=("parallel",),
                                             vmem_limit_bytes=VMEM_LIMIT),
        name="norm1_in_proj",
    )(x2d, mods, n1w, wt, wn, gbias, *tables)


def _attn_kernel(qT_ref, k_ref, vT_ref, o_ref, qp_scr, m_scr, acc_scr, s_scr, *, n_blocks):
    n_qt, _, tq = qT_ref.shape
    n_tasks = n_qt * N_Q_HEADS

    for t in range(n_tasks):
        g, hd = divmod(t, N_Q_HEADS)
        qg = qT_ref[g, hd * HEAD_DIM:(hd + 1) * HEAD_DIM, :]
        zero = jnp.zeros_like(qg)
        qp_scr[t] = jnp.concatenate([qg, zero] if hd < KV_GROUP else [zero, qg], axis=0)
    m_scr[...] = jnp.full(m_scr.shape, NEG_BIG, F32)
    acc_scr[...] = jnp.zeros(acc_scr.shape, F32)

    def scores(c, t):
        off = pl.multiple_of(c * KEY_BLOCK, KEY_BLOCK)
        return jnp.dot(k_ref[pl.ds(off, KEY_BLOCK), :], qp_scr[t], preferred_element_type=F32)

    assert n_tasks % S_SLOTS == 0 and QK_LOOKAHEAD < S_SLOTS
    for t in range(QK_LOOKAHEAD):
        s_scr[t] = scores(0, t)

    def body(c, _):
        c_next = jnp.minimum(c + 1, n_blocks - 1)
        for t in range(n_tasks):
            s = s_scr[t % S_SLOTS]
            ahead = t + QK_LOOKAHEAD
            if ahead < n_tasks:
                s_scr[ahead % S_SLOTS] = scores(c, ahead)
            else:
                s_scr[ahead % S_SLOTS] = scores(c_next, ahead - n_tasks)
            r0 = ((t % N_Q_HEADS) // KV_GROUP) * V_ROWS
            m = m_scr[t]
            m_new = jnp.maximum(m, jnp.max(s, axis=0, keepdims=True))
            p = jnp.exp2(s - m_new).astype(BF16)
            alpha = jnp.exp2(m - m_new)
            pv = jnp.dot(vT_ref[c, r0:r0 + V_ROWS, :], p, preferred_element_type=F32)
            acc_scr[t] = alpha * acc_scr[t] + pv
            m_scr[t] = m_new
        return 0

    lax.fori_loop(0, n_blocks, body, 0)
    for g in range(n_qt):
        acc = acc_scr[g * N_Q_HEADS:(g + 1) * N_Q_HEADS]
        out = acc[:, :HEAD_DIM] / acc[:, HEAD_DIM:HEAD_DIM + 1]
        o_ref[g * tq:(g + 1) * tq, :] = out.reshape(ATTN_WIDTH, tq).T.astype(BF16)


def _attn_call(qT, k, vT, n_batch, seq, n_keys, q_tiles):
    nq = seq // (Q_TILE * q_tiles)
    nkb = n_keys // KEY_BLOCK
    n_tasks = q_tiles * N_Q_HEADS
    return pl.pallas_call(
        functools.partial(_attn_kernel, n_blocks=nkb),
        out_shape=jax.ShapeDtypeStruct((n_batch * seq, ATTN_WIDTH), BF16),
        grid=(n_batch, nq),
        in_specs=[pl.BlockSpec((q_tiles, ATTN_WIDTH, Q_TILE), lambda b, i: (b * nq + i, 0, 0)),
                  pl.BlockSpec((n_keys, KV_WIDTH), lambda b, i: (b, 0)),
                  pl.BlockSpec((nkb, N_KV_HEADS * V_ROWS, KEY_BLOCK), lambda b, i: (b, 0, 0))],
        out_specs=pl.BlockSpec((q_tiles * Q_TILE, ATTN_WIDTH), lambda b, i: (b * nq + i, 0)),
        scratch_shapes=[pltpu.VMEM((n_tasks, KV_WIDTH, Q_TILE), BF16),
                        pltpu.VMEM((n_tasks, 1, Q_TILE), F32),
                        pltpu.VMEM((n_tasks, V_ROWS, Q_TILE), F32),
                        pltpu.VMEM((S_SLOTS, KEY_BLOCK, Q_TILE), F32)],
        compiler_params=pltpu.CompilerParams(dimension_semantics=("parallel", "parallel"),
                                             vmem_limit_bytes=VMEM_LIMIT),
        name="gqa_attention",
    )(qT, k, vT)


def _log_sigmoid(x):
    return jnp.minimum(x, 0.0) - jnp.log1p(jnp.exp(-jnp.abs(x)))


def _cumsum_lanes(x, tri):
    hi = x.astype(BF16)
    r1 = x - hi.astype(F32)
    mid = r1.astype(BF16)
    lo = (r1 - mid.astype(F32)).astype(BF16)
    return (jnp.dot(hi, tri, preferred_element_type=F32) + jnp.dot(mid, tri, preferred_element_type=F32)
            + jnp.dot(lo, tri, preferred_element_type=F32))


def _tri_masks():
    ri = lax.broadcasted_iota(jnp.int32, (CHUNK, CHUNK), 0)
    ci = lax.broadcasted_iota(jnp.int32, (CHUNK, CHUNK), 1)
    return ri <= ci, ri >= ci


def _gate_rows(gs, d, le, ge):
    nh = MLSTM_HEADS
    lf = _log_sigmoid(jnp.concatenate([g[8 * d + nh:8 * d + 2 * nh] for g in gs], axis=0))
    tri = (le if d == 0 else ge).astype(BF16)
    b = _cumsum_lanes(lf, tri)
    return [(lf[nh * j:nh * (j + 1)], b[nh * j:nh * (j + 1)], g[8 * d:8 * d + nh]) for j, g in enumerate(gs)]


def _scan_kernel(*refs, has_init):
    kTf_ref, vf_ref, gf_ref, kTb_ref, vb_ref, gb_ref = refs[:6]
    cxf_ref, cxb_ref, mf_ref, mb_ref, cfin_ref, mfin_ref = refs[-6:]

    @pl.when(pl.program_id(1) == 0)
    def _():
        if has_init:
            c0_ref, m0_ref = refs[6:8]
            cfin_ref[...] = c0_ref[...]
            mfin_ref[...] = m0_ref[...]
        else:
            cfin_ref[...] = jnp.zeros(cfin_ref.shape, F32)
            mfin_ref[...] = jnp.zeros(mfin_ref.shape, F32)

    n_ch = gf_ref.shape[0]
    le, ge = _tri_masks()
    ones = jnp.ones((CHUNK, MLSTM_DK), BF16)
    dirs = ((kTf_ref, vf_ref, gf_ref, cxf_ref, mf_ref), (kTb_ref, vb_ref, gb_ref, cxb_ref, mb_ref))
    gate = []
    for d, (_, _, g_ref, _, _) in enumerate(dirs):
        per_chunk = []
        for lf, b, ig in _gate_rows([g_ref[j] for j in range(n_ch)], d, le, ge):
            b_last = jnp.sum(lf, axis=1, keepdims=True)
            gt = b_last - b + ig
            a = jnp.max(gt, axis=1, keepdims=True)
            per_chunk.append((b_last, a, jnp.exp(gt - a)))
        gate.append(per_chunk)
    prods = []
    for d, (kT_ref, v_ref, _, _, _) in enumerate(dirs):
        per_chunk = []
        for j in range(n_ch):
            wt = gate[d][j][2]
            us = []
            for h in range(MLSTM_HEADS):
                sl = slice(h * MLSTM_DK, (h + 1) * MLSTM_DK)
                ktw = (kT_ref[j, sl, :].astype(F32) * wt[h:h + 1, :]).astype(BF16)
                vx = jnp.concatenate([v_ref[j * CHUNK:(j + 1) * CHUNK, sl], ones], axis=1)
                us.append(jnp.dot(ktw, vx, preferred_element_type=F32))
            per_chunk.append(us)
        prods.append(per_chunk)
    for d, (_, _, _, cx_ref, mo_ref) in enumerate(dirs):
        local = [gate[d][j][:2] + (prods[d][j],) for j in range(n_ch)]
        m_run = mfin_ref[0, d][:, 0:1]
        state = [cfin_ref[0, d, h] for h in range(MLSTM_HEADS)]
        for j in (range(n_ch) if d == 0 else reversed(range(n_ch))):
            b_last, a, us = local[j]
            m_new = jnp.maximum(a, b_last + m_run)
            decay = jnp.exp(b_last + m_run - m_new)
            gam = jnp.exp(a - m_new)
            mo_ref[0, j] = jnp.broadcast_to(m_run, (MLSTM_HEADS, CHUNK))
            for h in range(MLSTM_HEADS):
                cx_ref[0, j, h] = state[h].astype(BF16)
                state[h] = decay[h:h + 1, :] * state[h] + gam[h:h + 1, :] * us[h]
            m_run = m_new
        mfin_ref[0, d] = jnp.broadcast_to(m_run, (MLSTM_HEADS, CHUNK))
        for h in range(MLSTM_HEADS):
            cfin_ref[0, d, h] = state[h]


def _scan_call(mkT, mv, gates, c0, m0, n_batch, seq):
    nc = seq // CHUNK
    g = next(n for n in (SCAN_CHUNKS, 2, 1) if nc % n == 0)
    ns = nc // g
    fwd3 = lambda b, c: (b * ns + c, 0, 0)
    bwd3 = lambda b, c: (b * ns + ns - 1 - c, 0, 0)
    fwd2 = lambda b, c: (b * ns + c, 0)
    bwd2 = lambda b, c: (b * ns + ns - 1 - c, 0)
    kT_blk, v_blk, g_blk = (g, MLSTM_WIDTH, CHUNK), (g * CHUNK, MLSTM_WIDTH), (g, N_GATES, CHUNK)
    cx_blk = (1, g, MLSTM_HEADS, MLSTM_DK, 2 * MLSTM_DK)
    m_blk = (1, g, MLSTM_HEADS, CHUNK)
    st_blk = (1, 2, MLSTM_HEADS, MLSTM_DK, 2 * MLSTM_DK)
    sm_blk = (1, 2, MLSTM_HEADS, CHUNK)
    in_specs = [pl.BlockSpec(kT_blk, fwd3), pl.BlockSpec(v_blk, fwd2), pl.BlockSpec(g_blk, fwd3),
                pl.BlockSpec(kT_blk, bwd3), pl.BlockSpec(v_blk, bwd2), pl.BlockSpec(g_blk, bwd3)]
    args = [mkT, mv, gates, mkT, mv, gates]
    if c0 is not None:
        in_specs += [pl.BlockSpec(st_blk, lambda b, c: (b, 0, 0, 0, 0)),
                     pl.BlockSpec(sm_blk, lambda b, c: (b, 0, 0, 0))]
        args += [c0, m0]
    return pl.pallas_call(
        functools.partial(_scan_kernel, has_init=c0 is not None),
        out_shape=[jax.ShapeDtypeStruct((n_batch, nc) + cx_blk[2:], BF16),
                   jax.ShapeDtypeStruct((n_batch, nc) + cx_blk[2:], BF16),
                   jax.ShapeDtypeStruct((n_batch, nc) + m_blk[2:], F32),
                   jax.ShapeDtypeStruct((n_batch, nc) + m_blk[2:], F32),
                   jax.ShapeDtypeStruct((n_batch,) + st_blk[1:], F32),
                   jax.ShapeDtypeStruct((n_batch,) + sm_blk[1:], F32)],
        grid=(n_batch, ns),
        in_specs=in_specs,
        out_specs=[pl.BlockSpec(cx_blk, lambda b, c: (b, c, 0, 0, 0)),
                   pl.BlockSpec(cx_blk, lambda b, c: (b, ns - 1 - c, 0, 0, 0)),
                   pl.BlockSpec(m_blk, lambda b, c: (b, c, 0, 0)),
                   pl.BlockSpec(m_blk, lambda b, c: (b, ns - 1 - c, 0, 0)),
                   pl.BlockSpec(st_blk, lambda b, c: (b, 0, 0, 0, 0)),
                   pl.BlockSpec(sm_blk, lambda b, c: (b, 0, 0, 0))],
        compiler_params=pltpu.CompilerParams(dimension_semantics=("parallel", "arbitrary"),
                                             vmem_limit_bytes=VMEM_LIMIT),
        name="mlstm_state_scan",
    )(*args)


def _mlstm_out_kernel(q_ref, kT_ref, v_ref, g_ref, cxf_ref, cxb_ref, mf_ref, mb_ref, og_ref, nw_ref, o_ref):
    n_ch = g_ref.shape[0]
    le, ge = _tri_masks()
    ones = jnp.ones((CHUNK, MLSTM_DK), BF16)
    full = (CHUNK, MLSTM_DK)
    gs = [g_ref[j] for j in range(n_ch)]
    gate = [[(lf, ig - b) for lf, b, ig in _gate_rows(gs, d, le, ge)] for d in range(2)]
    qk = {}
    for j in range(n_ch):
        for h in range(MLSTM_HEADS):
            sl = slice(h * MLSTM_DK, (h + 1) * MLSTM_DK)
            qk[j, h] = jnp.dot(q_ref[j * CHUNK:(j + 1) * CHUNK, sl], kT_ref[j, sl, :],
                               preferred_element_type=F32)
    for j in range(n_ch):
        tok = slice(j * CHUNK, (j + 1) * CHUNK)
        rows = [gate[0][j], gate[1][j]]
        for h in range(MLSTM_HEADS):
            sl = slice(h * MLSTM_DK, (h + 1) * MLSTM_DK)
            qf = q_ref[tok, sl].astype(F32)
            vx = jnp.concatenate([v_ref[tok, sl], ones], axis=1)
            s = qk[j, h]
            h_sum = jnp.zeros(full, F32)
            for d, (cx_ref, m_ref) in enumerate(((cxf_ref, mf_ref), (cxb_ref, mb_ref))):
                mask = ge if d == 0 else le
                lf, r = rows[d]
                rm = jnp.where(mask, r[h:h + 1, :], NEG_BIG)
                cm = jnp.broadcast_to(jnp.max(rm, axis=1, keepdims=True), full)
                b_col = jnp.broadcast_to(
                    jnp.sum(jnp.where(mask, lf[h:h + 1, :], 0.0), axis=1, keepdims=True), full)
                m_prev = m_ref[0, j, h:h + 1, :]
                mm = jnp.maximum(cm, m_prev)
                dt = jnp.exp(rm - mm)
                inter = jnp.exp(m_prev - mm)
                lhs = jnp.concatenate([(s * dt).astype(BF16), (qf * inter).astype(BF16)], axis=1)
                rhs = jnp.concatenate([vx, cx_ref[0, j, h]], axis=0)
                mix = jnp.dot(lhs, rhs, preferred_element_type=F32)
                num = mix[:, :MLSTM_DK]
                den = mix[:, MLSTM_DK:]
                h_sum = h_sum + num / jnp.maximum(jnp.abs(den), jnp.exp(-(b_col + mm)))
            hn = _rms(h_sum) * nw_ref[:, sl]
            o_ref[tok, sl] = (hn * og_ref[tok, sl].astype(F32)).astype(BF16)


def _mlstm_out_call(mq, mkT, mv, gates, cxf, cxb, mf, mb, og, nw, n_batch, seq):
    g = OUT_CHUNKS if (seq // CHUNK) % OUT_CHUNKS == 0 else 1
    nc = seq // (CHUNK * g)
    tok = lambda b, c: (b * nc + c, 0)
    chk = lambda b, c: (b * nc + c, 0, 0)
    cx_blk = (1, g, MLSTM_HEADS, MLSTM_DK, 2 * MLSTM_DK)
    m_blk = (1, g, MLSTM_HEADS, CHUNK)
    return pl.pallas_call(
        _mlstm_out_kernel,
        out_shape=jax.ShapeDtypeStruct((n_batch * seq, MLSTM_WIDTH), BF16),
        grid=(n_batch, nc),
        in_specs=[pl.BlockSpec((g * CHUNK, MLSTM_WIDTH), tok),
                  pl.BlockSpec((g, MLSTM_WIDTH, CHUNK), chk),
                  pl.BlockSpec((g * CHUNK, MLSTM_WIDTH), tok),
                  pl.BlockSpec((g, N_GATES, CHUNK), chk),
                  pl.BlockSpec(cx_blk, lambda b, c: (b, c, 0, 0, 0)),
                  pl.BlockSpec(cx_blk, lambda b, c: (b, c, 0, 0, 0)),
                  pl.BlockSpec(m_blk, lambda b, c: (b, c, 0, 0)),
                  pl.BlockSpec(m_blk, lambda b, c: (b, c, 0, 0)),
                  pl.BlockSpec((g * CHUNK, MLSTM_WIDTH), tok),
                  pl.BlockSpec((1, MLSTM_WIDTH), lambda b, c: (0, 0))],
        out_specs=pl.BlockSpec((g * CHUNK, MLSTM_WIDTH), tok),
        compiler_params=pltpu.CompilerParams(dimension_semantics=("parallel", "parallel"),
                                             vmem_limit_bytes=VMEM_LIMIT),
        name="mlstm_chunk_out",
    )(mq, mkT, mv, gates, cxf, cxb, mf, mb, og, nw)


def _ffn_kernel(x_ref, a_ref, hm_ref, mods_ref, wo_ref, n2w_ref, wgu_ref, wd_ref, fw_ref, o_ref):
    mix = (jnp.dot(a_ref[...], wo_ref[0:ATTN_WIDTH, :], preferred_element_type=F32)
           + jnp.dot(hm_ref[...], wo_ref[ATTN_WIDTH:D_MODEL, :], preferred_element_type=F32))
    g1 = mods_ref[0, 2:3, :]
    sh2 = mods_ref[0, 3:4, :]
    sc2 = mods_ref[0, 4:5, :]
    g2 = mods_ref[0, 5:6, :]
    x1 = x_ref[...] + g1 * mix
    h2 = (_rms(x1) * n2w_ref[...] * (1.0 + sc2) + sh2).astype(BF16)
    acc = jnp.zeros(x1.shape, F32)
    for c0 in range(0, D_FF, FF_CHUNK):
        c1 = min(c0 + FF_CHUNK, D_FF)
        gg = jnp.dot(h2, wgu_ref[:, c0:c1], preferred_element_type=F32)
        uu = jnp.dot(h2, wgu_ref[:, D_FF + c0:D_FF + c1], preferred_element_type=F32)
        act = (gg * jax.nn.sigmoid(gg) * uu).astype(BF16)
        acc = acc + jnp.dot(act, wd_ref[c0:c1, :], preferred_element_type=F32)
    x2 = x1 + g2 * acc
    o_ref[...] = _rms(x2) * fw_ref[...]


def _ffn_call(x2d, attn_o, hm, mods, tiles_per_mod, wo, n2w, wgu, wd, fw):
    T = x2d.shape[0]
    tm = TOKEN_TILE
    if tiles_per_mod is None:
        mod_map = lambda i: (0, 0, 0)
    else:
        mod_map = lambda i: (i // tiles_per_mod, 0, 0)
    return pl.pallas_call(
        _ffn_kernel,
        out_shape=jax.ShapeDtypeStruct((T, D_MODEL), F32),
        grid=(T // tm,),
        in_specs=[pl.BlockSpec((tm, D_MODEL), lambda i: (i, 0)),
                  pl.BlockSpec((tm, ATTN_WIDTH), lambda i: (i, 0)),
                  pl.BlockSpec((tm, MLSTM_WIDTH), lambda i: (i, 0)),
                  pl.BlockSpec((1, 6, D_MODEL), mod_map),
                  _resident(wo.shape), _resident((1, D_MODEL)), _resident(wgu.shape),
                  _resident(wd.shape), _resident((1, D_MODEL))],
        out_specs=pl.BlockSpec((tm, D_MODEL), lambda i: (i, 0)),
        compiler_params=pltpu.CompilerParams(dimension_semantics=("parallel",),
                                             vmem_limit_bytes=VMEM_LIMIT),
        name="out_proj_ffn",
    )(x2d, attn_o, hm, mods, wo, n2w, wgu, wd, fw)


def _rope_tables(seq, gain, scale):
    pos = jnp.arange(seq, dtype=jnp.int32)
    row_ids = (pos // GRID_W).astype(F32)
    col_ids = (pos % GRID_W).astype(F32)
    inv = ROPE_THETA ** (-jnp.arange(0, AXIS_DIM, 2, dtype=F32) / AXIS_DIM)
    ang_r = inv[:, None] * row_ids[None, :]
    ang_c = inv[:, None] * col_ids[None, :]
    cos = jnp.concatenate([jnp.cos(ang_r)] * 2 + [jnp.cos(ang_c)] * 2, axis=0)
    sin = jnp.concatenate([-jnp.sin(ang_r), jnp.sin(ang_r), -jnp.sin(ang_c), jnp.sin(ang_c)], axis=0)
    g = gain.astype(F32) * scale
    g_sw = jnp.concatenate([g[16:32], g[0:16], g[48:64], g[32:48]])
    return g[:, None] * cos, g_sw[:, None] * sin


def _flat_tables(gain, scale, width):
    g = gain.astype(F32) * scale
    return jnp.broadcast_to(g[:, None], (HEAD_DIM, width)), jnp.zeros((HEAD_DIM, width), F32)


def _group(x, mods, tiles_per_mod, seq, weights, tables, tiles_per_table, cache, c0, m0, emit_cache):
    n_batch = x.shape[0]
    x2d = x.reshape(n_batch * seq, D_MODEL)
    outs = _proj_call(x2d, mods, tiles_per_mod, weights["n1w"], weights["wt"], weights["wn"],
                      weights["gbias"], tables, tiles_per_table, emit_cache)
    qT, k, vT, mkT, gates, mq, mv, og = outs[:8]
    n_keys = seq
    if cache is not None:
        ck, cvT = cache
        past = ck.shape[1]
        n_keys = seq + past
        k = jnp.concatenate([k.reshape(n_batch, seq, KV_WIDTH), ck], axis=1).reshape(n_batch * n_keys, KV_WIDTH)
        v_rows = N_KV_HEADS * V_ROWS
        vT = jnp.concatenate([vT.reshape(n_batch, seq // KEY_BLOCK, v_rows, KEY_BLOCK), cvT], axis=1)
        vT = vT.reshape(n_batch * n_keys // KEY_BLOCK, v_rows, KEY_BLOCK)
    q_tiles = Q_TILES_PER_STEP if seq % (Q_TILE * Q_TILES_PER_STEP) == 0 else 1
    attn_o = _attn_call(qT, k, vT, n_batch, seq, n_keys, q_tiles)
    cxf, cxb, mf, mb, cfin, mfin = _scan_call(mkT, mv, gates, c0, m0, n_batch, seq)
    hm = _mlstm_out_call(mq, mkT, mv, gates, cxf, cxb, mf, mb, og, weights["mnw"], n_batch, seq)
    y = _ffn_call(x2d, attn_o, hm, mods, tiles_per_mod, weights["wo"], weights["n2w"], weights["wgu"],
                  weights["wd"], weights["fw"])
    return y.reshape(n_batch, seq, D_MODEL), outs[8:], cfin, mfin


def kernel(x_prompt, x_sample, cache_k, cache_v, state_C, state_n, state_m, c, c_ctx, w_ada, b_ada,
           norm1_w, w_in, gate_bias, q_norm_w, k_norm_w, mlstm_norm_w, w_out, norm2_w, w_gu, w_down,
           final_norm_w):
    depth = w_ada.shape[0]
    assert depth == 1, "single-layer trunk"
    n_ctx, ctx_len, _ = x_prompt.shape
    n_dec, dec_len, _ = x_sample.shape
    l = 0

    cond = jnp.zeros((16, D_MODEL), F32).at[:n_dec].set(c).at[n_dec].set(c_ctx)
    mods = _mods_call(cond, w_ada[l], b_ada[l]).reshape(16, 6, D_MODEL)

    wi = w_in[l]
    aq, ak, av, mq_w, mk_w, mv_w, mo_w, mg_w = jnp.split(
        wi, [512, 640, 768, 1280, 1792, 2304, 2816], axis=1)
    weights = {
        "wt": jnp.concatenate([aq, ak, av, mk_w, mg_w], axis=1).T.astype(BF16),
        "wn": jnp.concatenate([mq_w, mv_w, mo_w], axis=1).astype(BF16),
        "gbias": gate_bias[l].reshape(N_GATES, 1).astype(F32),
        "n1w": norm1_w[l].reshape(1, D_MODEL),
        "n2w": norm2_w[l].reshape(1, D_MODEL),
        "mnw": mlstm_norm_w[l].reshape(1, MLSTM_WIDTH),
        "wo": w_out[l].astype(BF16),
        "wgu": w_gu[l].astype(BF16),
        "wd": w_down[l].astype(BF16),
        "fw": final_norm_w.reshape(1, D_MODEL),
    }
    q_scale = (HEAD_DIM ** -0.5) * LOG2E

    tabs_ctx = _flat_tables(q_norm_w[l], q_scale, TOKEN_TILE) + _flat_tables(k_norm_w[l], 1.0, TOKEN_TILE)
    y_prompt, (kc, vc), cfin, mfin = _group(x_prompt, mods[n_dec:n_dec + 1], None, ctx_len, weights,
                                            tabs_ctx, None, None, None, None, True)
    new_cache_k = kc.reshape(n_ctx, 1, ctx_len, N_KV_HEADS, HEAD_DIM)
    new_cache_v = vc.reshape(n_ctx, 1, ctx_len, N_KV_HEADS, HEAD_DIM)
    new_state_C = cfin[..., :MLSTM_DK][:, None]
    new_state_n = cfin[..., MLSTM_DK][:, None]
    new_state_m = mfin[..., 0][:, None]

    tabs_dec = _rope_tables(dec_len, q_norm_w[l], q_scale) + _rope_tables(dec_len, k_norm_w[l], 1.0)
    past = cache_k.shape[2]
    ck = cache_k[:, l].reshape(n_dec, past, KV_WIDTH).astype(BF16)
    cvT = cache_v[:, l].reshape(n_dec, past // KEY_BLOCK, KEY_BLOCK, KV_WIDTH).transpose(0, 1, 3, 2)
    pad_ones = jnp.ones(cvT.shape[:2] + (V_ROWS - HEAD_DIM, KEY_BLOCK), cvT.dtype)
    cvT = jnp.concatenate([cvT[:, :, :HEAD_DIM], pad_ones, cvT[:, :, HEAD_DIM:], pad_ones], axis=2).astype(BF16)
    sC = state_C[:, l].astype(F32)
    sn = state_n[:, l].astype(F32)
    c0 = jnp.concatenate([sC, jnp.broadcast_to(sn[..., None], sC.shape)], axis=-1)
    m0 = jnp.broadcast_to(state_m[:, l].astype(F32)[..., None], (n_dec, 2, MLSTM_HEADS, CHUNK))
    y_sample, _, _, _ = _group(x_sample, mods[:n_dec], dec_len // TOKEN_TILE, dec_len, weights,
                               tabs_dec, dec_len // TOKEN_TILE, (ck, cvT), c0, m0, False)

    return (y_prompt, y_sample, new_cache_k, new_cache_v, new_state_C, new_state_n, new_state_m)
```

```python
import functools

import jax
import jax.numpy as jnp
from jax import lax
from jax.experimental import pallas as pl
from jax.experimental.pallas import tpu as pltpu

F32 = jnp.float32
BF16 = jnp.bfloat16

D_MODEL = 1024
HEAD_DIM = 64
N_Q_HEADS = 8
N_KV_HEADS = 2
KV_GROUP = N_Q_HEADS // N_KV_HEADS
ATTN_WIDTH = N_Q_HEADS * HEAD_DIM
KV_WIDTH = N_KV_HEADS * HEAD_DIM
AXIS_DIM = HEAD_DIM // 2
GRID_W = 64
ROPE_THETA = 10000.0
MLSTM_HEADS = 4
MLSTM_DK = 128
MLSTM_WIDTH = MLSTM_HEADS * MLSTM_DK
CHUNK = 128
D_FF = 2816
N_GATES = 4 * MLSTM_HEADS
EPS = 1e-6
LOG2E = 1.4426950408889634
NEG_BIG = -1e30

TOKEN_TILE = 512
PROJ_TILE = 1024
PROJ_SUB = 256
Q_TILE = 256
KEY_BLOCK = 256
QK_LOOKAHEAD = 6
S_SLOTS = 8
Q_TILES_PER_STEP = 8
V_ROWS = 80
SCAN_CHUNKS = 8
OUT_CHUNKS = 4
FF_CHUNK = 1024
ADA_TILE = 1536
VMEM_LIMIT = 56 * 1024 * 1024

_R_Q, _R_K, _R_V, _R_MQ, _R_MV, _R_MO, _R_G, _R_END = 0, 512, 640, 768, 1280, 1792, 2304, 2320
MV_ROWS = MLSTM_DK + 16


def _resident(shape):
    nd = len(shape)
    return pl.BlockSpec(shape, lambda *_: (0,) * nd, pipeline_mode=pl.Buffered(1))


def _mods_kernel(cond_ref, w_ref, b_ref, o_ref):
    c = cond_ref[...]
    s = c * jax.nn.sigmoid(c)
    o_ref[...] = jnp.dot(s, w_ref[...], preferred_element_type=F32) + b_ref[...]


def _mods_call(cond, w_ada, b_ada):
    rows, n_out = cond.shape[0], w_ada.shape[1]
    return pl.pallas_call(
        _mods_kernel,
        out_shape=jax.ShapeDtypeStruct((rows, n_out), F32),
        grid=(n_out // ADA_TILE,),
        in_specs=[pl.BlockSpec((rows, D_MODEL), lambda j: (0, 0)),
                  pl.BlockSpec((D_MODEL, ADA_TILE), lambda j: (0, j)),
                  pl.BlockSpec((1, ADA_TILE), lambda j: (0, j))],
        out_specs=pl.BlockSpec((rows, ADA_TILE), lambda j: (0, j)),
        compiler_params=pltpu.CompilerParams(dimension_semantics=("parallel",),
                                             vmem_limit_bytes=VMEM_LIMIT),
        name="adaln_mods",
    )(cond, w_ada, b_ada.reshape(1, n_out))


def _rms(x):
    return x * lax.rsqrt(jnp.mean(x * x, axis=-1, keepdims=True) + EPS)


def _proj_kernel(x_ref, mods_ref, n1w_ref, wt_ref, wn_ref, gb_ref, tqa_ref, tqb_ref, tka_ref, tkb_ref,
                 qT_ref, k_ref, vT_ref, mqT_ref, mk_ref, mvT_ref, ogT_ref, g_ref, *cache_refs):
    sh = mods_ref[0, 0:1, :]
    sc = mods_ref[0, 1:2, :]
    sub = PROJ_SUB
    for a in range(0, x_ref.shape[0], sub):
        tok = slice(a, a + sub)
        h = (_rms(x_ref[tok, :]) * n1w_ref[...] * (1.0 + sc) + sh).astype(BF16)

        yt = lax.dot_general(wt_ref[...], h, (((1,), (1,)), ((), ())), preferred_element_type=F32)

        n_qk = N_Q_HEADS + N_KV_HEADS
        qk = yt[_R_Q:_R_V].reshape(n_qk, HEAD_DIM, sub)
        qk = qk * lax.rsqrt(jnp.sum(qk * qk, axis=1, keepdims=True) * (1.0 / HEAD_DIM) + EPS)
        sw = jnp.concatenate([qk[:, 16:32], qk[:, 0:16], qk[:, 48:64], qk[:, 32:48]], axis=1)
        q = qk[:N_Q_HEADS] * tqa_ref[:, tok][None] + sw[:N_Q_HEADS] * tqb_ref[:, tok][None]
        q = q.reshape(ATTN_WIDTH, sub).astype(BF16)
        for j in range(sub // Q_TILE):
            qT_ref[a // Q_TILE + j] = q[:, j * Q_TILE:(j + 1) * Q_TILE]
        k = qk[N_Q_HEADS:] * tka_ref[:, tok][None] + sw[N_Q_HEADS:] * tkb_ref[:, tok][None]
        kt = k.reshape(KV_WIDTH, sub).T
        k_ref[tok, :] = kt.astype(BF16)

        vt = yt[_R_V:_R_MQ]
        ones = jnp.ones((V_ROWS - HEAD_DIM, sub), F32)
        vtb = jnp.concatenate([vt[:HEAD_DIM], ones, vt[HEAD_DIM:], ones], axis=0).astype(BF16)
        for j in range(sub // KEY_BLOCK):
            vT_ref[a // KEY_BLOCK + j] = vtb[:, j * KEY_BLOCK:(j + 1) * KEY_BLOCK]
        if cache_refs:
            kc_ref, vc_ref = cache_refs
            kc_ref[tok, :] = kt
            vc_ref[tok, :] = vt.T

        mq = yt[_R_MQ:_R_MV].astype(BF16)
        mv = yt[_R_MV:_R_MO]
        pad = jnp.ones((MV_ROWS - MLSTM_DK, sub), F32)
        mvx = jnp.concatenate(
            [blk for hh in range(MLSTM_HEADS) for blk in (mv[hh * MLSTM_DK:(hh + 1) * MLSTM_DK], pad)],
            axis=0).astype(BF16)
        og = jax.nn.sigmoid(yt[_R_MO:_R_G]).astype(BF16)
        gates = yt[_R_G:_R_END] + gb_ref[...]
        for j in range(sub // CHUNK):
            cs = slice(j * CHUNK, (j + 1) * CHUNK)
            mqT_ref[a // CHUNK + j] = mq[:, cs]
            mvT_ref[a // CHUNK + j] = mvx[:, cs]
            ogT_ref[a // CHUNK + j] = og[:, cs]
            g_ref[a // CHUNK + j] = gates[:, cs]

        mk = jnp.dot(h, wn_ref[...], preferred_element_type=F32)
        mk_ref[tok, :] = (mk * (MLSTM_DK ** -0.5)).astype(BF16)


def _mod_map(per_request, seq, tm):
    if not per_request:
        return lambda i: (0, 0, 0)
    tiles = seq // tm
    return lambda i: (i // tiles, 0, 0)


def _proj_call(x2d, mods, per_request, seq, n1w, wt, wn, gbias, tables, emit_cache):
    T = x2d.shape[0]
    tm = PROJ_TILE
    n_tiles = T // tm
    mod_map = _mod_map(per_request, seq, tm)
    if per_request:
        tiles = seq // tm
        tab_map = lambda i: (0, i % tiles)
    else:
        tab_map = lambda i: (0, 0)
    tab_spec = pl.BlockSpec((HEAD_DIM, tm), tab_map)
    out_shape = [
        jax.ShapeDtypeStruct((T // Q_TILE, ATTN_WIDTH, Q_TILE), BF16),
        jax.ShapeDtypeStruct((T, KV_WIDTH), BF16),
        jax.ShapeDtypeStruct((T // KEY_BLOCK, N_KV_HEADS * V_ROWS, KEY_BLOCK), BF16),
        jax.ShapeDtypeStruct((T // CHUNK, MLSTM_WIDTH, CHUNK), BF16),
        jax.ShapeDtypeStruct((T, MLSTM_WIDTH), BF16),
        jax.ShapeDtypeStruct((T // CHUNK, MLSTM_HEADS * MV_ROWS, CHUNK), BF16),
        jax.ShapeDtypeStruct((T // CHUNK, MLSTM_WIDTH, CHUNK), BF16),
        jax.ShapeDtypeStruct((T // CHUNK, N_GATES, CHUNK), F32),
    ]
    out_specs = [
        pl.BlockSpec((tm // Q_TILE, ATTN_WIDTH, Q_TILE), lambda i: (i, 0, 0)),
        pl.BlockSpec((tm, KV_WIDTH), lambda i: (i, 0)),
        pl.BlockSpec((tm // KEY_BLOCK, N_KV_HEADS * V_ROWS, KEY_BLOCK), lambda i: (i, 0, 0)),
        pl.BlockSpec((tm // CHUNK, MLSTM_WIDTH, CHUNK), lambda i: (i, 0, 0)),
        pl.BlockSpec((tm, MLSTM_WIDTH), lambda i: (i, 0)),
        pl.BlockSpec((tm // CHUNK, MLSTM_HEADS * MV_ROWS, CHUNK), lambda i: (i, 0, 0)),
        pl.BlockSpec((tm // CHUNK, MLSTM_WIDTH, CHUNK), lambda i: (i, 0, 0)),
        pl.BlockSpec((tm // CHUNK, N_GATES, CHUNK), lambda i: (i, 0, 0)),
    ]
    if emit_cache:
        out_shape += [jax.ShapeDtypeStruct((T, KV_WIDTH), F32)] * 2
        out_specs += [pl.BlockSpec((tm, KV_WIDTH), lambda i: (i, 0))] * 2
    return pl.pallas_call(
        _proj_kernel,
        out_shape=out_shape,
        grid=(n_tiles,),
        in_specs=[pl.BlockSpec((tm, D_MODEL), lambda i: (i, 0)),
                  pl.BlockSpec((1, 6, D_MODEL), mod_map),
                  _resident((1, D_MODEL)),
                  _resident(wt.shape),
                  _resident(wn.shape),
                  _resident((N_GATES, 1)),
                  tab_spec, tab_spec, tab_spec, tab_spec],
        out_specs=out_specs,
        compiler_params=pltpu.CompilerParams(dimension_semantics=("parallel",),
                                             vmem_limit_bytes=VMEM_LIMIT),
        name="norm1_in_proj",
    )(x2d, mods, n1w, wt, wn, gbias, *tables)


def _attn_kernel(qT_ref, k_ref, vT_ref, o_ref, qp_scr, m_scr, acc_scr, s_scr, *, n_blocks):
    n_qt, _, tq = qT_ref.shape
    n_tasks = n_qt * N_Q_HEADS

    for t in range(n_tasks):
        g, hd = divmod(t, N_Q_HEADS)
        qg = qT_ref[g, hd * HEAD_DIM:(hd + 1) * HEAD_DIM, :]
        zero = jnp.zeros_like(qg)
        qp_scr[t] = jnp.concatenate([qg, zero] if hd < KV_GROUP else [zero, qg], axis=0)
    m_scr[...] = jnp.full(m_scr.shape, NEG_BIG, F32)
    acc_scr[...] = jnp.zeros(acc_scr.shape, F32)

    def scores(c, t):
        off = pl.multiple_of(c * KEY_BLOCK, KEY_BLOCK)
        return jnp.dot(k_ref[pl.ds(off, KEY_BLOCK), :], qp_scr[t], preferred_element_type=F32)

    assert n_tasks % S_SLOTS == 0 and QK_LOOKAHEAD < S_SLOTS
    for t in range(QK_LOOKAHEAD):
        s_scr[t] = scores(0, t)

    def body(c, _):
        c_next = jnp.minimum(c + 1, n_blocks - 1)
        for t in range(n_tasks):
            s = s_scr[t % S_SLOTS]
            ahead = t + QK_LOOKAHEAD
            if ahead < n_tasks:
                s_scr[ahead % S_SLOTS] = scores(c, ahead)
            else:
                s_scr[ahead % S_SLOTS] = scores(c_next, ahead - n_tasks)
            r0 = ((t % N_Q_HEADS) // KV_GROUP) * V_ROWS
            m = m_scr[t]
            m_new = jnp.maximum(m, jnp.max(s, axis=0, keepdims=True))
            p = jnp.exp2(s - m_new).astype(BF16)
            alpha = jnp.exp2(m - m_new)
            pv = jnp.dot(vT_ref[c, r0:r0 + V_ROWS, :], p, preferred_element_type=F32)
            acc_scr[t] = alpha * acc_scr[t] + pv
            m_scr[t] = m_new
        return 0

    lax.fori_loop(0, n_blocks, body, 0)
    for g in range(n_qt):
        acc = acc_scr[g * N_Q_HEADS:(g + 1) * N_Q_HEADS]
        out = acc[:, :HEAD_DIM] / acc[:, HEAD_DIM:HEAD_DIM + 1]
        o_ref[g * tq:(g + 1) * tq, :] = out.reshape(ATTN_WIDTH, tq).T.astype(BF16)


def _attn_call(qT, k, vT, n_batch, seq, n_keys, q_tiles):
    nq = seq // (Q_TILE * q_tiles)
    nkb = n_keys // KEY_BLOCK
    n_tasks = q_tiles * N_Q_HEADS
    return pl.pallas_call(
        functools.partial(_attn_kernel, n_blocks=nkb),
        out_shape=jax.ShapeDtypeStruct((n_batch * seq, ATTN_WIDTH), BF16),
        grid=(n_batch, nq),
        in_specs=[pl.BlockSpec((q_tiles, ATTN_WIDTH, Q_TILE), lambda b, i: (b * nq + i, 0, 0)),
                  pl.BlockSpec((n_keys, KV_WIDTH), lambda b, i: (b, 0)),
                  pl.BlockSpec((nkb, N_KV_HEADS * V_ROWS, KEY_BLOCK), lambda b, i: (b, 0, 0))],
        out_specs=pl.BlockSpec((q_tiles * Q_TILE, ATTN_WIDTH), lambda b, i: (b * nq + i, 0)),
        scratch_shapes=[pltpu.VMEM((n_tasks, KV_WIDTH, Q_TILE), BF16),
                        pltpu.VMEM((n_tasks, 1, Q_TILE), F32),
                        pltpu.VMEM((n_tasks, V_ROWS, Q_TILE), F32),
                        pltpu.VMEM((S_SLOTS, KEY_BLOCK, Q_TILE), F32)],
        compiler_params=pltpu.CompilerParams(dimension_semantics=("parallel", "parallel"),
                                             vmem_limit_bytes=VMEM_LIMIT),
        name="gqa_attention",
    )(qT, k, vT)


def _log_sigmoid(x):
    return jnp.minimum(x, 0.0) - jnp.log1p(jnp.exp(-jnp.abs(x)))


def _cumsum_lanes(x, tri):
    hi = x.astype(BF16)
    r1 = x - hi.astype(F32)
    mid = r1.astype(BF16)
    lo = (r1 - mid.astype(F32)).astype(BF16)
    return (jnp.dot(hi, tri, preferred_element_type=F32) + jnp.dot(mid, tri, preferred_element_type=F32)
            + jnp.dot(lo, tri, preferred_element_type=F32))


def _tri_masks():
    ri = lax.broadcasted_iota(jnp.int32, (CHUNK, CHUNK), 0)
    ci = lax.broadcasted_iota(jnp.int32, (CHUNK, CHUNK), 1)
    return ri <= ci, ri >= ci


def _gate_rows(gs, d, le, ge):
    nh = MLSTM_HEADS
    lf = _log_sigmoid(jnp.concatenate([g[8 * d + nh:8 * d + 2 * nh] for g in gs], axis=0))
    tri = (le if d == 0 else ge).astype(BF16)
    b = _cumsum_lanes(lf, tri)
    return [(lf[nh * j:nh * (j + 1)], b[nh * j:nh * (j + 1)], g[8 * d:8 * d + nh]) for j, g in enumerate(gs)]


def _scan_kernel(*refs, has_init):
    kf_ref, vf_ref, gf_ref, kb_ref, vb_ref, gb_ref = refs[:6]
    cxf_ref, cxb_ref, mf_ref, mb_ref, cfin_ref, mfin_ref = refs[-6:]

    @pl.when(pl.program_id(1) == 0)
    def _():
        if has_init:
            c0_ref, m0_ref = refs[6:8]
            cfin_ref[...] = c0_ref[...]
            mfin_ref[...] = m0_ref[...]
        else:
            cfin_ref[...] = jnp.zeros(cfin_ref.shape, F32)
            mfin_ref[...] = jnp.zeros(mfin_ref.shape, F32)

    n_ch = gf_ref.shape[0]
    le, ge = _tri_masks()
    dirs = ((kf_ref, vf_ref, gf_ref, cxf_ref, mf_ref), (kb_ref, vb_ref, gb_ref, cxb_ref, mb_ref))
    gate = []
    for d, (_, _, g_ref, _, _) in enumerate(dirs):
        per_chunk = []
        for lf, b, ig in _gate_rows([g_ref[j] for j in range(n_ch)], d, le, ge):
            b_last = jnp.sum(lf, axis=1, keepdims=True)
            gt = b_last - b + ig
            a = jnp.max(gt, axis=1, keepdims=True)
            per_chunk.append((b_last, a, jnp.exp(gt - a)))
        gate.append(per_chunk)
    prods = []
    for d, (k_ref, v_ref, _, _, _) in enumerate(dirs):
        per_chunk = []
        for j in range(n_ch):
            wt = gate[d][j][2]
            us = []
            for h in range(MLSTM_HEADS):
                vw = (v_ref[j, h * MV_ROWS:(h + 1) * MV_ROWS, :].astype(F32) * wt[h:h + 1, :]).astype(BF16)
                kh = k_ref[j * CHUNK:(j + 1) * CHUNK, h * MLSTM_DK:(h + 1) * MLSTM_DK]
                us.append(jnp.dot(vw, kh, preferred_element_type=F32))
            per_chunk.append(us)
        prods.append(per_chunk)
    for d, (_, _, _, cx_ref, mo_ref) in enumerate(dirs):
        local = [gate[d][j][:2] + (prods[d][j],) for j in range(n_ch)]
        m_run = mfin_ref[0, d][:, 0:1]
        state = [cfin_ref[0, d, h] for h in range(MLSTM_HEADS)]
        for j in (range(n_ch) if d == 0 else reversed(range(n_ch))):
            b_last, a, us = local[j]
            m_new = jnp.maximum(a, b_last + m_run)
            decay = jnp.exp(b_last + m_run - m_new)
            gam = jnp.exp(a - m_new)
            mo_ref[0, j] = jnp.broadcast_to(m_run, (MLSTM_HEADS, CHUNK))
            for h in range(MLSTM_HEADS):
                cx_ref[0, j, h] = state[h].astype(BF16)
                state[h] = decay[h:h + 1, :] * state[h] + gam[h:h + 1, :] * us[h]
            m_run = m_new
        mfin_ref[0, d] = jnp.broadcast_to(m_run, (MLSTM_HEADS, CHUNK))
        for h in range(MLSTM_HEADS):
            cfin_ref[0, d, h] = state[h]


def _scan_call(mk, mvT, gates, c0, m0, n_batch, seq):
    nc = seq // CHUNK
    g = next(n for n in (SCAN_CHUNKS, 2, 1) if nc % n == 0)
    ns = nc // g
    fwd3 = lambda b, c: (b * ns + c, 0, 0)
    bwd3 = lambda b, c: (b * ns + ns - 1 - c, 0, 0)
    fwd2 = lambda b, c: (b * ns + c, 0)
    bwd2 = lambda b, c: (b * ns + ns - 1 - c, 0)
    k_blk, v_blk, g_blk = (g * CHUNK, MLSTM_WIDTH), (g, MLSTM_HEADS * MV_ROWS, CHUNK), (g, N_GATES, CHUNK)
    cx_blk = (1, g, MLSTM_HEADS, MV_ROWS, MLSTM_DK)
    m_blk = (1, g, MLSTM_HEADS, CHUNK)
    st_blk = (1, 2, MLSTM_HEADS, MV_ROWS, MLSTM_DK)
    sm_blk = (1, 2, MLSTM_HEADS, CHUNK)
    in_specs = [pl.BlockSpec(k_blk, fwd2), pl.BlockSpec(v_blk, fwd3), pl.BlockSpec(g_blk, fwd3),
                pl.BlockSpec(k_blk, bwd2), pl.BlockSpec(v_blk, bwd3), pl.BlockSpec(g_blk, bwd3)]
    args = [mk, mvT, gates, mk, mvT, gates]
    if c0 is not None:
        in_specs += [pl.BlockSpec(st_blk, lambda b, c: (b, 0, 0, 0, 0)),
                     pl.BlockSpec(sm_blk, lambda b, c: (b, 0, 0, 0))]
        args += [c0, m0]
    return pl.pallas_call(
        functools.partial(_scan_kernel, has_init=c0 is not None),
        out_shape=[jax.ShapeDtypeStruct((n_batch, nc) + cx_blk[2:], BF16),
                   jax.ShapeDtypeStruct((n_batch, nc) + cx_blk[2:], BF16),
                   jax.ShapeDtypeStruct((n_batch, nc) + m_blk[2:], F32),
                   jax.ShapeDtypeStruct((n_batch, nc) + m_blk[2:], F32),
                   jax.ShapeDtypeStruct((n_batch,) + st_blk[1:], F32),
                   jax.ShapeDtypeStruct((n_batch,) + sm_blk[1:], F32)],
        grid=(n_batch, ns),
        in_specs=in_specs,
        out_specs=[pl.BlockSpec(cx_blk, lambda b, c: (b, c, 0, 0, 0)),
                   pl.BlockSpec(cx_blk, lambda b, c: (b, ns - 1 - c, 0, 0, 0)),
                   pl.BlockSpec(m_blk, lambda b, c: (b, c, 0, 0)),
                   pl.BlockSpec(m_blk, lambda b, c: (b, ns - 1 - c, 0, 0)),
                   pl.BlockSpec(st_blk, lambda b, c: (b, 0, 0, 0, 0)),
                   pl.BlockSpec(sm_blk, lambda b, c: (b, 0, 0, 0))],
        compiler_params=pltpu.CompilerParams(dimension_semantics=("parallel", "arbitrary"),
                                             vmem_limit_bytes=VMEM_LIMIT),
        name="mlstm_state_scan",
    )(*args)


def _mlstm_out_kernel(qT_ref, k_ref, vT_ref, g_ref, cxf_ref, cxb_ref, mf_ref, mb_ref, ogT_ref, nw_ref, o_ref):
    n_ch = g_ref.shape[0]
    nh = MLSTM_HEADS
    le, ge = _tri_masks()
    gs = [g_ref[j] for j in range(n_ch)]
    gate = [_gate_rows(gs, d, le, ge) for d in range(2)]
    kq = {}
    for j in range(n_ch):
        for h in range(nh):
            sl = slice(h * MLSTM_DK, (h + 1) * MLSTM_DK)
            kq[j, h] = jnp.dot(k_ref[j * CHUNK:(j + 1) * CHUNK, sl], qT_ref[j, sl, :],
                               preferred_element_type=F32)
    for j in range(n_ch):
        tok = slice(j * CHUNK, (j + 1) * CHUNK)
        r2, b2 = [], []
        for d in range(2):
            lf, b, ig = gate[d][j]
            r2.append((ig - b) * LOG2E)
            b2.append(b * LOG2E)
        r_cols = jnp.concatenate(r2, axis=0).T
        for h in range(nh):
            sl = slice(h * MLSTM_DK, (h + 1) * MLSTM_DK)
            qT = qT_ref[j, sl, :].astype(F32)
            vx = vT_ref[j, h * MV_ROWS:(h + 1) * MV_ROWS, :]
            sT = kq[j, h]
            h_sum = jnp.zeros((MLSTM_DK, CHUNK), F32)
            for d, (cx_ref, m_ref) in enumerate(((cxf_ref, mf_ref), (cxb_ref, mb_ref))):
                mask = le if d == 0 else ge
                i = d * nh + h
                rm = jnp.where(mask, jnp.broadcast_to(r_cols[:, i:i + 1], (CHUNK, CHUNK)), NEG_BIG)
                cm = jnp.max(rm, axis=0, keepdims=True)
                m_prev = m_ref[0, j, h:h + 1, :] * LOG2E
                mm = jnp.maximum(cm, m_prev)
                pT = (sT * jnp.exp2(rm - mm)).astype(BF16)
                qi = (qT * jnp.exp2(m_prev - mm)).astype(BF16)
                lhs = jnp.concatenate([vx, cx_ref[0, j, h]], axis=1)
                mix = jnp.dot(lhs, jnp.concatenate([pT, qi], axis=0), preferred_element_type=F32)
                den = mix[MLSTM_DK:MLSTM_DK + 1]
                clamp = jnp.exp2(-(b2[d][h:h + 1, :] + mm))
                h_sum = h_sum + mix[:MLSTM_DK] * (1.0 / jnp.maximum(jnp.abs(den), clamp))
            ms = jnp.mean(h_sum * h_sum, axis=0, keepdims=True)
            hn = h_sum * lax.rsqrt(ms + EPS) * nw_ref[sl, :] * ogT_ref[j, sl, :].astype(F32)
            o_ref[tok, sl] = hn.T.astype(BF16)


def _mlstm_out_call(mqT, mk, mvT, gates, cxf, cxb, mf, mb, ogT, nwb, n_batch, seq):
    g = next(n for n in (OUT_CHUNKS, 2, 1) if (seq // CHUNK) % n == 0)
    nc = seq // (CHUNK * g)
    tok = lambda b, c: (b * nc + c, 0)
    chk = lambda b, c: (b * nc + c, 0, 0)
    cx_blk = (1, g, MLSTM_HEADS, MV_ROWS, MLSTM_DK)
    m_blk = (1, g, MLSTM_HEADS, CHUNK)
    return pl.pallas_call(
        _mlstm_out_kernel,
        out_shape=jax.ShapeDtypeStruct((n_batch * seq, MLSTM_WIDTH), BF16),
        grid=(n_batch, nc),
        in_specs=[pl.BlockSpec((g, MLSTM_WIDTH, CHUNK), chk),
                  pl.BlockSpec((g * CHUNK, MLSTM_WIDTH), tok),
                  pl.BlockSpec((g, MLSTM_HEADS * MV_ROWS, CHUNK), chk),
                  pl.BlockSpec((g, N_GATES, CHUNK), chk),
                  pl.BlockSpec(cx_blk, lambda b, c: (b, c, 0, 0, 0)),
                  pl.BlockSpec(cx_blk, lambda b, c: (b, c, 0, 0, 0)),
                  pl.BlockSpec(m_blk, lambda b, c: (b, c, 0, 0)),
                  pl.BlockSpec(m_blk, lambda b, c: (b, c, 0, 0)),
                  pl.BlockSpec((g, MLSTM_WIDTH, CHUNK), chk),
                  pl.BlockSpec((MLSTM_WIDTH, CHUNK), lambda b, c: (0, 0))],
        out_specs=pl.BlockSpec((g * CHUNK, MLSTM_WIDTH), tok),
        compiler_params=pltpu.CompilerParams(dimension_semantics=("parallel", "parallel"),
                                             vmem_limit_bytes=VMEM_LIMIT),
        name="mlstm_chunk_out",
    )(mqT, mk, mvT, gates, cxf, cxb, mf, mb, ogT, nwb)


def _ffn_kernel(x_ref, a_ref, hm_ref, mods_ref, wo_ref, n2w_ref, wgu_ref, wd_ref, fw_ref, o_ref):
    mix = (jnp.dot(a_ref[...], wo_ref[0:ATTN_WIDTH, :], preferred_element_type=F32)
           + jnp.dot(hm_ref[...], wo_ref[ATTN_WIDTH:D_MODEL, :], preferred_element_type=F32))
    g1 = mods_ref[0, 2:3, :]
    sh2 = mods_ref[0, 3:4, :]
    sc2 = mods_ref[0, 4:5, :]
    g2 = mods_ref[0, 5:6, :]
    x1 = x_ref[...] + g1 * mix
    h2 = (_rms(x1) * n2w_ref[...] * (1.0 + sc2) + sh2).astype(BF16)
    acc = jnp.zeros(x1.shape, F32)
    for c0 in range(0, D_FF, FF_CHUNK):
        c1 = min(c0 + FF_CHUNK, D_FF)
        gg = jnp.dot(h2, wgu_ref[:, c0:c1], preferred_element_type=F32)
        uu = jnp.dot(h2, wgu_ref[:, D_FF + c0:D_FF + c1], preferred_element_type=F32)
        act = (gg * jax.nn.sigmoid(gg) * uu).astype(BF16)
        acc = acc + jnp.dot(act, wd_ref[c0:c1, :], preferred_element_type=F32)
    x2 = x1 + g2 * acc
    o_ref[...] = _rms(x2) * fw_ref[...]


def _ffn_call(x2d, attn_o, hm, mods, per_request, seq, wo, n2w, wgu, wd, fw):
    T = x2d.shape[0]
    tm = TOKEN_TILE
    mod_map = _mod_map(per_request, seq, tm)
    return pl.pallas_call(
        _ffn_kernel,
        out_shape=jax.ShapeDtypeStruct((T, D_MODEL), F32),
        grid=(T // tm,),
        in_specs=[pl.BlockSpec((tm, D_MODEL), lambda i: (i, 0)),
                  pl.BlockSpec((tm, ATTN_WIDTH), lambda i: (i, 0)),
                  pl.BlockSpec((tm, MLSTM_WIDTH), lambda i: (i, 0)),
                  pl.BlockSpec((1, 6, D_MODEL), mod_map),
                  _resident(wo.shape), _resident((1, D_MODEL)), _resident(wgu.shape),
                  _resident(wd.shape), _resident((1, D_MODEL))],
        out_specs=pl.BlockSpec((tm, D_MODEL), lambda i: (i, 0)),
        compiler_params=pltpu.CompilerParams(dimension_semantics=("parallel",),
                                             vmem_limit_bytes=VMEM_LIMIT),
        name="out_proj_ffn",
    )(x2d, attn_o, hm, mods, wo, n2w, wgu, wd, fw)


def _rope_tables(seq, gain, scale):
    pos = jnp.arange(seq, dtype=jnp.int32)
    row_ids = (pos // GRID_W).astype(F32)
    col_ids = (pos % GRID_W).astype(F32)
    inv = ROPE_THETA ** (-jnp.arange(0, AXIS_DIM, 2, dtype=F32) / AXIS_DIM)
    ang_r = inv[:, None] * row_ids[None, :]
    ang_c = inv[:, None] * col_ids[None, :]
    cos = jnp.concatenate([jnp.cos(ang_r)] * 2 + [jnp.cos(ang_c)] * 2, axis=0)
    sin = jnp.concatenate([-jnp.sin(ang_r), jnp.sin(ang_r), -jnp.sin(ang_c), jnp.sin(ang_c)], axis=0)
    g = gain.astype(F32) * scale
    g_sw = jnp.concatenate([g[16:32], g[0:16], g[48:64], g[32:48]])
    return g[:, None] * cos, g_sw[:, None] * sin


def _flat_tables(gain, scale, width):
    g = gain.astype(F32) * scale
    return jnp.broadcast_to(g[:, None], (HEAD_DIM, width)), jnp.zeros((HEAD_DIM, width), F32)


def _group(x, mods, per_request, seq, weights, tables, cache, c0, m0, emit_cache):
    n_batch = x.shape[0]
    x2d = x.reshape(n_batch * seq, D_MODEL)
    outs = _proj_call(x2d, mods, per_request, seq, weights["n1w"], weights["wt"], weights["wn"],
                      weights["gbias"], tables, emit_cache)
    qT, k, vT, mqT, mk, mvT, ogT, gates = outs[:8]
    n_keys = seq
    if cache is not None:
        ck, cvT = cache
        past = ck.shape[1]
        n_keys = seq + past
        k = jnp.concatenate([k.reshape(n_batch, seq, KV_WIDTH), ck], axis=1).reshape(n_batch * n_keys, KV_WIDTH)
        v_rows = N_KV_HEADS * V_ROWS
        vT = jnp.concatenate([vT.reshape(n_batch, seq // KEY_BLOCK, v_rows, KEY_BLOCK), cvT], axis=1)
        vT = vT.reshape(n_batch * n_keys // KEY_BLOCK, v_rows, KEY_BLOCK)
    q_tiles = Q_TILES_PER_STEP if seq % (Q_TILE * Q_TILES_PER_STEP) == 0 else 1
    attn_o = _attn_call(qT, k, vT, n_batch, seq, n_keys, q_tiles)
    cxf, cxb, mf, mb, cfin, mfin = _scan_call(mk, mvT, gates, c0, m0, n_batch, seq)
    hm = _mlstm_out_call(mqT, mk, mvT, gates, cxf, cxb, mf, mb, ogT, weights["mnw"], n_batch, seq)
    y = _ffn_call(x2d, attn_o, hm, mods, per_request, seq, weights["wo"], weights["n2w"], weights["wgu"],
                  weights["wd"], weights["fw"])
    return y.reshape(n_batch, seq, D_MODEL), outs[8:], cfin, mfin


def kernel(x_prompt, x_sample, cache_k, cache_v, state_C, state_n, state_m, c, c_ctx, w_ada, b_ada,
           norm1_w, w_in, gate_bias, q_norm_w, k_norm_w, mlstm_norm_w, w_out, norm2_w, w_gu, w_down,
           final_norm_w):
    depth = w_ada.shape[0]
    assert depth == 1, "single-layer trunk"
    n_ctx, ctx_len, _ = x_prompt.shape
    n_dec, dec_len, _ = x_sample.shape
    l = 0

    cond = jnp.zeros((16, D_MODEL), F32).at[:n_dec].set(c).at[n_dec].set(c_ctx)
    mods = _mods_call(cond, w_ada[l], b_ada[l]).reshape(16, 6, D_MODEL)

    wi = w_in[l]
    aq, ak, av, mq_w, mk_w, mv_w, mo_w, mg_w = jnp.split(
        wi, [512, 640, 768, 1280, 1792, 2304, 2816], axis=1)
    weights = {
        "wt": jnp.concatenate([aq, ak, av, mq_w, mv_w, mo_w, mg_w], axis=1).T.astype(BF16),
        "wn": mk_w.astype(BF16),
        "gbias": gate_bias[l].reshape(N_GATES, 1).astype(F32),
        "n1w": norm1_w[l].reshape(1, D_MODEL),
        "n2w": norm2_w[l].reshape(1, D_MODEL),
        "mnw": jnp.broadcast_to(mlstm_norm_w[l].astype(F32)[:, None], (MLSTM_WIDTH, CHUNK)),
        "wo": w_out[l].astype(BF16),
        "wgu": w_gu[l].astype(BF16),
        "wd": w_down[l].astype(BF16),
        "fw": final_norm_w.reshape(1, D_MODEL),
    }
    q_scale = (HEAD_DIM ** -0.5) * LOG2E

    tabs_ctx = _flat_tables(q_norm_w[l], q_scale, PROJ_TILE) + _flat_tables(k_norm_w[l], 1.0, PROJ_TILE)
    y_prompt, (kc, vc), cfin, mfin = _group(x_prompt, mods[n_dec:n_dec + 1], False, ctx_len, weights,
                                            tabs_ctx, None, None, None, True)
    new_cache_k = kc.reshape(n_ctx, 1, ctx_len, N_KV_HEADS, HEAD_DIM)
    new_cache_v = vc.reshape(n_ctx, 1, ctx_len, N_KV_HEADS, HEAD_DIM)
    new_state_C = jnp.swapaxes(cfin[..., :MLSTM_DK, :], -1, -2)[:, None]
    new_state_n = cfin[..., MLSTM_DK, :][:, None]
    new_state_m = mfin[..., 0][:, None]

    tabs_dec = _rope_tables(dec_len, q_norm_w[l], q_scale) + _rope_tables(dec_len, k_norm_w[l], 1.0)
    past = cache_k.shape[2]
    ck = cache_k[:, l].reshape(n_dec, past, KV_WIDTH).astype(BF16)
    cvT = cache_v[:, l].reshape(n_dec, past // KEY_BLOCK, KEY_BLOCK, KV_WIDTH).transpose(0, 1, 3, 2)
    pad_ones = jnp.ones(cvT.shape[:2] + (V_ROWS - HEAD_DIM, KEY_BLOCK), cvT.dtype)
    cvT = jnp.concatenate([cvT[:, :, :HEAD_DIM], pad_ones, cvT[:, :, HEAD_DIM:], pad_ones], axis=2).astype(BF16)
    sC = state_C[:, l].astype(F32)
    sn = state_n[:, l].astype(F32)
    n_rows = jnp.broadcast_to(sn[..., None, :], sn.shape[:-1] + (MV_ROWS - MLSTM_DK, MLSTM_DK))
    c0 = jnp.concatenate([jnp.swapaxes(sC, -1, -2), n_rows], axis=-2)
    m0 = jnp.broadcast_to(state_m[:, l].astype(F32)[..., None], (n_dec, 2, MLSTM_HEADS, CHUNK))
    y_sample, _, _, _ = _group(x_sample, mods[:n_dec], True, dec_len, weights,
                               tabs_dec, (ck, cvT), c0, m0, False)

    return (y_prompt, y_sample, new_cache_k, new_cache_v, new_state_C, new_state_n, new_state_m)
```

```python
import functools

import jax
import jax.numpy as jnp
from jax import lax
from jax.experimental import pallas as pl
from jax.experimental.pallas import tpu as pltpu

F32 = jnp.float32
BF16 = jnp.bfloat16

D_MODEL = 1024
HEAD_DIM = 64
N_Q_HEADS = 8
N_KV_HEADS = 2
KV_GROUP = N_Q_HEADS // N_KV_HEADS
ATTN_WIDTH = N_Q_HEADS * HEAD_DIM
KV_WIDTH = N_KV_HEADS * HEAD_DIM
AXIS_DIM = HEAD_DIM // 2
GRID_W = 64
ROPE_THETA = 10000.0
MLSTM_HEADS = 4
MLSTM_DK = 128
MLSTM_WIDTH = MLSTM_HEADS * MLSTM_DK
CHUNK = 128
D_FF = 2816
N_GATES = 4 * MLSTM_HEADS
EPS = 1e-6
LOG2E = 1.4426950408889634
NEG_BIG = -1e30

TOKEN_TILE = 512
FFN_SUB = 256
PROJ_TILE = 1024
PROJ_SUB = 256
Q_TILE = 256
KEY_BLOCK = 256
QK_LOOKAHEAD = 6
S_SLOTS = 8
Q_TILES_PER_STEP = 16
V_ROWS = 80
SCAN_CHUNKS = 16
OUT_CHUNKS = 4
FF_CHUNK = 1024
ADA_TILE = 1536
VMEM_LIMIT = 56 * 1024 * 1024

_R_Q, _R_K, _R_V, _R_MQ, _R_MV, _R_MO, _R_G, _R_END = 0, 512, 640, 768, 1280, 1792, 2304, 2320
MV_ROWS = MLSTM_DK + 16


def _per_step(n, cap):
    g = cap
    while n % g:
        g //= 2
    return g


def _resident(shape):
    nd = len(shape)
    return pl.BlockSpec(shape, lambda *_: (0,) * nd, pipeline_mode=pl.Buffered(1))


def _mods_kernel(cond_ref, w_ref, b_ref, o_ref):
    c = cond_ref[...]
    s = c * jax.nn.sigmoid(c)
    o_ref[...] = jnp.dot(s, w_ref[...], preferred_element_type=F32) + b_ref[...]


def _mods_call(cond, w_ada, b_ada):
    rows, n_out = cond.shape[0], w_ada.shape[1]
    return pl.pallas_call(
        _mods_kernel,
        out_shape=jax.ShapeDtypeStruct((rows, n_out), F32),
        grid=(n_out // ADA_TILE,),
        in_specs=[pl.BlockSpec((rows, D_MODEL), lambda j: (0, 0)),
                  pl.BlockSpec((D_MODEL, ADA_TILE), lambda j: (0, j)),
                  pl.BlockSpec((1, ADA_TILE), lambda j: (0, j))],
        out_specs=pl.BlockSpec((rows, ADA_TILE), lambda j: (0, j)),
        compiler_params=pltpu.CompilerParams(dimension_semantics=("parallel",),
                                             vmem_limit_bytes=VMEM_LIMIT),
        name="adaln_mods",
    )(cond, w_ada, b_ada.reshape(1, n_out))


def _rms(x):
    return x * lax.rsqrt(jnp.mean(x * x, axis=-1, keepdims=True) + EPS)


def _proj_kernel(x_ref, mods_ref, n1w_ref, wt_ref, wn_ref, gb_ref, tqa_ref, tqb_ref, tka_ref, tkb_ref,
                 qT_ref, k_ref, vT_ref, mqT_ref, mk_ref, mvT_ref, ogT_ref, g_ref, *cache_refs):
    sh = mods_ref[0, 0:1, :]
    sc = mods_ref[0, 1:2, :]
    sub = PROJ_SUB
    for a in range(0, x_ref.shape[0], sub):
        tok = slice(a, a + sub)
        h = (_rms(x_ref[tok, :]) * n1w_ref[...] * (1.0 + sc) + sh).astype(BF16)

        yt = lax.dot_general(wt_ref[...], h, (((1,), (1,)), ((), ())), preferred_element_type=F32)

        n_qk = N_Q_HEADS + N_KV_HEADS
        qk = yt[_R_Q:_R_V].reshape(n_qk, HEAD_DIM, sub)
        qk = qk * lax.rsqrt(jnp.sum(qk * qk, axis=1, keepdims=True) * (1.0 / HEAD_DIM) + EPS)
        sw = jnp.concatenate([qk[:, 16:32], qk[:, 0:16], qk[:, 48:64], qk[:, 32:48]], axis=1)
        q = qk[:N_Q_HEADS] * tqa_ref[:, tok][None] + sw[:N_Q_HEADS] * tqb_ref[:, tok][None]
        q = q.reshape(ATTN_WIDTH, sub).astype(BF16)
        for j in range(sub // Q_TILE):
            qT_ref[a // Q_TILE + j] = q[:, j * Q_TILE:(j + 1) * Q_TILE]
        k = qk[N_Q_HEADS:] * tka_ref[:, tok][None] + sw[N_Q_HEADS:] * tkb_ref[:, tok][None]
        kt = k.reshape(KV_WIDTH, sub).T
        k_ref[tok, :] = kt.astype(BF16)

        vt = yt[_R_V:_R_MQ]
        ones = jnp.ones((V_ROWS - HEAD_DIM, sub), F32)
        vtb = jnp.concatenate([vt[:HEAD_DIM], ones, vt[HEAD_DIM:], ones], axis=0).astype(BF16)
        for j in range(sub // KEY_BLOCK):
            vT_ref[a // KEY_BLOCK + j] = vtb[:, j * KEY_BLOCK:(j + 1) * KEY_BLOCK]
        if cache_refs:
            kc_ref, vc_ref = cache_refs
            kc_ref[tok, :] = kt
            vc_ref[tok, :] = vt.T

        mq = yt[_R_MQ:_R_MV].astype(BF16)
        mv = yt[_R_MV:_R_MO]
        pad = jnp.ones((MV_ROWS - MLSTM_DK, sub), F32)
        mvx = jnp.concatenate(
            [blk for hh in range(MLSTM_HEADS) for blk in (mv[hh * MLSTM_DK:(hh + 1) * MLSTM_DK], pad)],
            axis=0).astype(BF16)
        og = jax.nn.sigmoid(yt[_R_MO:_R_G]).astype(BF16)
        gates = yt[_R_G:_R_END] + gb_ref[...]
        for j in range(sub // CHUNK):
            cs = slice(j * CHUNK, (j + 1) * CHUNK)
            mqT_ref[a // CHUNK + j] = mq[:, cs]
            mvT_ref[a // CHUNK + j] = mvx[:, cs]
            ogT_ref[a // CHUNK + j] = og[:, cs]
            g_ref[a // CHUNK + j] = gates[:, cs]

        mk = jnp.dot(h, wn_ref[...], preferred_element_type=F32)
        mk_ref[tok, :] = (mk * (MLSTM_DK ** -0.5)).astype(BF16)


def _mod_map(per_request, seq, tm):
    if not per_request:
        return lambda i: (0, 0, 0)
    tiles = seq // tm
    return lambda i: (i // tiles, 0, 0)


def _proj_call(x2d, mods, per_request, seq, n1w, wt, wn, gbias, tables, emit_cache):
    T = x2d.shape[0]
    tm = PROJ_TILE
    n_tiles = T // tm
    mod_map = _mod_map(per_request, seq, tm)
    if per_request:
        tiles = seq // tm
        tab_map = lambda i: (0, i % tiles)
    else:
        tab_map = lambda i: (0, 0)
    tab_spec = pl.BlockSpec((HEAD_DIM, tm), tab_map)
    out_shape = [
        jax.ShapeDtypeStruct((T // Q_TILE, ATTN_WIDTH, Q_TILE), BF16),
        jax.ShapeDtypeStruct((T, KV_WIDTH), BF16),
        jax.ShapeDtypeStruct((T // KEY_BLOCK, N_KV_HEADS * V_ROWS, KEY_BLOCK), BF16),
        jax.ShapeDtypeStruct((T // CHUNK, MLSTM_WIDTH, CHUNK), BF16),
        jax.ShapeDtypeStruct((T, MLSTM_WIDTH), BF16),
        jax.ShapeDtypeStruct((T // CHUNK, MLSTM_HEADS * MV_ROWS, CHUNK), BF16),
        jax.ShapeDtypeStruct((T // CHUNK, MLSTM_WIDTH, CHUNK), BF16),
        jax.ShapeDtypeStruct((T // CHUNK, N_GATES, CHUNK), F32),
    ]
    out_specs = [
        pl.BlockSpec((tm // Q_TILE, ATTN_WIDTH, Q_TILE), lambda i: (i, 0, 0)),
        pl.BlockSpec((tm, KV_WIDTH), lambda i: (i, 0)),
        pl.BlockSpec((tm // KEY_BLOCK, N_KV_HEADS * V_ROWS, KEY_BLOCK), lambda i: (i, 0, 0)),
        pl.BlockSpec((tm // CHUNK, MLSTM_WIDTH, CHUNK), lambda i: (i, 0, 0)),
        pl.BlockSpec((tm, MLSTM_WIDTH), lambda i: (i, 0)),
        pl.BlockSpec((tm // CHUNK, MLSTM_HEADS * MV_ROWS, CHUNK), lambda i: (i, 0, 0)),
        pl.BlockSpec((tm // CHUNK, MLSTM_WIDTH, CHUNK), lambda i: (i, 0, 0)),
        pl.BlockSpec((tm // CHUNK, N_GATES, CHUNK), lambda i: (i, 0, 0)),
    ]
    if emit_cache:
        out_shape += [jax.ShapeDtypeStruct((T, KV_WIDTH), F32)] * 2
        out_specs += [pl.BlockSpec((tm, KV_WIDTH), lambda i: (i, 0))] * 2
    return pl.pallas_call(
        _proj_kernel,
        out_shape=out_shape,
        grid=(n_tiles,),
        in_specs=[pl.BlockSpec((tm, D_MODEL), lambda i: (i, 0)),
                  pl.BlockSpec((1, 6, D_MODEL), mod_map),
                  _resident((1, D_MODEL)),
                  _resident(wt.shape),
                  _resident(wn.shape),
                  _resident((N_GATES, 1)),
                  tab_spec, tab_spec, tab_spec, tab_spec],
        out_specs=out_specs,
        compiler_params=pltpu.CompilerParams(dimension_semantics=("parallel",),
                                             vmem_limit_bytes=VMEM_LIMIT),
        name="norm1_in_proj",
    )(x2d, mods, n1w, wt, wn, gbias, *tables)


def _attn_kernel(*refs, n_blocks, has_cache):
    if has_cache:
        qT_ref, k_new_ref, vT_new_ref, ck_ref, cvT_ref, o_ref, qp_scr, m_scr, acc_scr, s_scr, k_ref, vT_ref = refs
        n_new = k_new_ref.shape[0]
        k_ref[0:n_new] = k_new_ref[...]
        k_ref[n_new:] = ck_ref[0]
        vT_ref[0:n_new // KEY_BLOCK] = vT_new_ref[...]
        vT_ref[n_new // KEY_BLOCK:] = cvT_ref[0]
    else:
        qT_ref, k_ref, vT_ref, o_ref, qp_scr, m_scr, acc_scr, s_scr = refs
    n_qt, _, tq = qT_ref.shape
    n_tasks = n_qt * N_Q_HEADS

    for t in range(n_tasks):
        g, hd = divmod(t, N_Q_HEADS)
        qg = qT_ref[g, hd * HEAD_DIM:(hd + 1) * HEAD_DIM, :]
        zero = jnp.zeros_like(qg)
        qp_scr[t] = jnp.concatenate([qg, zero] if hd < KV_GROUP else [zero, qg], axis=0)
    m_scr[...] = jnp.full(m_scr.shape, NEG_BIG, F32)
    acc_scr[...] = jnp.zeros(acc_scr.shape, F32)

    def scores(c, t):
        off = pl.multiple_of(c * KEY_BLOCK, KEY_BLOCK)
        return jnp.dot(k_ref[pl.ds(off, KEY_BLOCK), :], qp_scr[t], preferred_element_type=F32)

    assert n_tasks % S_SLOTS == 0 and QK_LOOKAHEAD < S_SLOTS
    for t in range(QK_LOOKAHEAD):
        s_scr[t] = scores(0, t)

    def body(c, _):
        c_next = jnp.minimum(c + 1, n_blocks - 1)
        for t in range(n_tasks):
            s = s_scr[t % S_SLOTS]
            ahead = t + QK_LOOKAHEAD
            if ahead < n_tasks:
                s_scr[ahead % S_SLOTS] = scores(c, ahead)
            else:
                s_scr[ahead % S_SLOTS] = scores(c_next, ahead - n_tasks)
            r0 = ((t % N_Q_HEADS) // KV_GROUP) * V_ROWS
            m = m_scr[t]
            m_new = jnp.maximum(m, jnp.max(s, axis=0, keepdims=True))
            p = jnp.exp2(s - m_new).astype(BF16)
            alpha = jnp.exp2(m - m_new)
            pv = jnp.dot(vT_ref[c, r0:r0 + V_ROWS, :], p, preferred_element_type=F32)
            acc_scr[t] = alpha * acc_scr[t] + pv
            m_scr[t] = m_new
        return 0

    lax.fori_loop(0, n_blocks, body, 0)
    for g in range(n_qt):
        acc = acc_scr[g * N_Q_HEADS:(g + 1) * N_Q_HEADS]
        out = acc[:, :HEAD_DIM] / acc[:, HEAD_DIM:HEAD_DIM + 1]
        o_ref[g * tq:(g + 1) * tq, :] = out.reshape(ATTN_WIDTH, tq).T.astype(BF16)


def _attn_call(qT, k, vT, n_batch, seq, q_tiles, cache=None):
    nq = seq // (Q_TILE * q_tiles)
    nkb = seq // KEY_BLOCK
    n_tasks = q_tiles * N_Q_HEADS
    v_rows = N_KV_HEADS * V_ROWS
    in_specs = [pl.BlockSpec((q_tiles, ATTN_WIDTH, Q_TILE), lambda b, i: (b * nq + i, 0, 0)),
                pl.BlockSpec((seq, KV_WIDTH), lambda b, i: (b, 0)),
                pl.BlockSpec((nkb, v_rows, KEY_BLOCK), lambda b, i: (b, 0, 0))]
    scratch = [pltpu.VMEM((n_tasks, KV_WIDTH, Q_TILE), BF16),
               pltpu.VMEM((n_tasks, 1, Q_TILE), F32),
               pltpu.VMEM((n_tasks, V_ROWS, Q_TILE), F32),
               pltpu.VMEM((S_SLOTS, KEY_BLOCK, Q_TILE), F32)]
    args = [qT, k, vT]
    if cache is not None:
        ck, cvT = cache
        past = ck.shape[1]
        in_specs += [pl.BlockSpec((1, past, KV_WIDTH), lambda b, i: (b, 0, 0)),
                     pl.BlockSpec((1, past // KEY_BLOCK, v_rows, KEY_BLOCK), lambda b, i: (b, 0, 0, 0))]
        scratch += [pltpu.VMEM((seq + past, KV_WIDTH), BF16),
                    pltpu.VMEM((nkb + past // KEY_BLOCK, v_rows, KEY_BLOCK), BF16)]
        args += [ck, cvT]
        nkb += past // KEY_BLOCK
    return pl.pallas_call(
        functools.partial(_attn_kernel, n_blocks=nkb, has_cache=cache is not None),
        out_shape=jax.ShapeDtypeStruct((n_batch * seq, ATTN_WIDTH), BF16),
        grid=(n_batch, nq),
        in_specs=in_specs,
        out_specs=pl.BlockSpec((q_tiles * Q_TILE, ATTN_WIDTH), lambda b, i: (b * nq + i, 0)),
        scratch_shapes=scratch,
        compiler_params=pltpu.CompilerParams(dimension_semantics=("parallel", "parallel"),
                                             vmem_limit_bytes=VMEM_LIMIT),
        name="gqa_attention",
    )(*args)


def _log_sigmoid(x):
    return jnp.minimum(x, 0.0) - jnp.log1p(jnp.exp(-jnp.abs(x)))


def _cumsum_lanes(x, tri):
    hi = x.astype(BF16)
    r1 = x - hi.astype(F32)
    mid = r1.astype(BF16)
    lo = (r1 - mid.astype(F32)).astype(BF16)
    return (jnp.dot(hi, tri, preferred_element_type=F32) + jnp.dot(mid, tri, preferred_element_type=F32)
            + jnp.dot(lo, tri, preferred_element_type=F32))


def _tri_masks():
    ri = lax.broadcasted_iota(jnp.int32, (CHUNK, CHUNK), 0)
    ci = lax.broadcasted_iota(jnp.int32, (CHUNK, CHUNK), 1)
    return ri <= ci, ri >= ci


def _gate_rows(gs, d, le, ge):
    nh = MLSTM_HEADS
    lf = _log_sigmoid(jnp.concatenate([g[8 * d + nh:8 * d + 2 * nh] for g in gs], axis=0))
    tri = (le if d == 0 else ge).astype(BF16)
    b = _cumsum_lanes(lf, tri)
    return [(lf[nh * j:nh * (j + 1)], b[nh * j:nh * (j + 1)], g[8 * d:8 * d + nh]) for j, g in enumerate(gs)]


def _scan_kernel(*refs, has_init):
    kf_ref, vf_ref, gf_ref, kb_ref, vb_ref, gb_ref = refs[:6]
    cxf_ref, cxb_ref, mf_ref, mb_ref, cfin_ref, mfin_ref = refs[-6:]

    @pl.when(pl.program_id(1) == 0)
    def _():
        if has_init:
            c0_ref, m0_ref = refs[6:8]
            cfin_ref[...] = c0_ref[...]
            mfin_ref[...] = m0_ref[...]
        else:
            cfin_ref[...] = jnp.zeros(cfin_ref.shape, F32)
            mfin_ref[...] = jnp.zeros(mfin_ref.shape, F32)

    n_ch = gf_ref.shape[0]
    le, ge = _tri_masks()
    dirs = ((kf_ref, vf_ref, gf_ref, cxf_ref, mf_ref), (kb_ref, vb_ref, gb_ref, cxb_ref, mb_ref))
    gate = []
    for d, (_, _, g_ref, _, _) in enumerate(dirs):
        per_chunk = []
        for lf, b, ig in _gate_rows([g_ref[j] for j in range(n_ch)], d, le, ge):
            b_last = jnp.sum(lf, axis=1, keepdims=True)
            gt = b_last - b + ig
            a = jnp.max(gt, axis=1, keepdims=True)
            per_chunk.append((b_last, a, jnp.exp(gt - a)))
        gate.append(per_chunk)
    prods = []
    for d, (k_ref, v_ref, _, _, _) in enumerate(dirs):
        per_chunk = []
        for j in range(n_ch):
            wt = gate[d][j][2]
            us = []
            for h in range(MLSTM_HEADS):
                vw = (v_ref[j, h * MV_ROWS:(h + 1) * MV_ROWS, :].astype(F32) * wt[h:h + 1, :]).astype(BF16)
                kh = k_ref[j * CHUNK:(j + 1) * CHUNK, h * MLSTM_DK:(h + 1) * MLSTM_DK]
                us.append(jnp.dot(vw, kh, preferred_element_type=F32))
            per_chunk.append(us)
        prods.append(per_chunk)
    for d, (_, _, _, cx_ref, mo_ref) in enumerate(dirs):
        local = [gate[d][j][:2] + (prods[d][j],) for j in range(n_ch)]
        m_run = mfin_ref[0, d][:, 0:1]
        state = [cfin_ref[0, d, h] for h in range(MLSTM_HEADS)]
        for j in (range(n_ch) if d == 0 else reversed(range(n_ch))):
            b_last, a, us = local[j]
            m_new = jnp.maximum(a, b_last + m_run)
            decay = jnp.exp(b_last + m_run - m_new)
            gam = jnp.exp(a - m_new)
            mo_ref[0, j] = jnp.broadcast_to(m_run, (MLSTM_HEADS, CHUNK))
            for h in range(MLSTM_HEADS):
                cx_ref[0, j, h] = state[h].astype(BF16)
                state[h] = decay[h:h + 1, :] * state[h] + gam[h:h + 1, :] * us[h]
            m_run = m_new
        mfin_ref[0, d] = jnp.broadcast_to(m_run, (MLSTM_HEADS, CHUNK))
        for h in range(MLSTM_HEADS):
            cfin_ref[0, d, h] = state[h]


def _scan_call(mk, mvT, gates, c0, m0, n_batch, seq):
    nc = seq // CHUNK
    g = _per_step(nc, SCAN_CHUNKS)
    ns = nc // g
    fwd3 = lambda b, c: (b * ns + c, 0, 0)
    bwd3 = lambda b, c: (b * ns + ns - 1 - c, 0, 0)
    fwd2 = lambda b, c: (b * ns + c, 0)
    bwd2 = lambda b, c: (b * ns + ns - 1 - c, 0)
    k_blk, v_blk, g_blk = (g * CHUNK, MLSTM_WIDTH), (g, MLSTM_HEADS * MV_ROWS, CHUNK), (g, N_GATES, CHUNK)
    cx_blk = (1, g, MLSTM_HEADS, MV_ROWS, MLSTM_DK)
    m_blk = (1, g, MLSTM_HEADS, CHUNK)
    st_blk = (1, 2, MLSTM_HEADS, MV_ROWS, MLSTM_DK)
    sm_blk = (1, 2, MLSTM_HEADS, CHUNK)
    in_specs = [pl.BlockSpec(k_blk, fwd2), pl.BlockSpec(v_blk, fwd3), pl.BlockSpec(g_blk, fwd3),
                pl.BlockSpec(k_blk, bwd2), pl.BlockSpec(v_blk, bwd3), pl.BlockSpec(g_blk, bwd3)]
    args = [mk, mvT, gates, mk, mvT, gates]
    if c0 is not None:
        in_specs += [pl.BlockSpec(st_blk, lambda b, c: (b, 0, 0, 0, 0)),
                     pl.BlockSpec(sm_blk, lambda b, c: (b, 0, 0, 0))]
        args += [c0, m0]
    return pl.pallas_call(
        functools.partial(_scan_kernel, has_init=c0 is not None),
        out_shape=[jax.ShapeDtypeStruct((n_batch, nc) + cx_blk[2:], BF16),
                   jax.ShapeDtypeStruct((n_batch, nc) + cx_blk[2:], BF16),
                   jax.ShapeDtypeStruct((n_batch, nc) + m_blk[2:], F32),
                   jax.ShapeDtypeStruct((n_batch, nc) + m_blk[2:], F32),
                   jax.ShapeDtypeStruct((n_batch,) + st_blk[1:], F32),
                   jax.ShapeDtypeStruct((n_batch,) + sm_blk[1:], F32)],
        grid=(n_batch, ns),
        in_specs=in_specs,
        out_specs=[pl.BlockSpec(cx_blk, lambda b, c: (b, c, 0, 0, 0)),
                   pl.BlockSpec(cx_blk, lambda b, c: (b, ns - 1 - c, 0, 0, 0)),
                   pl.BlockSpec(m_blk, lambda b, c: (b, c, 0, 0)),
                   pl.BlockSpec(m_blk, lambda b, c: (b, ns - 1 - c, 0, 0)),
                   pl.BlockSpec(st_blk, lambda b, c: (b, 0, 0, 0, 0)),
                   pl.BlockSpec(sm_blk, lambda b, c: (b, 0, 0, 0))],
        compiler_params=pltpu.CompilerParams(dimension_semantics=("parallel", "arbitrary"),
                                             vmem_limit_bytes=VMEM_LIMIT),
        name="mlstm_state_scan",
    )(*args)


def _mlstm_out_kernel(qT_ref, k_ref, vT_ref, g_ref, cxf_ref, cxb_ref, mf_ref, mb_ref, ogT_ref, nw_ref, o_ref):
    n_ch = g_ref.shape[0]
    nh = MLSTM_HEADS
    le, ge = _tri_masks()
    gs = [g_ref[j] for j in range(n_ch)]
    gate = [_gate_rows(gs, d, le, ge) for d in range(2)]
    kq = {}
    for j in range(n_ch):
        for h in range(nh):
            sl = slice(h * MLSTM_DK, (h + 1) * MLSTM_DK)
            kq[j, h] = jnp.dot(k_ref[j * CHUNK:(j + 1) * CHUNK, sl], qT_ref[j, sl, :],
                               preferred_element_type=F32)
    for j in range(n_ch):
        tok = slice(j * CHUNK, (j + 1) * CHUNK)
        r2, b2 = [], []
        for d in range(2):
            lf, b, ig = gate[d][j]
            r2.append((ig - b) * LOG2E)
            b2.append(b * LOG2E)
        r_cols = jnp.concatenate(r2, axis=0).T
        for h in range(nh):
            sl = slice(h * MLSTM_DK, (h + 1) * MLSTM_DK)
            qT = qT_ref[j, sl, :].astype(F32)
            vx = vT_ref[j, h * MV_ROWS:(h + 1) * MV_ROWS, :]
            sT = kq[j, h]
            h_sum = jnp.zeros((MLSTM_DK, CHUNK), F32)
            for d, (cx_ref, m_ref) in enumerate(((cxf_ref, mf_ref), (cxb_ref, mb_ref))):
                mask = le if d == 0 else ge
                i = d * nh + h
                rm = jnp.where(mask, jnp.broadcast_to(r_cols[:, i:i + 1], (CHUNK, CHUNK)), NEG_BIG)
                cm = jnp.max(rm, axis=0, keepdims=True)
                m_prev = m_ref[0, j, h:h + 1, :] * LOG2E
                mm = jnp.maximum(cm, m_prev)
                pT = (sT * jnp.exp2(rm - mm)).astype(BF16)
                qi = (qT * jnp.exp2(m_prev - mm)).astype(BF16)
                lhs = jnp.concatenate([vx, cx_ref[0, j, h]], axis=1)
                mix = jnp.dot(lhs, jnp.concatenate([pT, qi], axis=0), preferred_element_type=F32)
                den = mix[MLSTM_DK:MLSTM_DK + 1]
                clamp = jnp.exp2(-(b2[d][h:h + 1, :] + mm))
                h_sum = h_sum + mix[:MLSTM_DK] * (1.0 / jnp.maximum(jnp.abs(den), clamp))
            ms = jnp.mean(h_sum * h_sum, axis=0, keepdims=True)
            hn = h_sum * lax.rsqrt(ms + EPS) * nw_ref[sl, :] * ogT_ref[j, sl, :].astype(F32)
            o_ref[tok, sl] = hn.T.astype(BF16)


def _mlstm_out_call(mqT, mk, mvT, gates, cxf, cxb, mf, mb, ogT, nwb, n_batch, seq):
    g = _per_step(seq // CHUNK, OUT_CHUNKS)
    nc = seq // (CHUNK * g)
    tok = lambda b, c: (b * nc + c, 0)
    chk = lambda b, c: (b * nc + c, 0, 0)
    cx_blk = (1, g, MLSTM_HEADS, MV_ROWS, MLSTM_DK)
    m_blk = (1, g, MLSTM_HEADS, CHUNK)
    return pl.pallas_call(
        _mlstm_out_kernel,
        out_shape=jax.ShapeDtypeStruct((n_batch * seq, MLSTM_WIDTH), BF16),
        grid=(n_batch, nc),
        in_specs=[pl.BlockSpec((g, MLSTM_WIDTH, CHUNK), chk),
                  pl.BlockSpec((g * CHUNK, MLSTM_WIDTH), tok),
                  pl.BlockSpec((g, MLSTM_HEADS * MV_ROWS, CHUNK), chk),
                  pl.BlockSpec((g, N_GATES, CHUNK), chk),
                  pl.BlockSpec(cx_blk, lambda b, c: (b, c, 0, 0, 0)),
                  pl.BlockSpec(cx_blk, lambda b, c: (b, c, 0, 0, 0)),
                  pl.BlockSpec(m_blk, lambda b, c: (b, c, 0, 0)),
                  pl.BlockSpec(m_blk, lambda b, c: (b, c, 0, 0)),
                  pl.BlockSpec((g, MLSTM_WIDTH, CHUNK), chk),
                  pl.BlockSpec((MLSTM_WIDTH, CHUNK), lambda b, c: (0, 0))],
        out_specs=pl.BlockSpec((g * CHUNK, MLSTM_WIDTH), tok),
        compiler_params=pltpu.CompilerParams(dimension_semantics=("parallel", "parallel"),
                                             vmem_limit_bytes=VMEM_LIMIT),
        name="mlstm_chunk_out",
    )(mqT, mk, mvT, gates, cxf, cxb, mf, mb, ogT, nwb)


def _ffn_kernel(x_ref, a_ref, hm_ref, mods_ref, wo_ref, n2w_ref, wgu_ref, wd_ref, fw_ref, o_ref):
    g1 = mods_ref[0, 2:3, :]
    sh2 = mods_ref[0, 3:4, :]
    sc2 = mods_ref[0, 4:5, :]
    g2 = mods_ref[0, 5:6, :]
    subs = [slice(a, a + FFN_SUB) for a in range(0, x_ref.shape[0], FFN_SUB)]
    mix = [jnp.dot(a_ref[s, :], wo_ref[0:ATTN_WIDTH, :], preferred_element_type=F32)
           + jnp.dot(hm_ref[s, :], wo_ref[ATTN_WIDTH:D_MODEL, :], preferred_element_type=F32) for s in subs]
    x1 = [x_ref[s, :] + g1 * m for s, m in zip(subs, mix)]
    h2 = [(_rms(x) * n2w_ref[...] * (1.0 + sc2) + sh2).astype(BF16) for x in x1]
    acc = [jnp.zeros(x.shape, F32) for x in x1]
    for c0 in range(0, D_FF, FF_CHUNK):
        c1 = min(c0 + FF_CHUNK, D_FF)
        act = []
        for h in h2:
            gg = jnp.dot(h, wgu_ref[:, c0:c1], preferred_element_type=F32)
            uu = jnp.dot(h, wgu_ref[:, D_FF + c0:D_FF + c1], preferred_element_type=F32)
            act.append((gg * jax.nn.sigmoid(gg) * uu).astype(BF16))
        acc = [a + jnp.dot(t, wd_ref[c0:c1, :], preferred_element_type=F32) for a, t in zip(acc, act)]
    for s, x, a in zip(subs, x1, acc):
        o_ref[s, :] = _rms(x + g2 * a) * fw_ref[...]


def _ffn_call(x2d, attn_o, hm, mods, per_request, seq, wo, n2w, wgu, wd, fw):
    T = x2d.shape[0]
    tm = TOKEN_TILE
    mod_map = _mod_map(per_request, seq, tm)
    return pl.pallas_call(
        _ffn_kernel,
        out_shape=jax.ShapeDtypeStruct((T, D_MODEL), F32),
        grid=(T // tm,),
        in_specs=[pl.BlockSpec((tm, D_MODEL), lambda i: (i, 0)),
                  pl.BlockSpec((tm, ATTN_WIDTH), lambda i: (i, 0)),
                  pl.BlockSpec((tm, MLSTM_WIDTH), lambda i: (i, 0)),
                  pl.BlockSpec((1, 6, D_MODEL), mod_map),
                  _resident(wo.shape), _resident((1, D_MODEL)), _resident(wgu.shape),
                  _resident(wd.shape), _resident((1, D_MODEL))],
        out_specs=pl.BlockSpec((tm, D_MODEL), lambda i: (i, 0)),
        compiler_params=pltpu.CompilerParams(dimension_semantics=("parallel",),
                                             vmem_limit_bytes=VMEM_LIMIT),
        name="out_proj_ffn",
    )(x2d, attn_o, hm, mods, wo, n2w, wgu, wd, fw)


def _rope_tables(seq, gain, scale):
    pos = jnp.arange(seq, dtype=jnp.int32)
    row_ids = (pos // GRID_W).astype(F32)
    col_ids = (pos % GRID_W).astype(F32)
    inv = ROPE_THETA ** (-jnp.arange(0, AXIS_DIM, 2, dtype=F32) / AXIS_DIM)
    ang_r = inv[:, None] * row_ids[None, :]
    ang_c = inv[:, None] * col_ids[None, :]
    cos = jnp.concatenate([jnp.cos(ang_r)] * 2 + [jnp.cos(ang_c)] * 2, axis=0)
    sin = jnp.concatenate([-jnp.sin(ang_r), jnp.sin(ang_r), -jnp.sin(ang_c), jnp.sin(ang_c)], axis=0)
    g = gain.astype(F32) * scale
    g_sw = jnp.concatenate([g[16:32], g[0:16], g[48:64], g[32:48]])
    return g[:, None] * cos, g_sw[:, None] * sin


def _flat_tables(gain, scale, width):
    g = gain.astype(F32) * scale
    return jnp.broadcast_to(g[:, None], (HEAD_DIM, width)), jnp.zeros((HEAD_DIM, width), F32)


def _group(x, mods, per_request, seq, weights, tables, cache, c0, m0, emit_cache):
    n_batch = x.shape[0]
    x2d = x.reshape(n_batch * seq, D_MODEL)
    outs = _proj_call(x2d, mods, per_request, seq, weights["n1w"], weights["wt"], weights["wn"],
                      weights["gbias"], tables, emit_cache)
    qT, k, vT, mqT, mk, mvT, ogT, gates = outs[:8]
    q_tiles = _per_step(seq // Q_TILE, Q_TILES_PER_STEP)
    attn_o = _attn_call(qT, k, vT, n_batch, seq, q_tiles, cache)
    cxf, cxb, mf, mb, cfin, mfin = _scan_call(mk, mvT, gates, c0, m0, n_batch, seq)
    hm = _mlstm_out_call(mqT, mk, mvT, gates, cxf, cxb, mf, mb, ogT, weights["mnw"], n_batch, seq)
    y = _ffn_call(x2d, attn_o, hm, mods, per_request, seq, weights["wo"], weights["n2w"], weights["wgu"],
                  weights["wd"], weights["fw"])
    return y.reshape(n_batch, seq, D_MODEL), outs[8:], cfin, mfin


def kernel(x_prompt, x_sample, cache_k, cache_v, state_C, state_n, state_m, c, c_ctx, w_ada, b_ada,
           norm1_w, w_in, gate_bias, q_norm_w, k_norm_w, mlstm_norm_w, w_out, norm2_w, w_gu, w_down,
           final_norm_w):
    depth = w_ada.shape[0]
    assert depth == 1, "single-layer trunk"
    n_ctx, ctx_len, _ = x_prompt.shape
    n_dec, dec_len, _ = x_sample.shape
    l = 0

    cond = jnp.zeros((16, D_MODEL), F32).at[:n_dec].set(c).at[n_dec].set(c_ctx)
    mods = _mods_call(cond, w_ada[l], b_ada[l]).reshape(16, 6, D_MODEL)

    wi = w_in[l]
    aq, ak, av, mq_w, mk_w, mv_w, mo_w, mg_w = jnp.split(
        wi, [512, 640, 768, 1280, 1792, 2304, 2816], axis=1)
    weights = {
        "wt": jnp.concatenate([aq, ak, av, mq_w, mv_w, mo_w, mg_w], axis=1).T.astype(BF16),
        "wn": mk_w.astype(BF16),
        "gbias": gate_bias[l].reshape(N_GATES, 1).astype(F32),
        "n1w": norm1_w[l].reshape(1, D_MODEL),
        "n2w": norm2_w[l].reshape(1, D_MODEL),
        "mnw": jnp.broadcast_to(mlstm_norm_w[l].astype(F32)[:, None], (MLSTM_WIDTH, CHUNK)),
        "wo": w_out[l].astype(BF16),
        "wgu": w_gu[l].astype(BF16),
        "wd": w_down[l].astype(BF16),
        "fw": final_norm_w.reshape(1, D_MODEL),
    }
    q_scale = (HEAD_DIM ** -0.5) * LOG2E

    tabs_ctx = _flat_tables(q_norm_w[l], q_scale, PROJ_TILE) + _flat_tables(k_norm_w[l], 1.0, PROJ_TILE)
    y_prompt, (kc, vc), cfin, mfin = _group(x_prompt, mods[n_dec:n_dec + 1], False, ctx_len, weights,
                                            tabs_ctx, None, None, None, True)
    new_cache_k = kc.reshape(n_ctx, 1, ctx_len, N_KV_HEADS, HEAD_DIM)
    new_cache_v = vc.reshape(n_ctx, 1, ctx_len, N_KV_HEADS, HEAD_DIM)
    new_state_C = jnp.swapaxes(cfin[..., :MLSTM_DK, :], -1, -2)[:, None]
    new_state_n = cfin[..., MLSTM_DK, :][:, None]
    new_state_m = mfin[..., 0][:, None]

    tabs_dec = _rope_tables(dec_len, q_norm_w[l], q_scale) + _rope_tables(dec_len, k_norm_w[l], 1.0)
    past = cache_k.shape[2]
    ck = cache_k[:, l].reshape(n_dec, past, KV_WIDTH).astype(BF16)
    cvT = cache_v[:, l].reshape(n_dec, past // KEY_BLOCK, KEY_BLOCK, KV_WIDTH).transpose(0, 1, 3, 2)
    pad_ones = jnp.ones(cvT.shape[:2] + (V_ROWS - HEAD_DIM, KEY_BLOCK), cvT.dtype)
    cvT = jnp.concatenate([cvT[:, :, :HEAD_DIM], pad_ones, cvT[:, :, HEAD_DIM:], pad_ones], axis=2).astype(BF16)
    sC = state_C[:, l].astype(F32)
    sn = state_n[:, l].astype(F32)
    n_rows = jnp.broadcast_to(sn[..., None, :], sn.shape[:-1] + (MV_ROWS - MLSTM_DK, MLSTM_DK))
    c0 = jnp.concatenate([jnp.swapaxes(sC, -1, -2), n_rows], axis=-2)
    m0 = jnp.broadcast_to(state_m[:, l].astype(F32)[..., None], (n_dec, 2, MLSTM_HEADS, CHUNK))
    y_sample, _, _, _ = _group(x_sample, mods[:n_dec], True, dec_len, weights,
                               tabs_dec, (ck, cvT), c0, m0, False)

    return (y_prompt, y_sample, new_cache_k, new_cache_v, new_state_C, new_state_n, new_state_m)
```

```python
import functools

import jax
import jax.numpy as jnp
from jax import lax
from jax.experimental import pallas as pl
from jax.experimental.pallas import tpu as pltpu

F32 = jnp.float32
BF16 = jnp.bfloat16

D_MODEL = 1024
HEAD_DIM = 64
N_Q_HEADS = 8
N_KV_HEADS = 2
KV_GROUP = N_Q_HEADS // N_KV_HEADS
ATTN_WIDTH = N_Q_HEADS * HEAD_DIM
KV_WIDTH = N_KV_HEADS * HEAD_DIM
AXIS_DIM = HEAD_DIM // 2
GRID_W = 64
ROPE_THETA = 10000.0
MLSTM_HEADS = 4
MLSTM_DK = 128
MLSTM_WIDTH = MLSTM_HEADS * MLSTM_DK
CHUNK = 128
D_FF = 2816
N_GATES = 4 * MLSTM_HEADS
EPS = 1e-6
LOG2E = 1.4426950408889634
NEG_BIG = -1e30

TOKEN_TILE = 512
FFN_SUB = 256
PROJ_TILE = 1024
PROJ_SUB = 256
Q_TILE = 256
KEY_BLOCK = 256
QK_LOOKAHEAD = 6
S_SLOTS = 8
Q_TILES_PER_STEP = 16
V_ROWS = 80
SCAN_CHUNKS = 16
OUT_CHUNKS = 4
FF_CHUNK = 1024
ADA_TILE = 1536
VMEM_LIMIT = 56 * 1024 * 1024

_R_Q, _R_K, _R_V, _R_MQ, _R_MV, _R_MO, _R_G, _R_END = 0, 512, 640, 768, 1280, 1792, 2304, 2320
MV_ROWS = MLSTM_DK + 16


def _per_step(n, cap):
    g = cap
    while n % g:
        g //= 2
    return g


def _resident(shape):
    nd = len(shape)
    return pl.BlockSpec(shape, lambda *_: (0,) * nd, pipeline_mode=pl.Buffered(1))


def _mods_kernel(cond_ref, w_ref, b_ref, o_ref):
    c = cond_ref[...]
    s = c * jax.nn.sigmoid(c)
    o_ref[...] = jnp.dot(s, w_ref[...], preferred_element_type=F32) + b_ref[...]


def _mods_call(cond, w_ada, b_ada):
    rows, n_out = cond.shape[0], w_ada.shape[1]
    return pl.pallas_call(
        _mods_kernel,
        out_shape=jax.ShapeDtypeStruct((rows, n_out), F32),
        grid=(n_out // ADA_TILE,),
        in_specs=[pl.BlockSpec((rows, D_MODEL), lambda j: (0, 0)),
                  pl.BlockSpec((D_MODEL, ADA_TILE), lambda j: (0, j)),
                  pl.BlockSpec((1, ADA_TILE), lambda j: (0, j))],
        out_specs=pl.BlockSpec((rows, ADA_TILE), lambda j: (0, j)),
        compiler_params=pltpu.CompilerParams(dimension_semantics=("parallel",),
                                             vmem_limit_bytes=VMEM_LIMIT),
        name="adaln_mods",
    )(cond, w_ada, b_ada.reshape(1, n_out))


def _rms(x):
    return x * lax.rsqrt(jnp.mean(x * x, axis=-1, keepdims=True) + EPS)


def _proj_kernel(x_ref, mods_ref, n1w_ref, wt_ref, wn_ref, gb_ref, tqa_ref, tqb_ref, tka_ref, tkb_ref,
                 qT_ref, k_ref, vT_ref, mqT_ref, mk_ref, mvT_ref, ogT_ref, g_ref, *cache_refs):
    sh = mods_ref[0, 0:1, :]
    sc = mods_ref[0, 1:2, :]
    sub = PROJ_SUB
    for a in range(0, x_ref.shape[0], sub):
        tok = slice(a, a + sub)
        h = (_rms(x_ref[tok, :]) * n1w_ref[...] * (1.0 + sc) + sh).astype(BF16)

        yt = lax.dot_general(wt_ref[...], h, (((1,), (1,)), ((), ())), preferred_element_type=F32)

        n_qk = N_Q_HEADS + N_KV_HEADS
        qk = yt[_R_Q:_R_V].reshape(n_qk, HEAD_DIM, sub)
        qk = qk * lax.rsqrt(jnp.sum(qk * qk, axis=1, keepdims=True) * (1.0 / HEAD_DIM) + EPS)
        sw = jnp.concatenate([qk[:, 16:32], qk[:, 0:16], qk[:, 48:64], qk[:, 32:48]], axis=1)
        q = qk[:N_Q_HEADS] * tqa_ref[:, tok][None] + sw[:N_Q_HEADS] * tqb_ref[:, tok][None]
        q = q.reshape(ATTN_WIDTH, sub).astype(BF16)
        for j in range(sub // Q_TILE):
            qT_ref[a // Q_TILE + j] = q[:, j * Q_TILE:(j + 1) * Q_TILE]
        k = qk[N_Q_HEADS:] * tka_ref[:, tok][None] + sw[N_Q_HEADS:] * tkb_ref[:, tok][None]
        kt = k.reshape(KV_WIDTH, sub).T
        k_ref[tok, :] = kt.astype(BF16)

        vt = yt[_R_V:_R_MQ]
        ones = jnp.ones((V_ROWS - HEAD_DIM, sub), F32)
        vtb = jnp.concatenate([vt[:HEAD_DIM], ones, vt[HEAD_DIM:], ones], axis=0).astype(BF16)
        for j in range(sub // KEY_BLOCK):
            vT_ref[a // KEY_BLOCK + j] = vtb[:, j * KEY_BLOCK:(j + 1) * KEY_BLOCK]
        if cache_refs:
            kc_ref, vc_ref = cache_refs
            kc_ref[tok, :] = kt
            vc_ref[tok, :] = vt.T

        mq = yt[_R_MQ:_R_MV].astype(BF16)
        mv = yt[_R_MV:_R_MO]
        pad = jnp.ones((MV_ROWS - MLSTM_DK, sub), F32)
        mvx = jnp.concatenate(
            [blk for hh in range(MLSTM_HEADS) for blk in (mv[hh * MLSTM_DK:(hh + 1) * MLSTM_DK], pad)],
            axis=0).astype(BF16)
        og = jax.nn.sigmoid(yt[_R_MO:_R_G]).astype(BF16)
        gates = yt[_R_G:_R_END] + gb_ref[...]
        for j in range(sub // CHUNK):
            cs = slice(j * CHUNK, (j + 1) * CHUNK)
            mqT_ref[a // CHUNK + j] = mq[:, cs]
            mvT_ref[a // CHUNK + j] = mvx[:, cs]
            ogT_ref[a // CHUNK + j] = og[:, cs]
            g_ref[a // CHUNK + j] = gates[:, cs]

        mk = jnp.dot(h, wn_ref[...], preferred_element_type=F32)
        mk_ref[tok, :] = (mk * (MLSTM_DK ** -0.5)).astype(BF16)


def _mod_map(per_request, seq, tm):
    if not per_request:
        return lambda i: (0, 0, 0)
    tiles = seq // tm
    return lambda i: (i // tiles, 0, 0)


def _proj_call(x2d, mods, per_request, seq, n1w, wt, wn, gbias, tables, emit_cache):
    T = x2d.shape[0]
    tm = PROJ_TILE
    n_tiles = T // tm
    mod_map = _mod_map(per_request, seq, tm)
    if per_request:
        tiles = seq // tm
        tab_map = lambda i: (0, i % tiles)
    else:
        tab_map = lambda i: (0, 0)
    tab_spec = pl.BlockSpec((HEAD_DIM, tm), tab_map)
    out_shape = [
        jax.ShapeDtypeStruct((T // Q_TILE, ATTN_WIDTH, Q_TILE), BF16),
        jax.ShapeDtypeStruct((T, KV_WIDTH), BF16),
        jax.ShapeDtypeStruct((T // KEY_BLOCK, N_KV_HEADS * V_ROWS, KEY_BLOCK), BF16),
        jax.ShapeDtypeStruct((T // CHUNK, MLSTM_WIDTH, CHUNK), BF16),
        jax.ShapeDtypeStruct((T, MLSTM_WIDTH), BF16),
        jax.ShapeDtypeStruct((T // CHUNK, MLSTM_HEADS * MV_ROWS, CHUNK), BF16),
        jax.ShapeDtypeStruct((T // CHUNK, MLSTM_WIDTH, CHUNK), BF16),
        jax.ShapeDtypeStruct((T // CHUNK, N_GATES, CHUNK), F32),
    ]
    out_specs = [
        pl.BlockSpec((tm // Q_TILE, ATTN_WIDTH, Q_TILE), lambda i: (i, 0, 0)),
        pl.BlockSpec((tm, KV_WIDTH), lambda i: (i, 0)),
        pl.BlockSpec((tm // KEY_BLOCK, N_KV_HEADS * V_ROWS, KEY_BLOCK), lambda i: (i, 0, 0)),
        pl.BlockSpec((tm // CHUNK, MLSTM_WIDTH, CHUNK), lambda i: (i, 0, 0)),
        pl.BlockSpec((tm, MLSTM_WIDTH), lambda i: (i, 0)),
        pl.BlockSpec((tm // CHUNK, MLSTM_HEADS * MV_ROWS, CHUNK), lambda i: (i, 0, 0)),
        pl.BlockSpec((tm // CHUNK, MLSTM_WIDTH, CHUNK), lambda i: (i, 0, 0)),
        pl.BlockSpec((tm // CHUNK, N_GATES, CHUNK), lambda i: (i, 0, 0)),
    ]
    if emit_cache:
        out_shape += [jax.ShapeDtypeStruct((T, KV_WIDTH), F32)] * 2
        out_specs += [pl.BlockSpec((tm, KV_WIDTH), lambda i: (i, 0))] * 2
    return pl.pallas_call(
        _proj_kernel,
        out_shape=out_shape,
        grid=(n_tiles,),
        in_specs=[pl.BlockSpec((tm, D_MODEL), lambda i: (i, 0)),
                  pl.BlockSpec((1, 6, D_MODEL), mod_map),
                  _resident((1, D_MODEL)),
                  _resident(wt.shape),
                  _resident(wn.shape),
                  _resident((N_GATES, 1)),
                  tab_spec, tab_spec, tab_spec, tab_spec],
        out_specs=out_specs,
        compiler_params=pltpu.CompilerParams(dimension_semantics=("parallel",),
                                             vmem_limit_bytes=VMEM_LIMIT),
        name="norm1_in_proj",
    )(x2d, mods, n1w, wt, wn, gbias, *tables)


def _attn_kernel(*refs, n_blocks, has_cache):
    if has_cache:
        qT_ref, k_new_ref, vT_new_ref, ck_ref, cvT_ref, o_ref, qp_scr, m_scr, acc_scr, s_scr, k_ref, vT_ref = refs
        n_new = k_new_ref.shape[0]
        k_ref[0:n_new] = k_new_ref[...]
        k_ref[n_new:] = ck_ref[0]
        vT_ref[0:n_new // KEY_BLOCK] = vT_new_ref[...]
        vT_ref[n_new // KEY_BLOCK:] = cvT_ref[0]
    else:
        qT_ref, k_ref, vT_ref, o_ref, qp_scr, m_scr, acc_scr, s_scr = refs
    n_qt, _, tq = qT_ref.shape
    n_tasks = n_qt * N_Q_HEADS

    for t in range(n_tasks):
        g, hd = divmod(t, N_Q_HEADS)
        qg = qT_ref[g, hd * HEAD_DIM:(hd + 1) * HEAD_DIM, :]
        zero = jnp.zeros_like(qg)
        qp_scr[t] = jnp.concatenate([qg, zero] if hd < KV_GROUP else [zero, qg], axis=0)
    m_scr[...] = jnp.full(m_scr.shape, NEG_BIG, F32)
    acc_scr[...] = jnp.zeros(acc_scr.shape, F32)

    def scores(c, t):
        off = pl.multiple_of(c * KEY_BLOCK, KEY_BLOCK)
        return jnp.dot(k_ref[pl.ds(off, KEY_BLOCK), :], qp_scr[t], preferred_element_type=F32)

    assert n_tasks % S_SLOTS == 0 and QK_LOOKAHEAD < S_SLOTS
    for t in range(QK_LOOKAHEAD):
        s_scr[t] = scores(0, t)

    def body(c, _):
        c_next = jnp.minimum(c + 1, n_blocks - 1)
        for t in range(n_tasks):
            s = s_scr[t % S_SLOTS]
            ahead = t + QK_LOOKAHEAD
            if ahead < n_tasks:
                s_scr[ahead % S_SLOTS] = scores(c, ahead)
            else:
                s_scr[ahead % S_SLOTS] = scores(c_next, ahead - n_tasks)
            r0 = ((t % N_Q_HEADS) // KV_GROUP) * V_ROWS
            m = m_scr[t]
            m_new = jnp.maximum(m, jnp.max(s, axis=0, keepdims=True))
            p = jnp.exp2(s - m_new).astype(BF16)
            alpha = jnp.exp2(m - m_new)
            pv = jnp.dot(vT_ref[c, r0:r0 + V_ROWS, :], p, preferred_element_type=F32)
            acc_scr[t] = alpha * acc_scr[t] + pv
            m_scr[t] = m_new
        return 0

    lax.fori_loop(0, n_blocks, body, 0)
    for g in range(n_qt):
        acc = acc_scr[g * N_Q_HEADS:(g + 1) * N_Q_HEADS]
        out = acc[:, :HEAD_DIM] / acc[:, HEAD_DIM:HEAD_DIM + 1]
        o_ref[g * tq:(g + 1) * tq, :] = out.reshape(ATTN_WIDTH, tq).T.astype(BF16)


def _attn_call(qT, k, vT, n_batch, seq, q_tiles, cache=None):
    nq = seq // (Q_TILE * q_tiles)
    nkb = seq // KEY_BLOCK
    n_tasks = q_tiles * N_Q_HEADS
    v_rows = N_KV_HEADS * V_ROWS
    in_specs = [pl.BlockSpec((q_tiles, ATTN_WIDTH, Q_TILE), lambda b, i: (b * nq + i, 0, 0)),
                pl.BlockSpec((seq, KV_WIDTH), lambda b, i: (b, 0)),
                pl.BlockSpec((nkb, v_rows, KEY_BLOCK), lambda b, i: (b, 0, 0))]
    scratch = [pltpu.VMEM((n_tasks, KV_WIDTH, Q_TILE), BF16),
               pltpu.VMEM((n_tasks, 1, Q_TILE), F32),
               pltpu.VMEM((n_tasks, V_ROWS, Q_TILE), F32),
               pltpu.VMEM((S_SLOTS, KEY_BLOCK, Q_TILE), F32)]
    args = [qT, k, vT]
    if cache is not None:
        ck, cvT = cache
        past = ck.shape[1]
        in_specs += [pl.BlockSpec((1, past, KV_WIDTH), lambda b, i: (b, 0, 0)),
                     pl.BlockSpec((1, past // KEY_BLOCK, v_rows, KEY_BLOCK), lambda b, i: (b, 0, 0, 0))]
        scratch += [pltpu.VMEM((seq + past, KV_WIDTH), BF16),
                    pltpu.VMEM((nkb + past // KEY_BLOCK, v_rows, KEY_BLOCK), BF16)]
        args += [ck, cvT]
        nkb += past // KEY_BLOCK
    return pl.pallas_call(
        functools.partial(_attn_kernel, n_blocks=nkb, has_cache=cache is not None),
        out_shape=jax.ShapeDtypeStruct((n_batch * seq, ATTN_WIDTH), BF16),
        grid=(n_batch, nq),
        in_specs=in_specs,
        out_specs=pl.BlockSpec((q_tiles * Q_TILE, ATTN_WIDTH), lambda b, i: (b * nq + i, 0)),
        scratch_shapes=scratch,
        compiler_params=pltpu.CompilerParams(dimension_semantics=("parallel", "parallel"),
                                             vmem_limit_bytes=VMEM_LIMIT),
        name="gqa_attention",
    )(*args)


def _log_sigmoid(x):
    return jnp.minimum(x, 0.0) - jnp.log1p(jnp.exp(-jnp.abs(x)))


def _cumsum_lanes(x, tri):
    hi = x.astype(BF16)
    r1 = x - hi.astype(F32)
    mid = r1.astype(BF16)
    lo = (r1 - mid.astype(F32)).astype(BF16)
    return (jnp.dot(hi, tri, preferred_element_type=F32) + jnp.dot(mid, tri, preferred_element_type=F32)
            + jnp.dot(lo, tri, preferred_element_type=F32))


def _tri_masks():
    ri = lax.broadcasted_iota(jnp.int32, (CHUNK, CHUNK), 0)
    ci = lax.broadcasted_iota(jnp.int32, (CHUNK, CHUNK), 1)
    return ri <= ci, ri >= ci


def _gate_rows(gs, d, le, ge):
    nh = MLSTM_HEADS
    lf = _log_sigmoid(jnp.concatenate([g[8 * d + nh:8 * d + 2 * nh] for g in gs], axis=0))
    tri = (le if d == 0 else ge).astype(BF16)
    b = _cumsum_lanes(lf, tri)
    return [(lf[nh * j:nh * (j + 1)], b[nh * j:nh * (j + 1)], g[8 * d:8 * d + nh]) for j, g in enumerate(gs)]


def _scan_kernel(*refs, has_init, emit_state):
    kf_ref, vf_ref, gf_ref, kb_ref, vb_ref, gb_ref = refs[:6]
    cfin_ref = refs[-1]
    outs = refs[9 if has_init else 6:-1]
    cxf_ref, cxb_ref, mf_ref, mb_ref, mfin_ref = outs[:5]

    @pl.when(pl.program_id(1) == 0)
    def _():
        if has_init:
            sc_ref, sn_ref, m0_ref = refs[6:9]
            for d in range(2):
                for h in range(MLSTM_HEADS):
                    cfin_ref[d, h, 0:MLSTM_DK, :] = sc_ref[0, 0, d, h].T
                    cfin_ref[d, h, MLSTM_DK:MV_ROWS, :] = jnp.broadcast_to(
                        sn_ref[0, 0, d, h:h + 1, :], (MV_ROWS - MLSTM_DK, MLSTM_DK))
            mfin_ref[...] = m0_ref[...]
        else:
            cfin_ref[...] = jnp.zeros(cfin_ref.shape, F32)
            mfin_ref[...] = jnp.zeros(mfin_ref.shape, F32)

    n_ch = gf_ref.shape[0]
    le, ge = _tri_masks()
    dirs = ((kf_ref, vf_ref, gf_ref, cxf_ref, mf_ref), (kb_ref, vb_ref, gb_ref, cxb_ref, mb_ref))
    gate = []
    for d, (_, _, g_ref, _, _) in enumerate(dirs):
        per_chunk = []
        for lf, b, ig in _gate_rows([g_ref[j] for j in range(n_ch)], d, le, ge):
            b_last = jnp.sum(lf, axis=1, keepdims=True)
            gt = b_last - b + ig
            a = jnp.max(gt, axis=1, keepdims=True)
            per_chunk.append((b_last, a, jnp.exp(gt - a)))
        gate.append(per_chunk)
    prods = []
    for d, (k_ref, v_ref, _, _, _) in enumerate(dirs):
        per_chunk = []
        for j in range(n_ch):
            wt = gate[d][j][2]
            us = []
            for h in range(MLSTM_HEADS):
                vw = (v_ref[j, h * MV_ROWS:(h + 1) * MV_ROWS, :].astype(F32) * wt[h:h + 1, :]).astype(BF16)
                kh = k_ref[j * CHUNK:(j + 1) * CHUNK, h * MLSTM_DK:(h + 1) * MLSTM_DK]
                us.append(jnp.dot(vw, kh, preferred_element_type=F32))
            per_chunk.append(us)
        prods.append(per_chunk)
    for d, (_, _, _, cx_ref, mo_ref) in enumerate(dirs):
        local = [gate[d][j][:2] + (prods[d][j],) for j in range(n_ch)]
        m_run = mfin_ref[0, d][:, 0:1]
        state = [cfin_ref[d, h] for h in range(MLSTM_HEADS)]
        for j in (range(n_ch) if d == 0 else reversed(range(n_ch))):
            b_last, a, us = local[j]
            m_new = jnp.maximum(a, b_last + m_run)
            decay = jnp.exp(b_last + m_run - m_new)
            gam = jnp.exp(a - m_new)
            mo_ref[0, j] = jnp.broadcast_to(m_run, (MLSTM_HEADS, CHUNK))
            for h in range(MLSTM_HEADS):
                cx_ref[0, j, h] = state[h].astype(BF16)
                state[h] = decay[h:h + 1, :] * state[h] + gam[h:h + 1, :] * us[h]
            m_run = m_new
        mfin_ref[0, d] = jnp.broadcast_to(m_run, (MLSTM_HEADS, CHUNK))
        for h in range(MLSTM_HEADS):
            cfin_ref[d, h] = state[h]

    if emit_state:
        c_out_ref, n_out_ref = outs[5:7]

        @pl.when(pl.program_id(1) == pl.num_programs(1) - 1)
        def _():
            for d in range(2):
                for h in range(MLSTM_HEADS):
                    c_out_ref[0, 0, d, h] = cfin_ref[d, h, 0:MLSTM_DK, :].T
                    n_out_ref[0, 0, d, h:h + 1, :] = cfin_ref[d, h, MLSTM_DK:MLSTM_DK + 1, :]


def _scan_call(mk, mvT, gates, init, n_batch, seq, emit_state):
    nc = seq // CHUNK
    g = _per_step(nc, SCAN_CHUNKS)
    ns = nc // g
    fwd3 = lambda b, c: (b * ns + c, 0, 0)
    bwd3 = lambda b, c: (b * ns + ns - 1 - c, 0, 0)
    fwd2 = lambda b, c: (b * ns + c, 0)
    bwd2 = lambda b, c: (b * ns + ns - 1 - c, 0)
    k_blk, v_blk, g_blk = (g * CHUNK, MLSTM_WIDTH), (g, MLSTM_HEADS * MV_ROWS, CHUNK), (g, N_GATES, CHUNK)
    cx_blk = (1, g, MLSTM_HEADS, MV_ROWS, MLSTM_DK)
    m_blk = (1, g, MLSTM_HEADS, CHUNK)
    sc_blk = (1, 1, 2, MLSTM_HEADS, MLSTM_DK, MLSTM_DK)
    sn_blk = (1, 1, 2, MLSTM_HEADS, MLSTM_DK)
    sm_blk = (1, 2, MLSTM_HEADS, CHUNK)
    per_request6 = lambda b, c: (b, 0, 0, 0, 0, 0)
    per_request5 = lambda b, c: (b, 0, 0, 0, 0)
    in_specs = [pl.BlockSpec(k_blk, fwd2), pl.BlockSpec(v_blk, fwd3), pl.BlockSpec(g_blk, fwd3),
                pl.BlockSpec(k_blk, bwd2), pl.BlockSpec(v_blk, bwd3), pl.BlockSpec(g_blk, bwd3)]
    args = [mk, mvT, gates, mk, mvT, gates]
    if init is not None:
        in_specs += [pl.BlockSpec(sc_blk, per_request6), pl.BlockSpec(sn_blk, per_request5),
                     pl.BlockSpec(sm_blk, lambda b, c: (b, 0, 0, 0))]
        args += list(init)
    out_shape = [jax.ShapeDtypeStruct((n_batch, nc) + cx_blk[2:], BF16),
                 jax.ShapeDtypeStruct((n_batch, nc) + cx_blk[2:], BF16),
                 jax.ShapeDtypeStruct((n_batch, nc) + m_blk[2:], F32),
                 jax.ShapeDtypeStruct((n_batch, nc) + m_blk[2:], F32),
                 jax.ShapeDtypeStruct((n_batch,) + sm_blk[1:], F32)]
    out_specs = [pl.BlockSpec(cx_blk, lambda b, c: (b, c, 0, 0, 0)),
                 pl.BlockSpec(cx_blk, lambda b, c: (b, ns - 1 - c, 0, 0, 0)),
                 pl.BlockSpec(m_blk, lambda b, c: (b, c, 0, 0)),
                 pl.BlockSpec(m_blk, lambda b, c: (b, ns - 1 - c, 0, 0)),
                 pl.BlockSpec(sm_blk, lambda b, c: (b, 0, 0, 0))]
    if emit_state:
        out_shape += [jax.ShapeDtypeStruct((n_batch,) + sc_blk[1:], F32),
                      jax.ShapeDtypeStruct((n_batch,) + sn_blk[1:], F32)]
        out_specs += [pl.BlockSpec(sc_blk, per_request6), pl.BlockSpec(sn_blk, per_request5)]
    return pl.pallas_call(
        functools.partial(_scan_kernel, has_init=init is not None, emit_state=emit_state),
        out_shape=out_shape,
        grid=(n_batch, ns),
        in_specs=in_specs,
        out_specs=out_specs,
        scratch_shapes=[pltpu.VMEM((2, MLSTM_HEADS, MV_ROWS, MLSTM_DK), F32)],
        compiler_params=pltpu.CompilerParams(dimension_semantics=("parallel", "arbitrary"),
                                             vmem_limit_bytes=VMEM_LIMIT),
        name="mlstm_state_scan",
    )(*args)


def _mlstm_out_kernel(qT_ref, k_ref, vT_ref, g_ref, cxf_ref, cxb_ref, mf_ref, mb_ref, ogT_ref, nw_ref, o_ref):
    n_ch = g_ref.shape[0]
    nh = MLSTM_HEADS
    le, ge = _tri_masks()
    gs = [g_ref[j] for j in range(n_ch)]
    gate = [_gate_rows(gs, d, le, ge) for d in range(2)]
    kq = {}
    for j in range(n_ch):
        for h in range(nh):
            sl = slice(h * MLSTM_DK, (h + 1) * MLSTM_DK)
            kq[j, h] = jnp.dot(k_ref[j * CHUNK:(j + 1) * CHUNK, sl], qT_ref[j, sl, :],
                               preferred_element_type=F32)
    for j in range(n_ch):
        tok = slice(j * CHUNK, (j + 1) * CHUNK)
        r2, b2 = [], []
        for d in range(2):
            lf, b, ig = gate[d][j]
            r2.append((ig - b) * LOG2E)
            b2.append(b * LOG2E)
        r_cols = jnp.concatenate(r2, axis=0).T
        for h in range(nh):
            sl = slice(h * MLSTM_DK, (h + 1) * MLSTM_DK)
            qT = qT_ref[j, sl, :]
            vx = vT_ref[j, h * MV_ROWS:(h + 1) * MV_ROWS, :]
            sT = kq[j, h]
            h_sum = jnp.zeros((MLSTM_DK, CHUNK), F32)
            for d, (cx_ref, m_ref) in enumerate(((cxf_ref, mf_ref), (cxb_ref, mb_ref))):
                mask = le if d == 0 else ge
                i = d * nh + h
                rm = jnp.where(mask, jnp.broadcast_to(r_cols[:, i:i + 1], (CHUNK, CHUNK)), NEG_BIG)
                cm = jnp.max(rm, axis=0, keepdims=True)
                m_prev = m_ref[0, j, h:h + 1, :] * LOG2E
                mm = jnp.maximum(cm, m_prev)
                pT = (sT * jnp.exp2(rm - mm)).astype(BF16)
                qi = qT * jnp.exp2(m_prev - mm).astype(BF16)
                lhs = jnp.concatenate([vx, cx_ref[0, j, h]], axis=1)
                mix = jnp.dot(lhs, jnp.concatenate([pT, qi], axis=0), preferred_element_type=F32)
                den = mix[MLSTM_DK:MLSTM_DK + 1]
                clamp = jnp.exp2(-(b2[d][h:h + 1, :] + mm))
                h_sum = h_sum + mix[:MLSTM_DK] * (1.0 / jnp.maximum(jnp.abs(den), clamp))
            ms = jnp.mean(h_sum * h_sum, axis=0, keepdims=True)
            hn = h_sum * lax.rsqrt(ms + EPS) * nw_ref[sl, :] * ogT_ref[j, sl, :].astype(F32)
            o_ref[tok, sl] = hn.T.astype(BF16)


def _mlstm_out_call(mqT, mk, mvT, gates, cxf, cxb, mf, mb, ogT, nwb, n_batch, seq):
    g = _per_step(seq // CHUNK, OUT_CHUNKS)
    nc = seq // (CHUNK * g)
    tok = lambda b, c: (b * nc + c, 0)
    chk = lambda b, c: (b * nc + c, 0, 0)
    cx_blk = (1, g, MLSTM_HEADS, MV_ROWS, MLSTM_DK)
    m_blk = (1, g, MLSTM_HEADS, CHUNK)
    return pl.pallas_call(
        _mlstm_out_kernel,
        out_shape=jax.ShapeDtypeStruct((n_batch * seq, MLSTM_WIDTH), BF16),
        grid=(n_batch, nc),
        in_specs=[pl.BlockSpec((g, MLSTM_WIDTH, CHUNK), chk),
                  pl.BlockSpec((g * CHUNK, MLSTM_WIDTH), tok),
                  pl.BlockSpec((g, MLSTM_HEADS * MV_ROWS, CHUNK), chk),
                  pl.BlockSpec((g, N_GATES, CHUNK), chk),
                  pl.BlockSpec(cx_blk, lambda b, c: (b, c, 0, 0, 0)),
                  pl.BlockSpec(cx_blk, lambda b, c: (b, c, 0, 0, 0)),
                  pl.BlockSpec(m_blk, lambda b, c: (b, c, 0, 0)),
                  pl.BlockSpec(m_blk, lambda b, c: (b, c, 0, 0)),
                  pl.BlockSpec((g, MLSTM_WIDTH, CHUNK), chk),
                  pl.BlockSpec((MLSTM_WIDTH, CHUNK), lambda b, c: (0, 0))],
        out_specs=pl.BlockSpec((g * CHUNK, MLSTM_WIDTH), tok),
        compiler_params=pltpu.CompilerParams(dimension_semantics=("parallel", "parallel"),
                                             vmem_limit_bytes=VMEM_LIMIT),
        name="mlstm_chunk_out",
    )(mqT, mk, mvT, gates, cxf, cxb, mf, mb, ogT, nwb)


def _ffn_kernel(x_ref, a_ref, hm_ref, mods_ref, wo_ref, n2w_ref, wgu_ref, wd_ref, fw_ref, o_ref):
    g1 = mods_ref[0, 2:3, :]
    sh2 = mods_ref[0, 3:4, :]
    sc2 = mods_ref[0, 4:5, :]
    g2 = mods_ref[0, 5:6, :]
    subs = [slice(a, a + FFN_SUB) for a in range(0, x_ref.shape[0], FFN_SUB)]
    mix = [jnp.dot(a_ref[s, :], wo_ref[0:ATTN_WIDTH, :], preferred_element_type=F32)
           + jnp.dot(hm_ref[s, :], wo_ref[ATTN_WIDTH:D_MODEL, :], preferred_element_type=F32) for s in subs]
    x1 = [x_ref[s, :] + g1 * m for s, m in zip(subs, mix)]
    h2 = [(_rms(x) * n2w_ref[...] * (1.0 + sc2) + sh2).astype(BF16) for x in x1]
    acc = [jnp.zeros(x.shape, F32) for x in x1]
    for c0 in range(0, D_FF, FF_CHUNK):
        c1 = min(c0 + FF_CHUNK, D_FF)
        act = []
        for h in h2:
            gg = jnp.dot(h, wgu_ref[:, c0:c1], preferred_element_type=F32)
            uu = jnp.dot(h, wgu_ref[:, D_FF + c0:D_FF + c1], preferred_element_type=F32)
            act.append((gg * jax.nn.sigmoid(gg) * uu).astype(BF16))
        acc = [a + jnp.dot(t, wd_ref[c0:c1, :], preferred_element_type=F32) for a, t in zip(acc, act)]
    for s, x, a in zip(subs, x1, acc):
        o_ref[s, :] = _rms(x + g2 * a) * fw_ref[...]


def _ffn_call(x2d, attn_o, hm, mods, per_request, seq, wo, n2w, wgu, wd, fw):
    T = x2d.shape[0]
    tm = TOKEN_TILE
    mod_map = _mod_map(per_request, seq, tm)
    return pl.pallas_call(
        _ffn_kernel,
        out_shape=jax.ShapeDtypeStruct((T, D_MODEL), F32),
        grid=(T // tm,),
        in_specs=[pl.BlockSpec((tm, D_MODEL), lambda i: (i, 0)),
                  pl.BlockSpec((tm, ATTN_WIDTH), lambda i: (i, 0)),
                  pl.BlockSpec((tm, MLSTM_WIDTH), lambda i: (i, 0)),
                  pl.BlockSpec((1, 6, D_MODEL), mod_map),
                  _resident(wo.shape), _resident((1, D_MODEL)), _resident(wgu.shape),
                  _resident(wd.shape), _resident((1, D_MODEL))],
        out_specs=pl.BlockSpec((tm, D_MODEL), lambda i: (i, 0)),
        compiler_params=pltpu.CompilerParams(dimension_semantics=("parallel",),
                                             vmem_limit_bytes=VMEM_LIMIT),
        name="out_proj_ffn",
    )(x2d, attn_o, hm, mods, wo, n2w, wgu, wd, fw)


def _rope_tables(seq, gain, scale):
    pos = jnp.arange(seq, dtype=jnp.int32)
    row_ids = (pos // GRID_W).astype(F32)
    col_ids = (pos % GRID_W).astype(F32)
    inv = ROPE_THETA ** (-jnp.arange(0, AXIS_DIM, 2, dtype=F32) / AXIS_DIM)
    ang_r = inv[:, None] * row_ids[None, :]
    ang_c = inv[:, None] * col_ids[None, :]
    cos = jnp.concatenate([jnp.cos(ang_r)] * 2 + [jnp.cos(ang_c)] * 2, axis=0)
    sin = jnp.concatenate([-jnp.sin(ang_r), jnp.sin(ang_r), -jnp.sin(ang_c), jnp.sin(ang_c)], axis=0)
    g = gain.astype(F32) * scale
    g_sw = jnp.concatenate([g[16:32], g[0:16], g[48:64], g[32:48]])
    return g[:, None] * cos, g_sw[:, None] * sin


def _flat_tables(gain, scale, width):
    g = gain.astype(F32) * scale
    return jnp.broadcast_to(g[:, None], (HEAD_DIM, width)), jnp.zeros((HEAD_DIM, width), F32)


def _group(x, mods, per_request, seq, weights, tables, cache, init, emit_cache):
    n_batch = x.shape[0]
    x2d = x.reshape(n_batch * seq, D_MODEL)
    outs = _proj_call(x2d, mods, per_request, seq, weights["n1w"], weights["wt"], weights["wn"],
                      weights["gbias"], tables, emit_cache)
    qT, k, vT, mqT, mk, mvT, ogT, gates = outs[:8]
    q_tiles = _per_step(seq // Q_TILE, Q_TILES_PER_STEP)
    attn_o = _attn_call(qT, k, vT, n_batch, seq, q_tiles, cache)
    cxf, cxb, mf, mb, *final = _scan_call(mk, mvT, gates, init, n_batch, seq, emit_cache)
    hm = _mlstm_out_call(mqT, mk, mvT, gates, cxf, cxb, mf, mb, ogT, weights["mnw"], n_batch, seq)
    y = _ffn_call(x2d, attn_o, hm, mods, per_request, seq, weights["wo"], weights["n2w"], weights["wgu"],
                  weights["wd"], weights["fw"])
    return y.reshape(n_batch, seq, D_MODEL), outs[8:], final


def kernel(x_prompt, x_sample, cache_k, cache_v, state_C, state_n, state_m, c, c_ctx, w_ada, b_ada,
           norm1_w, w_in, gate_bias, q_norm_w, k_norm_w, mlstm_norm_w, w_out, norm2_w, w_gu, w_down,
           final_norm_w):
    depth = w_ada.shape[0]
    assert depth == 1, "single-layer trunk"
    n_ctx, ctx_len, _ = x_prompt.shape
    n_dec, dec_len, _ = x_sample.shape
    l = 0

    cond = jnp.zeros((16, D_MODEL), F32).at[:n_dec].set(c).at[n_dec].set(c_ctx)
    mods = _mods_call(cond, w_ada[l], b_ada[l]).reshape(16, 6, D_MODEL)

    wi = w_in[l]
    aq, ak, av, mq_w, mk_w, mv_w, mo_w, mg_w = jnp.split(
        wi, [512, 640, 768, 1280, 1792, 2304, 2816], axis=1)
    weights = {
        "wt": jnp.concatenate([aq, ak, av, mq_w, mv_w, mo_w, mg_w], axis=1).T.astype(BF16),
        "wn": mk_w.astype(BF16),
        "gbias": gate_bias[l].reshape(N_GATES, 1).astype(F32),
        "n1w": norm1_w[l].reshape(1, D_MODEL),
        "n2w": norm2_w[l].reshape(1, D_MODEL),
        "mnw": jnp.broadcast_to(mlstm_norm_w[l].astype(F32)[:, None], (MLSTM_WIDTH, CHUNK)),
        "wo": w_out[l].astype(BF16),
        "wgu": w_gu[l].astype(BF16),
        "wd": w_down[l].astype(BF16),
        "fw": final_norm_w.reshape(1, D_MODEL),
    }
    q_scale = (HEAD_DIM ** -0.5) * LOG2E

    tabs_ctx = _flat_tables(q_norm_w[l], q_scale, PROJ_TILE) + _flat_tables(k_norm_w[l], 1.0, PROJ_TILE)
    y_prompt, (kc, vc), (mfin, new_state_C, new_state_n) = _group(
        x_prompt, mods[n_dec:n_dec + 1], False, ctx_len, weights, tabs_ctx, None, None, True)
    new_cache_k = kc.reshape(n_ctx, 1, ctx_len, N_KV_HEADS, HEAD_DIM)
    new_cache_v = vc.reshape(n_ctx, 1, ctx_len, N_KV_HEADS, HEAD_DIM)
    new_state_m = mfin[..., 0][:, None]

    tabs_dec = _rope_tables(dec_len, q_norm_w[l], q_scale) + _rope_tables(dec_len, k_norm_w[l], 1.0)
    past = cache_k.shape[2]
    ck = cache_k[:, l].reshape(n_dec, past, KV_WIDTH).astype(BF16)
    cvT = cache_v[:, l].reshape(n_dec, past // KEY_BLOCK, KEY_BLOCK, KV_WIDTH).transpose(0, 1, 3, 2)
    pad_ones = jnp.ones(cvT.shape[:2] + (V_ROWS - HEAD_DIM, KEY_BLOCK), cvT.dtype)
    cvT = jnp.concatenate([cvT[:, :, :HEAD_DIM], pad_ones, cvT[:, :, HEAD_DIM:], pad_ones], axis=2).astype(BF16)
    m0 = jnp.broadcast_to(state_m[:, l].astype(F32)[..., None], (n_dec, 2, MLSTM_HEADS, CHUNK))
    init = (state_C[:, l:l + 1].astype(F32), state_n[:, l:l + 1].astype(F32), m0)
    y_sample, _, _ = _group(x_sample, mods[:n_dec], True, dec_len, weights, tabs_dec, (ck, cvT), init, False)

    return (y_prompt, y_sample, new_cache_k, new_cache_v, new_state_C, new_state_n, new_state_m)
```

```python
import functools

import jax
import jax.numpy as jnp
from jax import lax
from jax.experimental import pallas as pl
from jax.experimental.pallas import tpu as pltpu

F32 = jnp.float32
BF16 = jnp.bfloat16

D_MODEL = 1024
HEAD_DIM = 64
N_Q_HEADS = 8
N_KV_HEADS = 2
KV_GROUP = N_Q_HEADS // N_KV_HEADS
ATTN_WIDTH = N_Q_HEADS * HEAD_DIM
KV_WIDTH = N_KV_HEADS * HEAD_DIM
AXIS_DIM = HEAD_DIM // 2
GRID_W = 64
ROPE_THETA = 10000.0
MLSTM_HEADS = 4
MLSTM_DK = 128
MLSTM_WIDTH = MLSTM_HEADS * MLSTM_DK
CHUNK = 128
D_FF = 2816
N_GATES = 4 * MLSTM_HEADS
EPS = 1e-6
LOG2E = 1.4426950408889634
NEG_BIG = -1e30

TOKEN_TILE = 512
FFN_SUB = 256
PROJ_TILE = 1024
PROJ_SUB = 256
Q_TILE = 256
KEY_BLOCK = 256
QK_LOOKAHEAD = 6
S_SLOTS = 8
Q_TILES_PER_STEP = 16
SCAN_CHUNKS = 16
OUT_CHUNKS = 4
FF_CHUNK = 1024
ADA_TILE = 1536
MOD_ROWS = 16
BF16_TILE_ROWS = 16
V_ROWS = HEAD_DIM + BF16_TILE_ROWS
MV_ROWS = MLSTM_DK + BF16_TILE_ROWS
VMEM_CAPACITY_V7X = 64 * 1024 * 1024
VMEM_LIMIT = VMEM_CAPACITY_V7X - 8 * 1024 * 1024

_SPLITS = (ATTN_WIDTH, KV_WIDTH, KV_WIDTH, MLSTM_WIDTH, MLSTM_WIDTH, MLSTM_WIDTH, MLSTM_WIDTH, N_GATES)
_R_Q = 0
_R_K = _R_Q + ATTN_WIDTH
_R_V = _R_K + KV_WIDTH
_R_MQ = _R_V + KV_WIDTH
_R_MV = _R_MQ + MLSTM_WIDTH
_R_MO = _R_MV + MLSTM_WIDTH
_R_G = _R_MO + MLSTM_WIDTH
_R_END = _R_G + N_GATES


def _per_step(n, cap):
    g = cap
    while n % g:
        g //= 2
    return g


def _resident(shape):
    nd = len(shape)
    return pl.BlockSpec(shape, lambda *_: (0,) * nd, pipeline_mode=pl.Buffered(1))


def _mods_kernel(cond_ref, w_ref, b_ref, o_ref):
    c = cond_ref[...]
    s = c * jax.nn.sigmoid(c)
    o_ref[...] = jnp.dot(s, w_ref[...], preferred_element_type=F32) + b_ref[...]


def _mods_call(cond, w_ada, b_ada):
    rows, n_out = cond.shape[0], w_ada.shape[1]
    return pl.pallas_call(
        _mods_kernel,
        out_shape=jax.ShapeDtypeStruct((rows, n_out), F32),
        grid=(n_out // ADA_TILE,),
        in_specs=[pl.BlockSpec((rows, D_MODEL), lambda j: (0, 0)),
                  pl.BlockSpec((D_MODEL, ADA_TILE), lambda j: (0, j)),
                  pl.BlockSpec((1, ADA_TILE), lambda j: (0, j))],
        out_specs=pl.BlockSpec((rows, ADA_TILE), lambda j: (0, j)),
        compiler_params=pltpu.CompilerParams(dimension_semantics=("parallel",),
                                             vmem_limit_bytes=VMEM_LIMIT),
        name="adaln_mods",
    )(cond, w_ada, b_ada.reshape(1, n_out))


def _swap_halves(x, axis):
    hw = AXIS_DIM // 2
    parts = [lax.slice_in_dim(x, s, s + hw, axis=axis) for s in range(0, HEAD_DIM, hw)]
    return jnp.concatenate([parts[1], parts[0], parts[3], parts[2]], axis=axis)


def _rms(x):
    return x * lax.rsqrt(jnp.mean(x * x, axis=-1, keepdims=True) + EPS)


def _proj_kernel(x_ref, mods_ref, n1w_ref, wt_ref, wn_ref, gb_ref, tqa_ref, tqb_ref, tka_ref, tkb_ref,
                 qT_ref, k_ref, vT_ref, mqT_ref, mk_ref, mvT_ref, ogT_ref, g_ref, *cache_refs):
    sh = mods_ref[0, 0:1, :]
    sc = mods_ref[0, 1:2, :]
    sub = PROJ_SUB
    for a in range(0, x_ref.shape[0], sub):
        tok = slice(a, a + sub)
        h = (_rms(x_ref[tok, :]) * n1w_ref[...] * (1.0 + sc) + sh).astype(BF16)

        yt = lax.dot_general(wt_ref[...], h, (((1,), (1,)), ((), ())), preferred_element_type=F32)

        n_qk = N_Q_HEADS + N_KV_HEADS
        qk = yt[_R_Q:_R_V].reshape(n_qk, HEAD_DIM, sub)
        qk = qk * lax.rsqrt(jnp.sum(qk * qk, axis=1, keepdims=True) * (1.0 / HEAD_DIM) + EPS)
        sw = _swap_halves(qk, axis=1)
        q = qk[:N_Q_HEADS] * tqa_ref[:, tok][None] + sw[:N_Q_HEADS] * tqb_ref[:, tok][None]
        q = q.reshape(ATTN_WIDTH, sub).astype(BF16)
        for j in range(sub // Q_TILE):
            qT_ref[a // Q_TILE + j] = q[:, j * Q_TILE:(j + 1) * Q_TILE]
        k = qk[N_Q_HEADS:] * tka_ref[:, tok][None] + sw[N_Q_HEADS:] * tkb_ref[:, tok][None]
        kt = k.reshape(KV_WIDTH, sub).T
        k_ref[tok, :] = kt.astype(BF16)

        vt = yt[_R_V:_R_MQ]
        ones = jnp.ones((V_ROWS - HEAD_DIM, sub), F32)
        vtb = jnp.concatenate([vt[:HEAD_DIM], ones, vt[HEAD_DIM:], ones], axis=0).astype(BF16)
        for j in range(sub // KEY_BLOCK):
            vT_ref[a // KEY_BLOCK + j] = vtb[:, j * KEY_BLOCK:(j + 1) * KEY_BLOCK]
        if cache_refs:
            kc_ref, vc_ref = cache_refs
            kc_ref[tok, :] = kt
            vc_ref[tok, :] = vt.T

        mq = yt[_R_MQ:_R_MV].astype(BF16)
        mv = yt[_R_MV:_R_MO]
        pad = jnp.ones((MV_ROWS - MLSTM_DK, sub), F32)
        mvx = jnp.concatenate(
            [blk for hh in range(MLSTM_HEADS) for blk in (mv[hh * MLSTM_DK:(hh + 1) * MLSTM_DK], pad)],
            axis=0).astype(BF16)
        og = jax.nn.sigmoid(yt[_R_MO:_R_G]).astype(BF16)
        gates = yt[_R_G:_R_END] + gb_ref[...]
        for j in range(sub // CHUNK):
            cs = slice(j * CHUNK, (j + 1) * CHUNK)
            mqT_ref[a // CHUNK + j] = mq[:, cs]
            mvT_ref[a // CHUNK + j] = mvx[:, cs]
            ogT_ref[a // CHUNK + j] = og[:, cs]
            g_ref[a // CHUNK + j] = gates[:, cs]

        mk = jnp.dot(h, wn_ref[...], preferred_element_type=F32)
        mk_ref[tok, :] = (mk * (MLSTM_DK ** -0.5)).astype(BF16)


def _mod_map(per_request, seq, tm):
    if not per_request:
        return lambda i: (0, 0, 0)
    tiles = seq // tm
    return lambda i: (i // tiles, 0, 0)


def _proj_call(x2d, mods, per_request, seq, n1w, wt, wn, gbias, tables, emit_cache):
    T = x2d.shape[0]
    tm = PROJ_TILE
    n_tiles = T // tm
    mod_map = _mod_map(per_request, seq, tm)
    if per_request:
        tiles = seq // tm
        tab_map = lambda i: (0, i % tiles)
    else:
        tab_map = lambda i: (0, 0)
    tab_spec = pl.BlockSpec((HEAD_DIM, tm), tab_map)
    out_shape = [
        jax.ShapeDtypeStruct((T // Q_TILE, ATTN_WIDTH, Q_TILE), BF16),
        jax.ShapeDtypeStruct((T, KV_WIDTH), BF16),
        jax.ShapeDtypeStruct((T // KEY_BLOCK, N_KV_HEADS * V_ROWS, KEY_BLOCK), BF16),
        jax.ShapeDtypeStruct((T // CHUNK, MLSTM_WIDTH, CHUNK), BF16),
        jax.ShapeDtypeStruct((T, MLSTM_WIDTH), BF16),
        jax.ShapeDtypeStruct((T // CHUNK, MLSTM_HEADS * MV_ROWS, CHUNK), BF16),
        jax.ShapeDtypeStruct((T // CHUNK, MLSTM_WIDTH, CHUNK), BF16),
        jax.ShapeDtypeStruct((T // CHUNK, N_GATES, CHUNK), F32),
    ]
    out_specs = [
        pl.BlockSpec((tm // Q_TILE, ATTN_WIDTH, Q_TILE), lambda i: (i, 0, 0)),
        pl.BlockSpec((tm, KV_WIDTH), lambda i: (i, 0)),
        pl.BlockSpec((tm // KEY_BLOCK, N_KV_HEADS * V_ROWS, KEY_BLOCK), lambda i: (i, 0, 0)),
        pl.BlockSpec((tm // CHUNK, MLSTM_WIDTH, CHUNK), lambda i: (i, 0, 0)),
        pl.BlockSpec((tm, MLSTM_WIDTH), lambda i: (i, 0)),
        pl.BlockSpec((tm // CHUNK, MLSTM_HEADS * MV_ROWS, CHUNK), lambda i: (i, 0, 0)),
        pl.BlockSpec((tm // CHUNK, MLSTM_WIDTH, CHUNK), lambda i: (i, 0, 0)),
        pl.BlockSpec((tm // CHUNK, N_GATES, CHUNK), lambda i: (i, 0, 0)),
    ]
    if emit_cache:
        out_shape += [jax.ShapeDtypeStruct((T, KV_WIDTH), F32)] * 2
        out_specs += [pl.BlockSpec((tm, KV_WIDTH), lambda i: (i, 0))] * 2
    return pl.pallas_call(
        _proj_kernel,
        out_shape=out_shape,
        grid=(n_tiles,),
        in_specs=[pl.BlockSpec((tm, D_MODEL), lambda i: (i, 0)),
                  pl.BlockSpec((1, 6, D_MODEL), mod_map),
                  _resident((1, D_MODEL)),
                  _resident(wt.shape),
                  _resident(wn.shape),
                  _resident((N_GATES, 1)),
                  tab_spec, tab_spec, tab_spec, tab_spec],
        out_specs=out_specs,
        compiler_params=pltpu.CompilerParams(dimension_semantics=("parallel",),
                                             vmem_limit_bytes=VMEM_LIMIT),
        name="norm1_in_proj",
    )(x2d, mods, n1w, wt, wn, gbias, *tables)


def _attn_kernel(*refs, n_blocks, has_cache):
    if has_cache:
        qT_ref, k_new_ref, vT_new_ref, ck_ref, cvT_ref, o_ref, qp_scr, m_scr, acc_scr, s_scr, k_ref, vT_ref = refs
        n_new = k_new_ref.shape[0]
        k_ref[0:n_new] = k_new_ref[...]
        k_ref[n_new:] = ck_ref[0]
        vT_ref[0:n_new // KEY_BLOCK] = vT_new_ref[...]
        vT_ref[n_new // KEY_BLOCK:] = cvT_ref[0]
    else:
        qT_ref, k_ref, vT_ref, o_ref, qp_scr, m_scr, acc_scr, s_scr = refs
    n_qt, _, tq = qT_ref.shape
    n_tasks = n_qt * N_Q_HEADS

    for t in range(n_tasks):
        g, hd = divmod(t, N_Q_HEADS)
        qg = qT_ref[g, hd * HEAD_DIM:(hd + 1) * HEAD_DIM, :]
        zero = jnp.zeros_like(qg)
        qp_scr[t] = jnp.concatenate([qg, zero] if hd < KV_GROUP else [zero, qg], axis=0)
    m_scr[...] = jnp.full(m_scr.shape, NEG_BIG, F32)
    acc_scr[...] = jnp.zeros(acc_scr.shape, F32)

    def scores(c, t):
        off = pl.multiple_of(c * KEY_BLOCK, KEY_BLOCK)
        return jnp.dot(k_ref[pl.ds(off, KEY_BLOCK), :], qp_scr[t], preferred_element_type=F32)

    assert n_tasks % S_SLOTS == 0 and QK_LOOKAHEAD < S_SLOTS
    for t in range(QK_LOOKAHEAD):
        s_scr[t] = scores(0, t)

    def body(c, _):
        c_next = jnp.minimum(c + 1, n_blocks - 1)
        for t in range(n_tasks):
            s = s_scr[t % S_SLOTS]
            ahead = t + QK_LOOKAHEAD
            if ahead < n_tasks:
                s_scr[ahead % S_SLOTS] = scores(c, ahead)
            else:
                s_scr[ahead % S_SLOTS] = scores(c_next, ahead - n_tasks)
            r0 = ((t % N_Q_HEADS) // KV_GROUP) * V_ROWS
            m = m_scr[t]
            m_new = jnp.maximum(m, jnp.max(s, axis=0, keepdims=True))
            p = jnp.exp2(s - m_new).astype(BF16)
            alpha = jnp.exp2(m - m_new)
            pv = jnp.dot(vT_ref[c, r0:r0 + V_ROWS, :], p, preferred_element_type=F32)
            acc_scr[t] = alpha * acc_scr[t] + pv
            m_scr[t] = m_new
        return 0

    lax.fori_loop(0, n_blocks, body, 0)
    for g in range(n_qt):
        acc = acc_scr[g * N_Q_HEADS:(g + 1) * N_Q_HEADS]
        out = acc[:, :HEAD_DIM] / acc[:, HEAD_DIM:HEAD_DIM + 1]
        o_ref[g * tq:(g + 1) * tq, :] = out.reshape(ATTN_WIDTH, tq).T.astype(BF16)


def _attn_call(qT, k, vT, n_batch, seq, q_tiles, cache=None):
    nq = seq // (Q_TILE * q_tiles)
    nkb = seq // KEY_BLOCK
    n_tasks = q_tiles * N_Q_HEADS
    v_rows = N_KV_HEADS * V_ROWS
    in_specs = [pl.BlockSpec((q_tiles, ATTN_WIDTH, Q_TILE), lambda b, i: (b * nq + i, 0, 0)),
                pl.BlockSpec((seq, KV_WIDTH), lambda b, i: (b, 0)),
                pl.BlockSpec((nkb, v_rows, KEY_BLOCK), lambda b, i: (b, 0, 0))]
    scratch = [pltpu.VMEM((n_tasks, KV_WIDTH, Q_TILE), BF16),
               pltpu.VMEM((n_tasks, 1, Q_TILE), F32),
               pltpu.VMEM((n_tasks, V_ROWS, Q_TILE), F32),
               pltpu.VMEM((S_SLOTS, KEY_BLOCK, Q_TILE), F32)]
    args = [qT, k, vT]
    if cache is not None:
        ck, cvT = cache
        past = ck.shape[1]
        in_specs += [pl.BlockSpec((1, past, KV_WIDTH), lambda b, i: (b, 0, 0)),
                     pl.BlockSpec((1, past // KEY_BLOCK, v_rows, KEY_BLOCK), lambda b, i: (b, 0, 0, 0))]
        scratch += [pltpu.VMEM((seq + past, KV_WIDTH), BF16),
                    pltpu.VMEM((nkb + past // KEY_BLOCK, v_rows, KEY_BLOCK), BF16)]
        args += [ck, cvT]
        nkb += past // KEY_BLOCK
    return pl.pallas_call(
        functools.partial(_attn_kernel, n_blocks=nkb, has_cache=cache is not None),
        out_shape=jax.ShapeDtypeStruct((n_batch * seq, ATTN_WIDTH), BF16),
        grid=(n_batch, nq),
        in_specs=in_specs,
        out_specs=pl.BlockSpec((q_tiles * Q_TILE, ATTN_WIDTH), lambda b, i: (b * nq + i, 0)),
        scratch_shapes=scratch,
        compiler_params=pltpu.CompilerParams(dimension_semantics=("parallel", "parallel"),
                                             vmem_limit_bytes=VMEM_LIMIT),
        name="gqa_attention",
    )(*args)


def _log_sigmoid(x):
    return jnp.minimum(x, 0.0) - jnp.log1p(jnp.exp(-jnp.abs(x)))


def _cumsum_lanes(x, tri):
    hi = x.astype(BF16)
    r1 = x - hi.astype(F32)
    mid = r1.astype(BF16)
    lo = (r1 - mid.astype(F32)).astype(BF16)
    return (jnp.dot(hi, tri, preferred_element_type=F32) + jnp.dot(mid, tri, preferred_element_type=F32)
            + jnp.dot(lo, tri, preferred_element_type=F32))


def _tri_masks():
    ri = lax.broadcasted_iota(jnp.int32, (CHUNK, CHUNK), 0)
    ci = lax.broadcasted_iota(jnp.int32, (CHUNK, CHUNK), 1)
    return ri <= ci, ri >= ci


def _gate_rows(gs, d, le, ge):
    nh = MLSTM_HEADS
    r_in, r_fg = 2 * d * nh, (2 * d + 1) * nh
    lf = _log_sigmoid(jnp.concatenate([g[r_fg:r_fg + nh] for g in gs], axis=0))
    tri = (le if d == 0 else ge).astype(BF16)
    b = _cumsum_lanes(lf, tri)
    return [(lf[nh * j:nh * (j + 1)], b[nh * j:nh * (j + 1)], g[r_in:r_in + nh]) for j, g in enumerate(gs)]


def _scan_kernel(*refs, has_init, emit_state):
    kf_ref, vf_ref, gf_ref, kb_ref, vb_ref, gb_ref = refs[:6]
    cfin_ref = refs[-1]
    outs = refs[9 if has_init else 6:-1]
    cxf_ref, cxb_ref, mf_ref, mb_ref, mfin_ref = outs[:5]

    @pl.when(pl.program_id(1) == 0)
    def _():
        if has_init:
            sc_ref, sn_ref, m0_ref = refs[6:9]
            for d in range(2):
                for h in range(MLSTM_HEADS):
                    cfin_ref[d, h, 0:MLSTM_DK, :] = sc_ref[0, 0, d, h].T
                    cfin_ref[d, h, MLSTM_DK:MV_ROWS, :] = jnp.broadcast_to(
                        sn_ref[0, 0, d, h:h + 1, :], (MV_ROWS - MLSTM_DK, MLSTM_DK))
            mfin_ref[...] = m0_ref[...]
        else:
            cfin_ref[...] = jnp.zeros(cfin_ref.shape, F32)
            mfin_ref[...] = jnp.zeros(mfin_ref.shape, F32)

    n_ch = gf_ref.shape[0]
    le, ge = _tri_masks()
    dirs = ((kf_ref, vf_ref, gf_ref, cxf_ref, mf_ref), (kb_ref, vb_ref, gb_ref, cxb_ref, mb_ref))
    gate = []
    for d, (_, _, g_ref, _, _) in enumerate(dirs):
        per_chunk = []
        for lf, b, ig in _gate_rows([g_ref[j] for j in range(n_ch)], d, le, ge):
            b_last = jnp.sum(lf, axis=1, keepdims=True)
            gt = b_last - b + ig
            a = jnp.max(gt, axis=1, keepdims=True)
            per_chunk.append((b_last, a, jnp.exp(gt - a)))
        gate.append(per_chunk)
    prods = []
    for d, (k_ref, v_ref, _, _, _) in enumerate(dirs):
        per_chunk = []
        for j in range(n_ch):
            wt = gate[d][j][2]
            us = []
            for h in range(MLSTM_HEADS):
                vw = (v_ref[j, h * MV_ROWS:(h + 1) * MV_ROWS, :].astype(F32) * wt[h:h + 1, :]).astype(BF16)
                kh = k_ref[j * CHUNK:(j + 1) * CHUNK, h * MLSTM_DK:(h + 1) * MLSTM_DK]
                us.append(jnp.dot(vw, kh, preferred_element_type=F32))
            per_chunk.append(us)
        prods.append(per_chunk)
    for d, (_, _, _, cx_ref, mo_ref) in enumerate(dirs):
        local = [gate[d][j][:2] + (prods[d][j],) for j in range(n_ch)]
        m_run = mfin_ref[0, d][:, 0:1]
        state = [cfin_ref[d, h] for h in range(MLSTM_HEADS)]
        for j in (range(n_ch) if d == 0 else reversed(range(n_ch))):
            b_last, a, us = local[j]
            m_new = jnp.maximum(a, b_last + m_run)
            decay = jnp.exp(b_last + m_run - m_new)
            gam = jnp.exp(a - m_new)
            mo_ref[0, j] = jnp.broadcast_to(m_run, (MLSTM_HEADS, CHUNK))
            for h in range(MLSTM_HEADS):
                cx_ref[0, j, h] = state[h].astype(BF16)
                state[h] = decay[h:h + 1, :] * state[h] + gam[h:h + 1, :] * us[h]
            m_run = m_new
        mfin_ref[0, d] = jnp.broadcast_to(m_run, (MLSTM_HEADS, CHUNK))
        for h in range(MLSTM_HEADS):
            cfin_ref[d, h] = state[h]

    if emit_state:
        c_out_ref, n_out_ref = outs[5:7]

        @pl.when(pl.program_id(1) == pl.num_programs(1) - 1)
        def _():
            for d in range(2):
                for h in range(MLSTM_HEADS):
                    c_out_ref[0, 0, d, h] = cfin_ref[d, h, 0:MLSTM_DK, :].T
                    n_out_ref[0, 0, d, h:h + 1, :] = cfin_ref[d, h, MLSTM_DK:MLSTM_DK + 1, :]


def _scan_call(mk, mvT, gates, init, n_batch, seq, emit_state):
    nc = seq // CHUNK
    g = _per_step(nc, SCAN_CHUNKS)
    ns = nc // g
    fwd3 = lambda b, c: (b * ns + c, 0, 0)
    bwd3 = lambda b, c: (b * ns + ns - 1 - c, 0, 0)
    fwd2 = lambda b, c: (b * ns + c, 0)
    bwd2 = lambda b, c: (b * ns + ns - 1 - c, 0)
    k_blk, v_blk, g_blk = (g * CHUNK, MLSTM_WIDTH), (g, MLSTM_HEADS * MV_ROWS, CHUNK), (g, N_GATES, CHUNK)
    cx_blk = (1, g, MLSTM_HEADS, MV_ROWS, MLSTM_DK)
    m_blk = (1, g, MLSTM_HEADS, CHUNK)
    sc_blk = (1, 1, 2, MLSTM_HEADS, MLSTM_DK, MLSTM_DK)
    sn_blk = (1, 1, 2, MLSTM_HEADS, MLSTM_DK)
    sm_blk = (1, 2, MLSTM_HEADS, CHUNK)
    per_request6 = lambda b, c: (b, 0, 0, 0, 0, 0)
    per_request5 = lambda b, c: (b, 0, 0, 0, 0)
    in_specs = [pl.BlockSpec(k_blk, fwd2), pl.BlockSpec(v_blk, fwd3), pl.BlockSpec(g_blk, fwd3),
                pl.BlockSpec(k_blk, bwd2), pl.BlockSpec(v_blk, bwd3), pl.BlockSpec(g_blk, bwd3)]
    args = [mk, mvT, gates, mk, mvT, gates]
    if init is not None:
        in_specs += [pl.BlockSpec(sc_blk, per_request6), pl.BlockSpec(sn_blk, per_request5),
                     pl.BlockSpec(sm_blk, lambda b, c: (b, 0, 0, 0))]
        args += list(init)
    out_shape = [jax.ShapeDtypeStruct((n_batch, nc) + cx_blk[2:], BF16),
                 jax.ShapeDtypeStruct((n_batch, nc) + cx_blk[2:], BF16),
                 jax.ShapeDtypeStruct((n_batch, nc) + m_blk[2:], F32),
                 jax.ShapeDtypeStruct((n_batch, nc) + m_blk[2:], F32),
                 jax.ShapeDtypeStruct((n_batch,) + sm_blk[1:], F32)]
    out_specs = [pl.BlockSpec(cx_blk, lambda b, c: (b, c, 0, 0, 0)),
                 pl.BlockSpec(cx_blk, lambda b, c: (b, ns - 1 - c, 0, 0, 0)),
                 pl.BlockSpec(m_blk, lambda b, c: (b, c, 0, 0)),
                 pl.BlockSpec(m_blk, lambda b, c: (b, ns - 1 - c, 0, 0)),
                 pl.BlockSpec(sm_blk, lambda b, c: (b, 0, 0, 0))]
    if emit_state:
        out_shape += [jax.ShapeDtypeStruct((n_batch,) + sc_blk[1:], F32),
                      jax.ShapeDtypeStruct((n_batch,) + sn_blk[1:], F32)]
        out_specs += [pl.BlockSpec(sc_blk, per_request6), pl.BlockSpec(sn_blk, per_request5)]
    return pl.pallas_call(
        functools.partial(_scan_kernel, has_init=init is not None, emit_state=emit_state),
        out_shape=out_shape,
        grid=(n_batch, ns),
        in_specs=in_specs,
        out_specs=out_specs,
        scratch_shapes=[pltpu.VMEM((2, MLSTM_HEADS, MV_ROWS, MLSTM_DK), F32)],
        compiler_params=pltpu.CompilerParams(dimension_semantics=("parallel", "arbitrary"),
                                             vmem_limit_bytes=VMEM_LIMIT),
        name="mlstm_state_scan",
    )(*args)


def _mlstm_out_kernel(qT_ref, k_ref, vT_ref, g_ref, cxf_ref, cxb_ref, mf_ref, mb_ref, ogT_ref, nw_ref, o_ref):
    n_ch = g_ref.shape[0]
    nh = MLSTM_HEADS
    le, ge = _tri_masks()
    gs = [g_ref[j] for j in range(n_ch)]
    gate = [_gate_rows(gs, d, le, ge) for d in range(2)]
    kq = {}
    for j in range(n_ch):
        for h in range(nh):
            sl = slice(h * MLSTM_DK, (h + 1) * MLSTM_DK)
            kq[j, h] = jnp.dot(k_ref[j * CHUNK:(j + 1) * CHUNK, sl], qT_ref[j, sl, :],
                               preferred_element_type=F32)
    for j in range(n_ch):
        tok = slice(j * CHUNK, (j + 1) * CHUNK)
        r2, b2 = [], []
        for d in range(2):
            lf, b, ig = gate[d][j]
            r2.append((ig - b) * LOG2E)
            b2.append(b * LOG2E)
        r_cols = jnp.concatenate(r2, axis=0).T
        for h in range(nh):
            sl = slice(h * MLSTM_DK, (h + 1) * MLSTM_DK)
            qT = qT_ref[j, sl, :]
            vx = vT_ref[j, h * MV_ROWS:(h + 1) * MV_ROWS, :]
            sT = kq[j, h]
            h_sum = jnp.zeros((MLSTM_DK, CHUNK), F32)
            for d, (cx_ref, m_ref) in enumerate(((cxf_ref, mf_ref), (cxb_ref, mb_ref))):
                mask = le if d == 0 else ge
                i = d * nh + h
                rm = jnp.where(mask, jnp.broadcast_to(r_cols[:, i:i + 1], (CHUNK, CHUNK)), NEG_BIG)
                cm = jnp.max(rm, axis=0, keepdims=True)
                m_prev = m_ref[0, j, h:h + 1, :] * LOG2E
                mm = jnp.maximum(cm, m_prev)
                pT = (sT * jnp.exp2(rm - mm)).astype(BF16)
                qi = qT * jnp.exp2(m_prev - mm).astype(BF16)
                lhs = jnp.concatenate([vx, cx_ref[0, j, h]], axis=1)
                mix = jnp.dot(lhs, jnp.concatenate([pT, qi], axis=0), preferred_element_type=F32)
                den = mix[MLSTM_DK:MLSTM_DK + 1]
                clamp = jnp.exp2(-(b2[d][h:h + 1, :] + mm))
                h_sum = h_sum + mix[:MLSTM_DK] * (1.0 / jnp.maximum(jnp.abs(den), clamp))
            ms = jnp.mean(h_sum * h_sum, axis=0, keepdims=True)
            hn = h_sum * lax.rsqrt(ms + EPS) * nw_ref[sl, :] * ogT_ref[j, sl, :].astype(F32)
            o_ref[tok, sl] = hn.T.astype(BF16)


def _mlstm_out_call(mqT, mk, mvT, gates, cxf, cxb, mf, mb, ogT, nwb, n_batch, seq):
    g = _per_step(seq // CHUNK, OUT_CHUNKS)
    nc = seq // (CHUNK * g)
    tok = lambda b, c: (b * nc + c, 0)
    chk = lambda b, c: (b * nc + c, 0, 0)
    cx_blk = (1, g, MLSTM_HEADS, MV_ROWS, MLSTM_DK)
    m_blk = (1, g, MLSTM_HEADS, CHUNK)
    return pl.pallas_call(
        _mlstm_out_kernel,
        out_shape=jax.ShapeDtypeStruct((n_batch * seq, MLSTM_WIDTH), BF16),
        grid=(n_batch, nc),
        in_specs=[pl.BlockSpec((g, MLSTM_WIDTH, CHUNK), chk),
                  pl.BlockSpec((g * CHUNK, MLSTM_WIDTH), tok),
                  pl.BlockSpec((g, MLSTM_HEADS * MV_ROWS, CHUNK), chk),
                  pl.BlockSpec((g, N_GATES, CHUNK), chk),
                  pl.BlockSpec(cx_blk, lambda b, c: (b, c, 0, 0, 0)),
                  pl.BlockSpec(cx_blk, lambda b, c: (b, c, 0, 0, 0)),
                  pl.BlockSpec(m_blk, lambda b, c: (b, c, 0, 0)),
                  pl.BlockSpec(m_blk, lambda b, c: (b, c, 0, 0)),
                  pl.BlockSpec((g, MLSTM_WIDTH, CHUNK), chk),
                  pl.BlockSpec((MLSTM_WIDTH, CHUNK), lambda b, c: (0, 0))],
        out_specs=pl.BlockSpec((g * CHUNK, MLSTM_WIDTH), tok),
        compiler_params=pltpu.CompilerParams(dimension_semantics=("parallel", "parallel"),
                                             vmem_limit_bytes=VMEM_LIMIT),
        name="mlstm_chunk_out",
    )(mqT, mk, mvT, gates, cxf, cxb, mf, mb, ogT, nwb)


def _ffn_kernel(x_ref, a_ref, hm_ref, mods_ref, wo_ref, n2w_ref, wgu_ref, wd_ref, fw_ref, o_ref):
    g1 = mods_ref[0, 2:3, :]
    sh2 = mods_ref[0, 3:4, :]
    sc2 = mods_ref[0, 4:5, :]
    g2 = mods_ref[0, 5:6, :]
    subs = [slice(a, a + FFN_SUB) for a in range(0, x_ref.shape[0], FFN_SUB)]
    mix = [jnp.dot(a_ref[s, :], wo_ref[0:ATTN_WIDTH, :], preferred_element_type=F32)
           + jnp.dot(hm_ref[s, :], wo_ref[ATTN_WIDTH:D_MODEL, :], preferred_element_type=F32) for s in subs]
    x1 = [x_ref[s, :] + g1 * m for s, m in zip(subs, mix)]
    h2 = [(_rms(x) * n2w_ref[...] * (1.0 + sc2) + sh2).astype(BF16) for x in x1]
    acc = [jnp.zeros(x.shape, F32) for x in x1]
    for c0 in range(0, D_FF, FF_CHUNK):
        c1 = min(c0 + FF_CHUNK, D_FF)
        act = []
        for h in h2:
            gg = jnp.dot(h, wgu_ref[:, c0:c1], preferred_element_type=F32)
            uu = jnp.dot(h, wgu_ref[:, D_FF + c0:D_FF + c1], preferred_element_type=F32)
            act.append((gg * jax.nn.sigmoid(gg) * uu).astype(BF16))
        acc = [a + jnp.dot(t, wd_ref[c0:c1, :], preferred_element_type=F32) for a, t in zip(acc, act)]
    for s, x, a in zip(subs, x1, acc):
        o_ref[s, :] = _rms(x + g2 * a) * fw_ref[...]


def _ffn_call(x2d, attn_o, hm, mods, per_request, seq, wo, n2w, wgu, wd, fw):
    T = x2d.shape[0]
    tm = TOKEN_TILE
    mod_map = _mod_map(per_request, seq, tm)
    return pl.pallas_call(
        _ffn_kernel,
        out_shape=jax.ShapeDtypeStruct((T, D_MODEL), F32),
        grid=(T // tm,),
        in_specs=[pl.BlockSpec((tm, D_MODEL), lambda i: (i, 0)),
                  pl.BlockSpec((tm, ATTN_WIDTH), lambda i: (i, 0)),
                  pl.BlockSpec((tm, MLSTM_WIDTH), lambda i: (i, 0)),
                  pl.BlockSpec((1, 6, D_MODEL), mod_map),
                  _resident(wo.shape), _resident((1, D_MODEL)), _resident(wgu.shape),
                  _resident(wd.shape), _resident((1, D_MODEL))],
        out_specs=pl.BlockSpec((tm, D_MODEL), lambda i: (i, 0)),
        compiler_params=pltpu.CompilerParams(dimension_semantics=("parallel",),
                                             vmem_limit_bytes=VMEM_LIMIT),
        name="out_proj_ffn",
    )(x2d, attn_o, hm, mods, wo, n2w, wgu, wd, fw)


def _rope_tables(seq, gain, scale):
    pos = jnp.arange(seq, dtype=jnp.int32)
    row_ids = (pos // GRID_W).astype(F32)
    col_ids = (pos % GRID_W).astype(F32)
    inv = ROPE_THETA ** (-jnp.arange(0, AXIS_DIM, 2, dtype=F32) / AXIS_DIM)
    ang_r = inv[:, None] * row_ids[None, :]
    ang_c = inv[:, None] * col_ids[None, :]
    cos = jnp.concatenate([jnp.cos(ang_r)] * 2 + [jnp.cos(ang_c)] * 2, axis=0)
    sin = jnp.concatenate([-jnp.sin(ang_r), jnp.sin(ang_r), -jnp.sin(ang_c), jnp.sin(ang_c)], axis=0)
    g = gain.astype(F32) * scale
    g_sw = _swap_halves(g, axis=0)
    return g[:, None] * cos, g_sw[:, None] * sin


def _flat_tables(gain, scale, width):
    g = gain.astype(F32) * scale
    return jnp.broadcast_to(g[:, None], (HEAD_DIM, width)), jnp.zeros((HEAD_DIM, width), F32)


def _group(x, mods, per_request, seq, weights, tables, cache, init, emit_cache):
    n_batch = x.shape[0]
    x2d = x.reshape(n_batch * seq, D_MODEL)
    outs = _proj_call(x2d, mods, per_request, seq, weights["n1w"], weights["wt"], weights["wn"],
                      weights["gbias"], tables, emit_cache)
    qT, k, vT, mqT, mk, mvT, ogT, gates = outs[:8]
    q_tiles = _per_step(seq // Q_TILE, Q_TILES_PER_STEP)
    attn_o = _attn_call(qT, k, vT, n_batch, seq, q_tiles, cache)
    cxf, cxb, mf, mb, *final = _scan_call(mk, mvT, gates, init, n_batch, seq, emit_cache)
    hm = _mlstm_out_call(mqT, mk, mvT, gates, cxf, cxb, mf, mb, ogT, weights["mnw"], n_batch, seq)
    y = _ffn_call(x2d, attn_o, hm, mods, per_request, seq, weights["wo"], weights["n2w"], weights["wgu"],
                  weights["wd"], weights["fw"])
    return y.reshape(n_batch, seq, D_MODEL), outs[8:], final


def kernel(x_prompt, x_sample, cache_k, cache_v, state_C, state_n, state_m, c, c_ctx, w_ada, b_ada,
           norm1_w, w_in, gate_bias, q_norm_w, k_norm_w, mlstm_norm_w, w_out, norm2_w, w_gu, w_down,
           final_norm_w):
    depth = w_ada.shape[0]
    assert depth == 1, "single-layer trunk"
    n_ctx, ctx_len, _ = x_prompt.shape
    n_dec, dec_len, _ = x_sample.shape
    l = 0

    assert n_dec + 1 <= MOD_ROWS
    cond = jnp.zeros((MOD_ROWS, D_MODEL), F32).at[:n_dec].set(c).at[n_dec].set(c_ctx)
    mods = _mods_call(cond, w_ada[l], b_ada[l]).reshape(MOD_ROWS, 6, D_MODEL)

    wi = w_in[l]
    edges = [sum(_SPLITS[:i]) for i in range(1, len(_SPLITS))]
    aq, ak, av, mq_w, mk_w, mv_w, mo_w, mg_w = jnp.split(wi, edges, axis=1)
    weights = {
        "wt": jnp.concatenate([aq, ak, av, mq_w, mv_w, mo_w, mg_w], axis=1).T.astype(BF16),
        "wn": mk_w.astype(BF16),
        "gbias": gate_bias[l].reshape(N_GATES, 1).astype(F32),
        "n1w": norm1_w[l].reshape(1, D_MODEL),
        "n2w": norm2_w[l].reshape(1, D_MODEL),
        "mnw": jnp.broadcast_to(mlstm_norm_w[l].astype(F32)[:, None], (MLSTM_WIDTH, CHUNK)),
        "wo": w_out[l].astype(BF16),
        "wgu": w_gu[l].astype(BF16),
        "wd": w_down[l].astype(BF16),
        "fw": final_norm_w.reshape(1, D_MODEL),
    }
    q_scale = (HEAD_DIM ** -0.5) * LOG2E

    tabs_ctx = _flat_tables(q_norm_w[l], q_scale, PROJ_TILE) + _flat_tables(k_norm_w[l], 1.0, PROJ_TILE)
    y_prompt, (kc, vc), (mfin, new_state_C, new_state_n) = _group(
        x_prompt, mods[n_dec:n_dec + 1], False, ctx_len, weights, tabs_ctx, None, None, True)
    new_cache_k = kc.reshape(n_ctx, 1, ctx_len, N_KV_HEADS, HEAD_DIM)
    new_cache_v = vc.reshape(n_ctx, 1, ctx_len, N_KV_HEADS, HEAD_DIM)
    new_state_m = mfin[..., 0][:, None]

    tabs_dec = _rope_tables(dec_len, q_norm_w[l], q_scale) + _rope_tables(dec_len, k_norm_w[l], 1.0)
    past = cache_k.shape[2]
    ck = cache_k[:, l].reshape(n_dec, past, KV_WIDTH).astype(BF16)
    cvT = cache_v[:, l].reshape(n_dec, past // KEY_BLOCK, KEY_BLOCK, KV_WIDTH).transpose(0, 1, 3, 2)
    pad_ones = jnp.ones(cvT.shape[:2] + (V_ROWS - HEAD_DIM, KEY_BLOCK), cvT.dtype)
    cvT = jnp.concatenate([cvT[:, :, :HEAD_DIM], pad_ones, cvT[:, :, HEAD_DIM:], pad_ones], axis=2).astype(BF16)
    m0 = jnp.broadcast_to(state_m[:, l].astype(F32)[..., None], (n_dec, 2, MLSTM_HEADS, CHUNK))
    init = (state_C[:, l:l + 1].astype(F32), state_n[:, l:l + 1].astype(F32), m0)
    y_sample, _, _ = _group(x_sample, mods[:n_dec], True, dec_len, weights, tabs_dec, (ck, cvT), init, False)

    return (y_prompt, y_sample, new_cache_k, new_cache_v, new_state_C, new_state_n, new_state_m)
```

```python
import functools

import jax
import jax.numpy as jnp
from jax import lax
from jax.experimental import pallas as pl
from jax.experimental.pallas import tpu as pltpu

F32 = jnp.float32
BF16 = jnp.bfloat16

D_MODEL = 1024
HEAD_DIM = 64
N_Q_HEADS = 8
N_KV_HEADS = 2
KV_GROUP = N_Q_HEADS // N_KV_HEADS
ATTN_WIDTH = N_Q_HEADS * HEAD_DIM
KV_WIDTH = N_KV_HEADS * HEAD_DIM
AXIS_DIM = HEAD_DIM // 2
GRID_W = 64
ROPE_THETA = 10000.0
MLSTM_HEADS = 4
MLSTM_DK = 128
MLSTM_WIDTH = MLSTM_HEADS * MLSTM_DK
CHUNK = 128
D_FF = 2816
N_GATES = 4 * MLSTM_HEADS
EPS = 1e-6
LOG2E = 1.4426950408889634
NEG_BIG = -1e30

TOKEN_TILE = 512
FFN_SUB = 256
PROJ_TILE = 1024
PROJ_SUB = 256
Q_TILE = 256
KEY_BLOCK = 256
QK_LOOKAHEAD = 6
S_SLOTS = 8
Q_TILES_PER_STEP = 16
SCAN_CHUNKS = 32
OUT_CHUNKS = 4
FF_CHUNK = 1024
ADA_TILE = 1536
MOD_ROWS = 16
BF16_TILE_ROWS = 16
V_ROWS = HEAD_DIM + BF16_TILE_ROWS
MV_ROWS = MLSTM_DK + BF16_TILE_ROWS
VMEM_CAPACITY_V7X = 64 * 1024 * 1024
VMEM_LIMIT = VMEM_CAPACITY_V7X - 8 * 1024 * 1024

_SPLITS = (ATTN_WIDTH, KV_WIDTH, KV_WIDTH, MLSTM_WIDTH, MLSTM_WIDTH, MLSTM_WIDTH, MLSTM_WIDTH, N_GATES)
_R_Q = 0
_R_K = _R_Q + ATTN_WIDTH
_R_V = _R_K + KV_WIDTH
_R_MQ = _R_V + KV_WIDTH
_R_MV = _R_MQ + MLSTM_WIDTH
_R_MO = _R_MV + MLSTM_WIDTH
_R_G = _R_MO + MLSTM_WIDTH
_R_END = _R_G + N_GATES


def _per_step(n, cap):
    g = cap
    while n % g:
        g //= 2
    return g


def _resident(shape):
    nd = len(shape)
    return pl.BlockSpec(shape, lambda *_: (0,) * nd, pipeline_mode=pl.Buffered(1))


def _mods_kernel(cond_ref, w_ref, b_ref, o_ref):
    c = cond_ref[...]
    s = c * jax.nn.sigmoid(c)
    o_ref[...] = jnp.dot(s, w_ref[...], preferred_element_type=F32) + b_ref[...]


def _mods_call(cond, w_ada, b_ada):
    rows, n_out = cond.shape[0], w_ada.shape[1]
    return pl.pallas_call(
        _mods_kernel,
        out_shape=jax.ShapeDtypeStruct((rows, n_out), F32),
        grid=(n_out // ADA_TILE,),
        in_specs=[pl.BlockSpec((rows, D_MODEL), lambda j: (0, 0)),
                  pl.BlockSpec((D_MODEL, ADA_TILE), lambda j: (0, j)),
                  pl.BlockSpec((1, ADA_TILE), lambda j: (0, j))],
        out_specs=pl.BlockSpec((rows, ADA_TILE), lambda j: (0, j)),
        compiler_params=pltpu.CompilerParams(dimension_semantics=("parallel",),
                                             vmem_limit_bytes=VMEM_LIMIT),
        name="adaln_mods",
    )(cond, w_ada, b_ada.reshape(1, n_out))


def _swap_halves(x, axis):
    hw = AXIS_DIM // 2
    parts = [lax.slice_in_dim(x, s, s + hw, axis=axis) for s in range(0, HEAD_DIM, hw)]
    return jnp.concatenate([parts[1], parts[0], parts[3], parts[2]], axis=axis)


def _rms(x):
    return x * lax.rsqrt(jnp.mean(x * x, axis=-1, keepdims=True) + EPS)


def _proj_kernel(x_ref, mods_ref, n1w_ref, wt_ref, wn_ref, gb_ref, tqa_ref, tqb_ref, tka_ref, tkb_ref,
                 qT_ref, k_ref, vT_ref, mqT_ref, mk_ref, mvT_ref, ogT_ref, g_ref, *cache_refs):
    sh = mods_ref[0, 0:1, :]
    sc = mods_ref[0, 1:2, :]
    sub = PROJ_SUB
    for a in range(0, x_ref.shape[0], sub):
        tok = slice(a, a + sub)
        h = (_rms(x_ref[tok, :]) * n1w_ref[...] * (1.0 + sc) + sh).astype(BF16)

        yt = lax.dot_general(wt_ref[...], h, (((1,), (1,)), ((), ())), preferred_element_type=F32)

        n_qk = N_Q_HEADS + N_KV_HEADS
        qk = yt[_R_Q:_R_V].reshape(n_qk, HEAD_DIM, sub)
        qk = qk * lax.rsqrt(jnp.sum(qk * qk, axis=1, keepdims=True) * (1.0 / HEAD_DIM) + EPS)
        sw = _swap_halves(qk, axis=1)
        q = qk[:N_Q_HEADS] * tqa_ref[:, tok][None] + sw[:N_Q_HEADS] * tqb_ref[:, tok][None]
        q = q.reshape(ATTN_WIDTH, sub).astype(BF16)
        for j in range(sub // Q_TILE):
            qT_ref[a // Q_TILE + j] = q[:, j * Q_TILE:(j + 1) * Q_TILE]
        k = qk[N_Q_HEADS:] * tka_ref[:, tok][None] + sw[N_Q_HEADS:] * tkb_ref[:, tok][None]
        kt = k.reshape(KV_WIDTH, sub).T
        k_ref[tok, :] = kt.astype(BF16)

        vt = yt[_R_V:_R_MQ]
        ones = jnp.ones((V_ROWS - HEAD_DIM, sub), F32)
        vtb = jnp.concatenate([vt[:HEAD_DIM], ones, vt[HEAD_DIM:], ones], axis=0).astype(BF16)
        for j in range(sub // KEY_BLOCK):
            vT_ref[a // KEY_BLOCK + j] = vtb[:, j * KEY_BLOCK:(j + 1) * KEY_BLOCK]
        if cache_refs:
            kc_ref, vc_ref = cache_refs
            kc_ref[tok, :] = kt
            vc_ref[tok, :] = vt.T

        mq = yt[_R_MQ:_R_MV].astype(BF16)
        mv = yt[_R_MV:_R_MO]
        pad = jnp.ones((MV_ROWS - MLSTM_DK, sub), F32)
        mvx = jnp.concatenate(
            [blk for hh in range(MLSTM_HEADS) for blk in (mv[hh * MLSTM_DK:(hh + 1) * MLSTM_DK], pad)],
            axis=0).astype(BF16)
        og = jax.nn.sigmoid(yt[_R_MO:_R_G]).astype(BF16)
        gates = yt[_R_G:_R_END] + gb_ref[...]
        for j in range(sub // CHUNK):
            cs = slice(j * CHUNK, (j + 1) * CHUNK)
            mqT_ref[a // CHUNK + j] = mq[:, cs]
            mvT_ref[a // CHUNK + j] = mvx[:, cs]
            ogT_ref[a // CHUNK + j] = og[:, cs]
            g_ref[a // CHUNK + j] = gates[:, cs]

        mk = jnp.dot(h, wn_ref[...], preferred_element_type=F32)
        mk_ref[tok, :] = (mk * (MLSTM_DK ** -0.5)).astype(BF16)


def _mod_map(per_request, seq, tm):
    if not per_request:
        return lambda i: (0, 0, 0)
    tiles = seq // tm
    return lambda i: (i // tiles, 0, 0)


def _proj_call(x2d, mods, per_request, seq, n1w, wt, wn, gbias, tables, emit_cache):
    T = x2d.shape[0]
    tm = PROJ_TILE
    n_tiles = T // tm
    mod_map = _mod_map(per_request, seq, tm)
    if per_request:
        tiles = seq // tm
        tab_map = lambda i: (0, i % tiles)
    else:
        tab_map = lambda i: (0, 0)
    tab_spec = pl.BlockSpec((HEAD_DIM, tm), tab_map)
    out_shape = [
        jax.ShapeDtypeStruct((T // Q_TILE, ATTN_WIDTH, Q_TILE), BF16),
        jax.ShapeDtypeStruct((T, KV_WIDTH), BF16),
        jax.ShapeDtypeStruct((T // KEY_BLOCK, N_KV_HEADS * V_ROWS, KEY_BLOCK), BF16),
        jax.ShapeDtypeStruct((T // CHUNK, MLSTM_WIDTH, CHUNK), BF16),
        jax.ShapeDtypeStruct((T, MLSTM_WIDTH), BF16),
        jax.ShapeDtypeStruct((T // CHUNK, MLSTM_HEADS * MV_ROWS, CHUNK), BF16),
        jax.ShapeDtypeStruct((T // CHUNK, MLSTM_WIDTH, CHUNK), BF16),
        jax.ShapeDtypeStruct((T // CHUNK, N_GATES, CHUNK), F32),
    ]
    out_specs = [
        pl.BlockSpec((tm // Q_TILE, ATTN_WIDTH, Q_TILE), lambda i: (i, 0, 0)),
        pl.BlockSpec((tm, KV_WIDTH), lambda i: (i, 0)),
        pl.BlockSpec((tm // KEY_BLOCK, N_KV_HEADS * V_ROWS, KEY_BLOCK), lambda i: (i, 0, 0)),
        pl.BlockSpec((tm // CHUNK, MLSTM_WIDTH, CHUNK), lambda i: (i, 0, 0)),
        pl.BlockSpec((tm, MLSTM_WIDTH), lambda i: (i, 0)),
        pl.BlockSpec((tm // CHUNK, MLSTM_HEADS * MV_ROWS, CHUNK), lambda i: (i, 0, 0)),
        pl.BlockSpec((tm // CHUNK, MLSTM_WIDTH, CHUNK), lambda i: (i, 0, 0)),
        pl.BlockSpec((tm // CHUNK, N_GATES, CHUNK), lambda i: (i, 0, 0)),
    ]
    if emit_cache:
        out_shape += [jax.ShapeDtypeStruct((T, KV_WIDTH), F32)] * 2
        out_specs += [pl.BlockSpec((tm, KV_WIDTH), lambda i: (i, 0))] * 2
    return pl.pallas_call(
        _proj_kernel,
        out_shape=out_shape,
        grid=(n_tiles,),
        in_specs=[pl.BlockSpec((tm, D_MODEL), lambda i: (i, 0)),
                  pl.BlockSpec((1, 6, D_MODEL), mod_map),
                  _resident((1, D_MODEL)),
                  _resident(wt.shape),
                  _resident(wn.shape),
                  _resident((N_GATES, 1)),
                  tab_spec, tab_spec, tab_spec, tab_spec],
        out_specs=out_specs,
        compiler_params=pltpu.CompilerParams(dimension_semantics=("parallel",),
                                             vmem_limit_bytes=VMEM_LIMIT),
        name="norm1_in_proj",
    )(x2d, mods, n1w, wt, wn, gbias, *tables)


def _attn_kernel(*refs, n_blocks, has_cache):
    if has_cache:
        qT_ref, k_new_ref, vT_new_ref, ck_ref, cvT_ref, o_ref, qp_scr, m_scr, acc_scr, s_scr, k_ref, vT_ref = refs
        n_new = k_new_ref.shape[0]
        k_ref[0:n_new] = k_new_ref[...]
        k_ref[n_new:] = ck_ref[0]
        vT_ref[0:n_new // KEY_BLOCK] = vT_new_ref[...]
        vT_ref[n_new // KEY_BLOCK:] = cvT_ref[0]
    else:
        qT_ref, k_ref, vT_ref, o_ref, qp_scr, m_scr, acc_scr, s_scr = refs
    n_qt, _, tq = qT_ref.shape
    n_tasks = n_qt * N_Q_HEADS

    for t in range(n_tasks):
        g, hd = divmod(t, N_Q_HEADS)
        qg = qT_ref[g, hd * HEAD_DIM:(hd + 1) * HEAD_DIM, :]
        zero = jnp.zeros_like(qg)
        qp_scr[t] = jnp.concatenate([qg, zero] if hd < KV_GROUP else [zero, qg], axis=0)
    m_scr[...] = jnp.full(m_scr.shape, NEG_BIG, F32)
    acc_scr[...] = jnp.zeros(acc_scr.shape, F32)

    def scores(c, t):
        off = pl.multiple_of(c * KEY_BLOCK, KEY_BLOCK)
        return jnp.dot(k_ref[pl.ds(off, KEY_BLOCK), :], qp_scr[t], preferred_element_type=F32)

    assert n_tasks % S_SLOTS == 0 and QK_LOOKAHEAD < S_SLOTS
    for t in range(QK_LOOKAHEAD):
        s_scr[t] = scores(0, t)

    def body(c, _):
        c_next = jnp.minimum(c + 1, n_blocks - 1)
        for t in range(n_tasks):
            s = s_scr[t % S_SLOTS]
            ahead = t + QK_LOOKAHEAD
            if ahead < n_tasks:
                s_scr[ahead % S_SLOTS] = scores(c, ahead)
            else:
                s_scr[ahead % S_SLOTS] = scores(c_next, ahead - n_tasks)
            r0 = ((t % N_Q_HEADS) // KV_GROUP) * V_ROWS
            m = m_scr[t]
            m_new = jnp.maximum(m, jnp.max(s, axis=0, keepdims=True))
            p = jnp.exp2(s - m_new).astype(BF16)
            alpha = jnp.exp2(m - m_new)
            pv = jnp.dot(vT_ref[c, r0:r0 + V_ROWS, :], p, preferred_element_type=F32)
            acc_scr[t] = alpha * acc_scr[t] + pv
            m_scr[t] = m_new
        return 0

    lax.fori_loop(0, n_blocks, body, 0)
    for g in range(n_qt):
        acc = acc_scr[g * N_Q_HEADS:(g + 1) * N_Q_HEADS]
        out = acc[:, :HEAD_DIM] / acc[:, HEAD_DIM:HEAD_DIM + 1]
        o_ref[g * tq:(g + 1) * tq, :] = out.reshape(ATTN_WIDTH, tq).T.astype(BF16)


def _attn_call(qT, k, vT, n_batch, seq, q_tiles, cache=None):
    nq = seq // (Q_TILE * q_tiles)
    nkb = seq // KEY_BLOCK
    n_tasks = q_tiles * N_Q_HEADS
    v_rows = N_KV_HEADS * V_ROWS
    in_specs = [pl.BlockSpec((q_tiles, ATTN_WIDTH, Q_TILE), lambda b, i: (b * nq + i, 0, 0)),
                pl.BlockSpec((seq, KV_WIDTH), lambda b, i: (b, 0)),
                pl.BlockSpec((nkb, v_rows, KEY_BLOCK), lambda b, i: (b, 0, 0))]
    scratch = [pltpu.VMEM((n_tasks, KV_WIDTH, Q_TILE), BF16),
               pltpu.VMEM((n_tasks, 1, Q_TILE), F32),
               pltpu.VMEM((n_tasks, V_ROWS, Q_TILE), F32),
               pltpu.VMEM((S_SLOTS, KEY_BLOCK, Q_TILE), F32)]
    args = [qT, k, vT]
    if cache is not None:
        ck, cvT = cache
        past = ck.shape[1]
        in_specs += [pl.BlockSpec((1, past, KV_WIDTH), lambda b, i: (b, 0, 0)),
                     pl.BlockSpec((1, past // KEY_BLOCK, v_rows, KEY_BLOCK), lambda b, i: (b, 0, 0, 0))]
        scratch += [pltpu.VMEM((seq + past, KV_WIDTH), BF16),
                    pltpu.VMEM((nkb + past // KEY_BLOCK, v_rows, KEY_BLOCK), BF16)]
        args += [ck, cvT]
        nkb += past // KEY_BLOCK
    return pl.pallas_call(
        functools.partial(_attn_kernel, n_blocks=nkb, has_cache=cache is not None),
        out_shape=jax.ShapeDtypeStruct((n_batch * seq, ATTN_WIDTH), BF16),
        grid=(n_batch, nq),
        in_specs=in_specs,
        out_specs=pl.BlockSpec((q_tiles * Q_TILE, ATTN_WIDTH), lambda b, i: (b * nq + i, 0)),
        scratch_shapes=scratch,
        compiler_params=pltpu.CompilerParams(dimension_semantics=("parallel", "parallel"),
                                             vmem_limit_bytes=VMEM_LIMIT),
        name="gqa_attention",
    )(*args)


def _log_sigmoid(x):
    return jnp.minimum(x, 0.0) - jnp.log1p(jnp.exp(-jnp.abs(x)))


def _cumsum_lanes(x, tri):
    hi = x.astype(BF16)
    r1 = x - hi.astype(F32)
    mid = r1.astype(BF16)
    lo = (r1 - mid.astype(F32)).astype(BF16)
    return (jnp.dot(hi, tri, preferred_element_type=F32) + jnp.dot(mid, tri, preferred_element_type=F32)
            + jnp.dot(lo, tri, preferred_element_type=F32))


def _tri_masks():
    ri = lax.broadcasted_iota(jnp.int32, (CHUNK, CHUNK), 0)
    ci = lax.broadcasted_iota(jnp.int32, (CHUNK, CHUNK), 1)
    return ri <= ci, ri >= ci


def _gate_rows(gs, d, le, ge):
    nh = MLSTM_HEADS
    r_in, r_fg = 2 * d * nh, (2 * d + 1) * nh
    lf = _log_sigmoid(jnp.concatenate([g[r_fg:r_fg + nh] for g in gs], axis=0))
    tri = (le if d == 0 else ge).astype(BF16)
    b = _cumsum_lanes(lf, tri)
    return [(lf[nh * j:nh * (j + 1)], b[nh * j:nh * (j + 1)], g[r_in:r_in + nh]) for j, g in enumerate(gs)]


def _scan_kernel(*refs, has_init, emit_state, shared):
    if shared:
        n_in = 3
        kf_ref, vf_ref, gf_ref = kb_ref, vb_ref, gb_ref = refs[:3]
    else:
        n_in = 6
        kf_ref, vf_ref, gf_ref, kb_ref, vb_ref, gb_ref = refs[:6]
    cfin_ref = refs[-1]
    outs = refs[n_in + (3 if has_init else 0):-1]
    cxf_ref, cxb_ref, mf_ref, mb_ref, mfin_ref = outs[:5]

    @pl.when(pl.program_id(1) == 0)
    def _():
        if has_init:
            sc_ref, sn_ref, m0_ref = refs[n_in:n_in + 3]
            for d in range(2):
                for h in range(MLSTM_HEADS):
                    cfin_ref[d, h, 0:MLSTM_DK, :] = sc_ref[0, 0, d, h].T
                    cfin_ref[d, h, MLSTM_DK:MV_ROWS, :] = jnp.broadcast_to(
                        sn_ref[0, 0, d, h:h + 1, :], (MV_ROWS - MLSTM_DK, MLSTM_DK))
            mfin_ref[...] = m0_ref[...]
        else:
            cfin_ref[...] = jnp.zeros(cfin_ref.shape, F32)
            mfin_ref[...] = jnp.zeros(mfin_ref.shape, F32)

    n_ch = gf_ref.shape[0]
    le, ge = _tri_masks()
    dirs = ((kf_ref, vf_ref, gf_ref, cxf_ref, mf_ref), (kb_ref, vb_ref, gb_ref, cxb_ref, mb_ref))
    gate = []
    for d, (_, _, g_ref, _, _) in enumerate(dirs):
        per_chunk = []
        for lf, b, ig in _gate_rows([g_ref[j] for j in range(n_ch)], d, le, ge):
            b_last = jnp.sum(lf, axis=1, keepdims=True)
            gt = b_last - b + ig
            a = jnp.max(gt, axis=1, keepdims=True)
            per_chunk.append((b_last, a, jnp.exp(gt - a)))
        gate.append(per_chunk)
    prods = []
    for d, (k_ref, v_ref, _, _, _) in enumerate(dirs):
        per_chunk = []
        for j in range(n_ch):
            wt = gate[d][j][2]
            us = []
            for h in range(MLSTM_HEADS):
                vw = (v_ref[j, h * MV_ROWS:(h + 1) * MV_ROWS, :].astype(F32) * wt[h:h + 1, :]).astype(BF16)
                kh = k_ref[j * CHUNK:(j + 1) * CHUNK, h * MLSTM_DK:(h + 1) * MLSTM_DK]
                us.append(jnp.dot(vw, kh, preferred_element_type=F32))
            per_chunk.append(us)
        prods.append(per_chunk)
    for d, (_, _, _, cx_ref, mo_ref) in enumerate(dirs):
        local = [gate[d][j][:2] + (prods[d][j],) for j in range(n_ch)]
        m_run = mfin_ref[0, d][:, 0:1]
        state = [cfin_ref[d, h] for h in range(MLSTM_HEADS)]
        for j in (range(n_ch) if d == 0 else reversed(range(n_ch))):
            b_last, a, us = local[j]
            m_new = jnp.maximum(a, b_last + m_run)
            decay = jnp.exp(b_last + m_run - m_new)
            gam = jnp.exp(a - m_new)
            mo_ref[0, j] = jnp.broadcast_to(m_run, (MLSTM_HEADS, CHUNK))
            for h in range(MLSTM_HEADS):
                cx_ref[0, j, h] = state[h].astype(BF16)
                state[h] = decay[h:h + 1, :] * state[h] + gam[h:h + 1, :] * us[h]
            m_run = m_new
        mfin_ref[0, d] = jnp.broadcast_to(m_run, (MLSTM_HEADS, CHUNK))
        for h in range(MLSTM_HEADS):
            cfin_ref[d, h] = state[h]

    if emit_state:
        c_out_ref, n_out_ref = outs[5:7]

        @pl.when(pl.program_id(1) == pl.num_programs(1) - 1)
        def _():
            for d in range(2):
                for h in range(MLSTM_HEADS):
                    c_out_ref[0, 0, d, h] = cfin_ref[d, h, 0:MLSTM_DK, :].T
                    n_out_ref[0, 0, d, h:h + 1, :] = cfin_ref[d, h, MLSTM_DK:MLSTM_DK + 1, :]


def _scan_call(mk, mvT, gates, init, n_batch, seq, emit_state):
    nc = seq // CHUNK
    g = _per_step(nc, SCAN_CHUNKS)
    ns = nc // g
    fwd3 = lambda b, c: (b * ns + c, 0, 0)
    bwd3 = lambda b, c: (b * ns + ns - 1 - c, 0, 0)
    fwd2 = lambda b, c: (b * ns + c, 0)
    bwd2 = lambda b, c: (b * ns + ns - 1 - c, 0)
    k_blk, v_blk, g_blk = (g * CHUNK, MLSTM_WIDTH), (g, MLSTM_HEADS * MV_ROWS, CHUNK), (g, N_GATES, CHUNK)
    cx_blk = (1, g, MLSTM_HEADS, MV_ROWS, MLSTM_DK)
    m_blk = (1, g, MLSTM_HEADS, CHUNK)
    sc_blk = (1, 1, 2, MLSTM_HEADS, MLSTM_DK, MLSTM_DK)
    sn_blk = (1, 1, 2, MLSTM_HEADS, MLSTM_DK)
    sm_blk = (1, 2, MLSTM_HEADS, CHUNK)
    per_request6 = lambda b, c: (b, 0, 0, 0, 0, 0)
    per_request5 = lambda b, c: (b, 0, 0, 0, 0)
    in_specs = [pl.BlockSpec(k_blk, fwd2), pl.BlockSpec(v_blk, fwd3), pl.BlockSpec(g_blk, fwd3)]
    args = [mk, mvT, gates]
    shared = ns == 1
    if not shared:
        in_specs += [pl.BlockSpec(k_blk, bwd2), pl.BlockSpec(v_blk, bwd3), pl.BlockSpec(g_blk, bwd3)]
        args += [mk, mvT, gates]
    if init is not None:
        in_specs += [pl.BlockSpec(sc_blk, per_request6), pl.BlockSpec(sn_blk, per_request5),
                     pl.BlockSpec(sm_blk, lambda b, c: (b, 0, 0, 0))]
        args += list(init)
    out_shape = [jax.ShapeDtypeStruct((n_batch, nc) + cx_blk[2:], BF16),
                 jax.ShapeDtypeStruct((n_batch, nc) + cx_blk[2:], BF16),
                 jax.ShapeDtypeStruct((n_batch, nc) + m_blk[2:], F32),
                 jax.ShapeDtypeStruct((n_batch, nc) + m_blk[2:], F32),
                 jax.ShapeDtypeStruct((n_batch,) + sm_blk[1:], F32)]
    out_specs = [pl.BlockSpec(cx_blk, lambda b, c: (b, c, 0, 0, 0)),
                 pl.BlockSpec(cx_blk, lambda b, c: (b, ns - 1 - c, 0, 0, 0)),
                 pl.BlockSpec(m_blk, lambda b, c: (b, c, 0, 0)),
                 pl.BlockSpec(m_blk, lambda b, c: (b, ns - 1 - c, 0, 0)),
                 pl.BlockSpec(sm_blk, lambda b, c: (b, 0, 0, 0))]
    if emit_state:
        out_shape += [jax.ShapeDtypeStruct((n_batch,) + sc_blk[1:], F32),
                      jax.ShapeDtypeStruct((n_batch,) + sn_blk[1:], F32)]
        out_specs += [pl.BlockSpec(sc_blk, per_request6), pl.BlockSpec(sn_blk, per_request5)]
    return pl.pallas_call(
        functools.partial(_scan_kernel, has_init=init is not None, emit_state=emit_state, shared=shared),
        out_shape=out_shape,
        grid=(n_batch, ns),
        in_specs=in_specs,
        out_specs=out_specs,
        scratch_shapes=[pltpu.VMEM((2, MLSTM_HEADS, MV_ROWS, MLSTM_DK), F32)],
        compiler_params=pltpu.CompilerParams(dimension_semantics=("parallel", "arbitrary"),
                                             vmem_limit_bytes=VMEM_LIMIT),
        name="mlstm_state_scan",
    )(*args)


def _mlstm_out_kernel(qT_ref, k_ref, vT_ref, g_ref, cxf_ref, cxb_ref, mf_ref, mb_ref, ogT_ref, nw_ref, o_ref):
    n_ch = g_ref.shape[0]
    nh = MLSTM_HEADS
    le, ge = _tri_masks()
    gs = [g_ref[j] for j in range(n_ch)]
    gate = [_gate_rows(gs, d, le, ge) for d in range(2)]
    kq = {}
    for j in range(n_ch):
        for h in range(nh):
            sl = slice(h * MLSTM_DK, (h + 1) * MLSTM_DK)
            kq[j, h] = jnp.dot(k_ref[j * CHUNK:(j + 1) * CHUNK, sl], qT_ref[j, sl, :],
                               preferred_element_type=F32)
    for j in range(n_ch):
        tok = slice(j * CHUNK, (j + 1) * CHUNK)
        r2, b2 = [], []
        for d in range(2):
            lf, b, ig = gate[d][j]
            r2.append((ig - b) * LOG2E)
            b2.append(b * LOG2E)
        r_cols = jnp.concatenate(r2, axis=0).T
        for h in range(nh):
            sl = slice(h * MLSTM_DK, (h + 1) * MLSTM_DK)
            qT = qT_ref[j, sl, :]
            vx = vT_ref[j, h * MV_ROWS:(h + 1) * MV_ROWS, :]
            sT = kq[j, h]
            h_sum = jnp.zeros((MLSTM_DK, CHUNK), F32)
            for d, (cx_ref, m_ref) in enumerate(((cxf_ref, mf_ref), (cxb_ref, mb_ref))):
                mask = le if d == 0 else ge
                i = d * nh + h
                rm = jnp.where(mask, jnp.broadcast_to(r_cols[:, i:i + 1], (CHUNK, CHUNK)), NEG_BIG)
                cm = jnp.max(rm, axis=0, keepdims=True)
                m_prev = m_ref[0, j, h:h + 1, :] * LOG2E
                mm = jnp.maximum(cm, m_prev)
                pT = (sT * jnp.exp2(rm - mm)).astype(BF16)
                qi = qT * jnp.exp2(m_prev - mm).astype(BF16)
                lhs = jnp.concatenate([vx, cx_ref[0, j, h]], axis=1)
                mix = jnp.dot(lhs, jnp.concatenate([pT, qi], axis=0), preferred_element_type=F32)
                den = mix[MLSTM_DK:MLSTM_DK + 1]
                clamp = jnp.exp2(-(b2[d][h:h + 1, :] + mm))
                h_sum = h_sum + mix[:MLSTM_DK] * (1.0 / jnp.maximum(jnp.abs(den), clamp))
            ms = jnp.mean(h_sum * h_sum, axis=0, keepdims=True)
            hn = h_sum * lax.rsqrt(ms + EPS) * nw_ref[sl, :] * ogT_ref[j, sl, :].astype(F32)
            o_ref[tok, sl] = hn.T.astype(BF16)


def _mlstm_out_call(mqT, mk, mvT, gates, cxf, cxb, mf, mb, ogT, nwb, n_batch, seq):
    g = _per_step(seq // CHUNK, OUT_CHUNKS)
    nc = seq // (CHUNK * g)
    tok = lambda b, c: (b * nc + c, 0)
    chk = lambda b, c: (b * nc + c, 0, 0)
    cx_blk = (1, g, MLSTM_HEADS, MV_ROWS, MLSTM_DK)
    m_blk = (1, g, MLSTM_HEADS, CHUNK)
    return pl.pallas_call(
        _mlstm_out_kernel,
        out_shape=jax.ShapeDtypeStruct((n_batch * seq, MLSTM_WIDTH), BF16),
        grid=(n_batch, nc),
        in_specs=[pl.BlockSpec((g, MLSTM_WIDTH, CHUNK), chk),
                  pl.BlockSpec((g * CHUNK, MLSTM_WIDTH), tok),
                  pl.BlockSpec((g, MLSTM_HEADS * MV_ROWS, CHUNK), chk),
                  pl.BlockSpec((g, N_GATES, CHUNK), chk),
                  pl.BlockSpec(cx_blk, lambda b, c: (b, c, 0, 0, 0)),
                  pl.BlockSpec(cx_blk, lambda b, c: (b, c, 0, 0, 0)),
                  pl.BlockSpec(m_blk, lambda b, c: (b, c, 0, 0)),
                  pl.BlockSpec(m_blk, lambda b, c: (b, c, 0, 0)),
                  pl.BlockSpec((g, MLSTM_WIDTH, CHUNK), chk),
                  pl.BlockSpec((MLSTM_WIDTH, CHUNK), lambda b, c: (0, 0))],
        out_specs=pl.BlockSpec((g * CHUNK, MLSTM_WIDTH), tok),
        compiler_params=pltpu.CompilerParams(dimension_semantics=("parallel", "parallel"),
                                             vmem_limit_bytes=VMEM_LIMIT),
        name="mlstm_chunk_out",
    )(mqT, mk, mvT, gates, cxf, cxb, mf, mb, ogT, nwb)


def _ffn_kernel(x_ref, a_ref, hm_ref, mods_ref, wo_ref, n2w_ref, wgu_ref, wd_ref, fw_ref, o_ref):
    g1 = mods_ref[0, 2:3, :]
    sh2 = mods_ref[0, 3:4, :]
    sc2 = mods_ref[0, 4:5, :]
    g2 = mods_ref[0, 5:6, :]
    subs = [slice(a, a + FFN_SUB) for a in range(0, x_ref.shape[0], FFN_SUB)]
    mix = [jnp.dot(a_ref[s, :], wo_ref[0:ATTN_WIDTH, :], preferred_element_type=F32)
           + jnp.dot(hm_ref[s, :], wo_ref[ATTN_WIDTH:D_MODEL, :], preferred_element_type=F32) for s in subs]
    x1 = [x_ref[s, :] + g1 * m for s, m in zip(subs, mix)]
    h2 = [(_rms(x) * n2w_ref[...] * (1.0 + sc2) + sh2).astype(BF16) for x in x1]
    acc = [jnp.zeros(x.shape, F32) for x in x1]
    for c0 in range(0, D_FF, FF_CHUNK):
        c1 = min(c0 + FF_CHUNK, D_FF)
        act = []
        for h in h2:
            gg = jnp.dot(h, wgu_ref[:, c0:c1], preferred_element_type=F32)
            uu = jnp.dot(h, wgu_ref[:, D_FF + c0:D_FF + c1], preferred_element_type=F32)
            act.append((gg * jax.nn.sigmoid(gg) * uu).astype(BF16))
        acc = [a + jnp.dot(t, wd_ref[c0:c1, :], preferred_element_type=F32) for a, t in zip(acc, act)]
    for s, x, a in zip(subs, x1, acc):
        o_ref[s, :] = _rms(x + g2 * a) * fw_ref[...]


def _ffn_call(x2d, attn_o, hm, mods, per_request, seq, wo, n2w, wgu, wd, fw):
    T = x2d.shape[0]
    tm = TOKEN_TILE
    mod_map = _mod_map(per_request, seq, tm)
    return pl.pallas_call(
        _ffn_kernel,
        out_shape=jax.ShapeDtypeStruct((T, D_MODEL), F32),
        grid=(T // tm,),
        in_specs=[pl.BlockSpec((tm, D_MODEL), lambda i: (i, 0)),
                  pl.BlockSpec((tm, ATTN_WIDTH), lambda i: (i, 0)),
                  pl.BlockSpec((tm, MLSTM_WIDTH), lambda i: (i, 0)),
                  pl.BlockSpec((1, 6, D_MODEL), mod_map),
                  _resident(wo.shape), _resident((1, D_MODEL)), _resident(wgu.shape),
                  _resident(wd.shape), _resident((1, D_MODEL))],
        out_specs=pl.BlockSpec((tm, D_MODEL), lambda i: (i, 0)),
        compiler_params=pltpu.CompilerParams(dimension_semantics=("parallel",),
                                             vmem_limit_bytes=VMEM_LIMIT),
        name="out_proj_ffn",
    )(x2d, attn_o, hm, mods, wo, n2w, wgu, wd, fw)


def _rope_tables(seq, gain, scale):
    pos = jnp.arange(seq, dtype=jnp.int32)
    row_ids = (pos // GRID_W).astype(F32)
    col_ids = (pos % GRID_W).astype(F32)
    inv = ROPE_THETA ** (-jnp.arange(0, AXIS_DIM, 2, dtype=F32) / AXIS_DIM)
    ang_r = inv[:, None] * row_ids[None, :]
    ang_c = inv[:, None] * col_ids[None, :]
    cos = jnp.concatenate([jnp.cos(ang_r)] * 2 + [jnp.cos(ang_c)] * 2, axis=0)
    sin = jnp.concatenate([-jnp.sin(ang_r), jnp.sin(ang_r), -jnp.sin(ang_c), jnp.sin(ang_c)], axis=0)
    g = gain.astype(F32) * scale
    g_sw = _swap_halves(g, axis=0)
    return g[:, None] * cos, g_sw[:, None] * sin


def _flat_tables(gain, scale, width):
    g = gain.astype(F32) * scale
    return jnp.broadcast_to(g[:, None], (HEAD_DIM, width)), jnp.zeros((HEAD_DIM, width), F32)


def _group(x, mods, per_request, seq, weights, tables, cache, init, emit_cache):
    n_batch = x.shape[0]
    x2d = x.reshape(n_batch * seq, D_MODEL)
    outs = _proj_call(x2d, mods, per_request, seq, weights["n1w"], weights["wt"], weights["wn"],
                      weights["gbias"], tables, emit_cache)
    qT, k, vT, mqT, mk, mvT, ogT, gates = outs[:8]
    q_tiles = _per_step(seq // Q_TILE, Q_TILES_PER_STEP)
    attn_o = _attn_call(qT, k, vT, n_batch, seq, q_tiles, cache)
    cxf, cxb, mf, mb, *final = _scan_call(mk, mvT, gates, init, n_batch, seq, emit_cache)
    hm = _mlstm_out_call(mqT, mk, mvT, gates, cxf, cxb, mf, mb, ogT, weights["mnw"], n_batch, seq)
    y = _ffn_call(x2d, attn_o, hm, mods, per_request, seq, weights["wo"], weights["n2w"], weights["wgu"],
                  weights["wd"], weights["fw"])
    return y.reshape(n_batch, seq, D_MODEL), outs[8:], final


def kernel(x_prompt, x_sample, cache_k, cache_v, state_C, state_n, state_m, c, c_ctx, w_ada, b_ada,
           norm1_w, w_in, gate_bias, q_norm_w, k_norm_w, mlstm_norm_w, w_out, norm2_w, w_gu, w_down,
           final_norm_w):
    depth = w_ada.shape[0]
    assert depth == 1, "single-layer trunk"
    n_ctx, ctx_len, _ = x_prompt.shape
    n_dec, dec_len, _ = x_sample.shape
    l = 0

    assert n_dec + 1 <= MOD_ROWS
    cond = jnp.zeros((MOD_ROWS, D_MODEL), F32).at[:n_dec].set(c).at[n_dec].set(c_ctx)
    mods = _mods_call(cond, w_ada[l], b_ada[l]).reshape(MOD_ROWS, 6, D_MODEL)

    wi = w_in[l]
    edges = [sum(_SPLITS[:i]) for i in range(1, len(_SPLITS))]
    aq, ak, av, mq_w, mk_w, mv_w, mo_w, mg_w = jnp.split(wi, edges, axis=1)
    weights = {
        "wt": jnp.concatenate([aq, ak, av, mq_w, mv_w, mo_w, mg_w], axis=1).T.astype(BF16),
        "wn": mk_w.astype(BF16),
        "gbias": gate_bias[l].reshape(N_GATES, 1).astype(F32),
        "n1w": norm1_w[l].reshape(1, D_MODEL),
        "n2w": norm2_w[l].reshape(1, D_MODEL),
        "mnw": jnp.broadcast_to(mlstm_norm_w[l].astype(F32)[:, None], (MLSTM_WIDTH, CHUNK)),
        "wo": w_out[l].astype(BF16),
        "wgu": w_gu[l].astype(BF16),
        "wd": w_down[l].astype(BF16),
        "fw": final_norm_w.reshape(1, D_MODEL),
    }
    q_scale = (HEAD_DIM ** -0.5) * LOG2E

    tabs_ctx = _flat_tables(q_norm_w[l], q_scale, PROJ_TILE) + _flat_tables(k_norm_w[l], 1.0, PROJ_TILE)
    y_prompt, (kc, vc), (mfin, new_state_C, new_state_n) = _group(
        x_prompt, mods[n_dec:n_dec + 1], False, ctx_len, weights, tabs_ctx, None, None, True)
    new_cache_k = kc.reshape(n_ctx, 1, ctx_len, N_KV_HEADS, HEAD_DIM)
    new_cache_v = vc.reshape(n_ctx, 1, ctx_len, N_KV_HEADS, HEAD_DIM)
    new_state_m = mfin[..., 0][:, None]

    tabs_dec = _rope_tables(dec_len, q_norm_w[l], q_scale) + _rope_tables(dec_len, k_norm_w[l], 1.0)
    past = cache_k.shape[2]
    ck = cache_k[:, l].reshape(n_dec, past, KV_WIDTH).astype(BF16)
    cvT = cache_v[:, l].reshape(n_dec, past // KEY_BLOCK, KEY_BLOCK, KV_WIDTH).transpose(0, 1, 3, 2)
    pad_ones = jnp.ones(cvT.shape[:2] + (V_ROWS - HEAD_DIM, KEY_BLOCK), cvT.dtype)
    cvT = jnp.concatenate([cvT[:, :, :HEAD_DIM], pad_ones, cvT[:, :, HEAD_DIM:], pad_ones], axis=2).astype(BF16)
    m0 = jnp.broadcast_to(state_m[:, l].astype(F32)[..., None], (n_dec, 2, MLSTM_HEADS, CHUNK))
    init = (state_C[:, l:l + 1].astype(F32), state_n[:, l:l + 1].astype(F32), m0)
    y_sample, _, _ = _group(x_sample, mods[:n_dec], True, dec_len, weights, tabs_dec, (ck, cvT), init, False)

    return (y_prompt, y_sample, new_cache_k, new_cache_v, new_state_C, new_state_n, new_state_m)
```

```python
import functools

import jax
import jax.numpy as jnp
from jax import lax
from jax.experimental import pallas as pl
from jax.experimental.pallas import tpu as pltpu

F32 = jnp.float32
BF16 = jnp.bfloat16

D_MODEL = 1024
HEAD_DIM = 64
N_Q_HEADS = 8
N_KV_HEADS = 2
KV_GROUP = N_Q_HEADS // N_KV_HEADS
ATTN_WIDTH = N_Q_HEADS * HEAD_DIM
KV_WIDTH = N_KV_HEADS * HEAD_DIM
AXIS_DIM = HEAD_DIM // 2
GRID_W = 64
ROPE_THETA = 10000.0
MLSTM_HEADS = 4
MLSTM_DK = 128
MLSTM_WIDTH = MLSTM_HEADS * MLSTM_DK
CHUNK = 128
D_FF = 2816
N_GATES = 4 * MLSTM_HEADS
EPS = 1e-6
LOG2E = 1.4426950408889634
NEG_BIG = -1e30

TOKEN_TILE = 512
FFN_SUB = 256
PROJ_TILE = 1024
PROJ_SUB = 256
Q_TILE = 256
KEY_BLOCK = 256
QK_LOOKAHEAD = 6
S_SLOTS = 8
Q_TILES_PER_STEP = 16
SCAN_CHUNKS = 32
OUT_CHUNKS = 4
FF_CHUNK = 1024
ADA_TILE = 1536
MOD_ROWS = 16
BF16_TILE_ROWS = 16
V_ROWS = HEAD_DIM + BF16_TILE_ROWS
MV_ROWS = MLSTM_DK + BF16_TILE_ROWS
VMEM_CAPACITY_V7X = 64 * 1024 * 1024
VMEM_LIMIT = VMEM_CAPACITY_V7X - 8 * 1024 * 1024

_SPLITS = (ATTN_WIDTH, KV_WIDTH, KV_WIDTH, MLSTM_WIDTH, MLSTM_WIDTH, MLSTM_WIDTH, MLSTM_WIDTH, N_GATES)
_R_Q = 0
_R_K = _R_Q + ATTN_WIDTH
_R_V = _R_K + KV_WIDTH
_R_MQ = _R_V + KV_WIDTH
_R_MV = _R_MQ + MLSTM_WIDTH
_R_MO = _R_MV + MLSTM_WIDTH
_R_G = _R_MO + MLSTM_WIDTH
_R_END = _R_G + N_GATES


def _per_step(n, cap):
    g = cap
    while n % g:
        g //= 2
    return g


def _resident(shape):
    nd = len(shape)
    return pl.BlockSpec(shape, lambda *_: (0,) * nd, pipeline_mode=pl.Buffered(1))


def _mods_kernel(cond_ref, w_ref, b_ref, o_ref):
    c = cond_ref[...]
    s = c * jax.nn.sigmoid(c)
    o_ref[...] = jnp.dot(s, w_ref[...], preferred_element_type=F32) + b_ref[...]


def _mods_call(cond, w_ada, b_ada):
    rows, n_out = cond.shape[0], w_ada.shape[1]
    return pl.pallas_call(
        _mods_kernel,
        out_shape=jax.ShapeDtypeStruct((rows, n_out), F32),
        grid=(n_out // ADA_TILE,),
        in_specs=[pl.BlockSpec((rows, D_MODEL), lambda j: (0, 0)),
                  pl.BlockSpec((D_MODEL, ADA_TILE), lambda j: (0, j)),
                  pl.BlockSpec((1, ADA_TILE), lambda j: (0, j))],
        out_specs=pl.BlockSpec((rows, ADA_TILE), lambda j: (0, j)),
        compiler_params=pltpu.CompilerParams(dimension_semantics=("parallel",),
                                             vmem_limit_bytes=VMEM_LIMIT),
        name="adaln_mods",
    )(cond, w_ada, b_ada.reshape(1, n_out))


def _swap_halves(x, axis):
    hw = AXIS_DIM // 2
    parts = [lax.slice_in_dim(x, s, s + hw, axis=axis) for s in range(0, HEAD_DIM, hw)]
    return jnp.concatenate([parts[1], parts[0], parts[3], parts[2]], axis=axis)


def _rms(x):
    return x * lax.rsqrt(jnp.mean(x * x, axis=-1, keepdims=True) + EPS)


def _proj_kernel(x_ref, mods_ref, n1w_ref, wt_ref, wn_ref, gb_ref, tqa_ref, tqb_ref, tka_ref, tkb_ref,
                 qT_ref, k_ref, vT_ref, mqT_ref, mk_ref, mvT_ref, ogT_ref, g_ref, *cache_refs):
    sh = mods_ref[0, 0:1, :]
    sc = mods_ref[0, 1:2, :]
    sub = PROJ_SUB
    for a in range(0, x_ref.shape[0], sub):
        tok = slice(a, a + sub)
        h = (_rms(x_ref[tok, :]) * n1w_ref[...] * (1.0 + sc) + sh).astype(BF16)

        yt = lax.dot_general(wt_ref[...], h, (((1,), (1,)), ((), ())), preferred_element_type=F32)

        n_qk = N_Q_HEADS + N_KV_HEADS
        qk = yt[_R_Q:_R_V].reshape(n_qk, HEAD_DIM, sub)
        qk = qk * lax.rsqrt(jnp.sum(qk * qk, axis=1, keepdims=True) * (1.0 / HEAD_DIM) + EPS)
        sw = _swap_halves(qk, axis=1)
        q = qk[:N_Q_HEADS] * tqa_ref[:, tok][None] + sw[:N_Q_HEADS] * tqb_ref[:, tok][None]
        q = q.reshape(ATTN_WIDTH, sub).astype(BF16)
        for j in range(sub // Q_TILE):
            qT_ref[a // Q_TILE + j] = q[:, j * Q_TILE:(j + 1) * Q_TILE]
        k = qk[N_Q_HEADS:] * tka_ref[:, tok][None] + sw[N_Q_HEADS:] * tkb_ref[:, tok][None]
        kt = k.reshape(KV_WIDTH, sub).T
        k_ref[tok, :] = kt.astype(BF16)

        vt = yt[_R_V:_R_MQ]
        ones = jnp.ones((V_ROWS - HEAD_DIM, sub), F32)
        vtb = jnp.concatenate([vt[:HEAD_DIM], ones, vt[HEAD_DIM:], ones], axis=0).astype(BF16)
        for j in range(sub // KEY_BLOCK):
            vT_ref[a // KEY_BLOCK + j] = vtb[:, j * KEY_BLOCK:(j + 1) * KEY_BLOCK]
        if cache_refs:
            kc_ref, vc_ref = cache_refs
            kc_ref[tok, :] = kt
            vc_ref[tok, :] = vt.T

        mq = yt[_R_MQ:_R_MV].astype(BF16)
        mv = yt[_R_MV:_R_MO]
        pad = jnp.ones((MV_ROWS - MLSTM_DK, sub), F32)
        mvx = jnp.concatenate(
            [blk for hh in range(MLSTM_HEADS) for blk in (mv[hh * MLSTM_DK:(hh + 1) * MLSTM_DK], pad)],
            axis=0).astype(BF16)
        og = jax.nn.sigmoid(yt[_R_MO:_R_G]).astype(BF16)
        gates = yt[_R_G:_R_END] + gb_ref[...]
        for j in range(sub // CHUNK):
            cs = slice(j * CHUNK, (j + 1) * CHUNK)
            mqT_ref[a // CHUNK + j] = mq[:, cs]
            mvT_ref[a // CHUNK + j] = mvx[:, cs]
            ogT_ref[a // CHUNK + j] = og[:, cs]
            g_ref[a // CHUNK + j] = gates[:, cs]

        mk = jnp.dot(h, wn_ref[...], preferred_element_type=F32)
        mk_ref[tok, :] = (mk * (MLSTM_DK ** -0.5)).astype(BF16)


def _mod_map(per_request, seq, tm):
    if not per_request:
        return lambda i: (0, 0, 0)
    tiles = seq // tm
    return lambda i: (i // tiles, 0, 0)


def _proj_call(x2d, mods, per_request, seq, n1w, wt, wn, gbias, tables, emit_cache):
    T = x2d.shape[0]
    tm = PROJ_TILE
    n_tiles = T // tm
    mod_map = _mod_map(per_request, seq, tm)
    if per_request:
        tiles = seq // tm
        tab_map = lambda i: (0, i % tiles)
    else:
        tab_map = lambda i: (0, 0)
    tab_spec = pl.BlockSpec((HEAD_DIM, tm), tab_map)
    out_shape = [
        jax.ShapeDtypeStruct((T // Q_TILE, ATTN_WIDTH, Q_TILE), BF16),
        jax.ShapeDtypeStruct((T, KV_WIDTH), BF16),
        jax.ShapeDtypeStruct((T // KEY_BLOCK, N_KV_HEADS * V_ROWS, KEY_BLOCK), BF16),
        jax.ShapeDtypeStruct((T // CHUNK, MLSTM_WIDTH, CHUNK), BF16),
        jax.ShapeDtypeStruct((T, MLSTM_WIDTH), BF16),
        jax.ShapeDtypeStruct((T // CHUNK, MLSTM_HEADS * MV_ROWS, CHUNK), BF16),
        jax.ShapeDtypeStruct((T // CHUNK, MLSTM_WIDTH, CHUNK), BF16),
        jax.ShapeDtypeStruct((T // CHUNK, N_GATES, CHUNK), F32),
    ]
    out_specs = [
        pl.BlockSpec((tm // Q_TILE, ATTN_WIDTH, Q_TILE), lambda i: (i, 0, 0)),
        pl.BlockSpec((tm, KV_WIDTH), lambda i: (i, 0)),
        pl.BlockSpec((tm // KEY_BLOCK, N_KV_HEADS * V_ROWS, KEY_BLOCK), lambda i: (i, 0, 0)),
        pl.BlockSpec((tm // CHUNK, MLSTM_WIDTH, CHUNK), lambda i: (i, 0, 0)),
        pl.BlockSpec((tm, MLSTM_WIDTH), lambda i: (i, 0)),
        pl.BlockSpec((tm // CHUNK, MLSTM_HEADS * MV_ROWS, CHUNK), lambda i: (i, 0, 0)),
        pl.BlockSpec((tm // CHUNK, MLSTM_WIDTH, CHUNK), lambda i: (i, 0, 0)),
        pl.BlockSpec((tm // CHUNK, N_GATES, CHUNK), lambda i: (i, 0, 0)),
    ]
    if emit_cache:
        out_shape += [jax.ShapeDtypeStruct((T, KV_WIDTH), F32)] * 2
        out_specs += [pl.BlockSpec((tm, KV_WIDTH), lambda i: (i, 0))] * 2
    return pl.pallas_call(
        _proj_kernel,
        out_shape=out_shape,
        grid=(n_tiles,),
        in_specs=[pl.BlockSpec((tm, D_MODEL), lambda i: (i, 0)),
                  pl.BlockSpec((1, 6, D_MODEL), mod_map),
                  _resident((1, D_MODEL)),
                  _resident(wt.shape),
                  _resident(wn.shape),
                  _resident((N_GATES, 1)),
                  tab_spec, tab_spec, tab_spec, tab_spec],
        out_specs=out_specs,
        compiler_params=pltpu.CompilerParams(dimension_semantics=("parallel",),
                                             vmem_limit_bytes=VMEM_LIMIT),
        name="norm1_in_proj",
    )(x2d, mods, n1w, wt, wn, gbias, *tables)


def _attn_kernel(*refs, n_blocks, has_cache):
    if has_cache:
        qT_ref, k_new_ref, vT_new_ref, ck_ref, cvT_ref, o_ref, qp_scr, m_scr, acc_scr, s_scr, k_ref, vT_ref = refs
        n_new = k_new_ref.shape[0]
        k_ref[0:n_new] = k_new_ref[...]
        k_ref[n_new:] = ck_ref[0]
        vT_ref[0:n_new // KEY_BLOCK] = vT_new_ref[...]
        vT_ref[n_new // KEY_BLOCK:] = cvT_ref[0]
    else:
        qT_ref, k_ref, vT_ref, o_ref, qp_scr, m_scr, acc_scr, s_scr = refs
    n_qt, _, tq = qT_ref.shape
    n_tasks = n_qt * N_Q_HEADS

    for t in range(n_tasks):
        g, hd = divmod(t, N_Q_HEADS)
        qg = qT_ref[g, hd * HEAD_DIM:(hd + 1) * HEAD_DIM, :]
        zero = jnp.zeros_like(qg)
        qp_scr[t] = jnp.concatenate([qg, zero] if hd < KV_GROUP else [zero, qg], axis=0)
    m_scr[...] = jnp.full(m_scr.shape, NEG_BIG, F32)
    acc_scr[...] = jnp.zeros(acc_scr.shape, F32)

    def scores(c, t):
        off = pl.multiple_of(c * KEY_BLOCK, KEY_BLOCK)
        return jnp.dot(k_ref[pl.ds(off, KEY_BLOCK), :], qp_scr[t], preferred_element_type=F32)

    assert n_tasks % S_SLOTS == 0 and QK_LOOKAHEAD < S_SLOTS
    for t in range(QK_LOOKAHEAD):
        s_scr[t] = scores(0, t)

    def body(c, _):
        c_next = jnp.minimum(c + 1, n_blocks - 1)
        for t in range(n_tasks):
            s = s_scr[t % S_SLOTS]
            ahead = t + QK_LOOKAHEAD
            if ahead < n_tasks:
                s_scr[ahead % S_SLOTS] = scores(c, ahead)
            else:
                s_scr[ahead % S_SLOTS] = scores(c_next, ahead - n_tasks)
            r0 = ((t % N_Q_HEADS) // KV_GROUP) * V_ROWS
            m = m_scr[t]
            m_new = jnp.maximum(m, jnp.max(s, axis=0, keepdims=True))
            p = jnp.exp2(s - m_new).astype(BF16)
            alpha = jnp.exp2(m - m_new)
            pv = jnp.dot(vT_ref[c, r0:r0 + V_ROWS, :], p, preferred_element_type=F32)
            acc_scr[t] = alpha * acc_scr[t] + pv
            m_scr[t] = m_new
        return 0

    lax.fori_loop(0, n_blocks, body, 0)
    for g in range(n_qt):
        acc = acc_scr[g * N_Q_HEADS:(g + 1) * N_Q_HEADS]
        out = acc[:, :HEAD_DIM] / acc[:, HEAD_DIM:HEAD_DIM + 1]
        o_ref[g * tq:(g + 1) * tq, :] = out.reshape(ATTN_WIDTH, tq).T.astype(BF16)


def _attn_call(qT, k, vT, n_batch, seq, q_tiles, cache=None):
    nq = seq // (Q_TILE * q_tiles)
    nkb = seq // KEY_BLOCK
    n_tasks = q_tiles * N_Q_HEADS
    v_rows = N_KV_HEADS * V_ROWS
    in_specs = [pl.BlockSpec((q_tiles, ATTN_WIDTH, Q_TILE), lambda b, i: (b * nq + i, 0, 0)),
                pl.BlockSpec((seq, KV_WIDTH), lambda b, i: (b, 0)),
                pl.BlockSpec((nkb, v_rows, KEY_BLOCK), lambda b, i: (b, 0, 0))]
    scratch = [pltpu.VMEM((n_tasks, KV_WIDTH, Q_TILE), BF16),
               pltpu.VMEM((n_tasks, 1, Q_TILE), F32),
               pltpu.VMEM((n_tasks, V_ROWS, Q_TILE), F32),
               pltpu.VMEM((S_SLOTS, KEY_BLOCK, Q_TILE), F32)]
    args = [qT, k, vT]
    if cache is not None:
        ck, cvT = cache
        past = ck.shape[1]
        in_specs += [pl.BlockSpec((1, past, KV_WIDTH), lambda b, i: (b, 0, 0)),
                     pl.BlockSpec((1, past // KEY_BLOCK, v_rows, KEY_BLOCK), lambda b, i: (b, 0, 0, 0))]
        scratch += [pltpu.VMEM((seq + past, KV_WIDTH), BF16),
                    pltpu.VMEM((nkb + past // KEY_BLOCK, v_rows, KEY_BLOCK), BF16)]
        args += [ck, cvT]
        nkb += past // KEY_BLOCK
    return pl.pallas_call(
        functools.partial(_attn_kernel, n_blocks=nkb, has_cache=cache is not None),
        out_shape=jax.ShapeDtypeStruct((n_batch * seq, ATTN_WIDTH), BF16),
        grid=(n_batch, nq),
        in_specs=in_specs,
        out_specs=pl.BlockSpec((q_tiles * Q_TILE, ATTN_WIDTH), lambda b, i: (b * nq + i, 0)),
        scratch_shapes=scratch,
        compiler_params=pltpu.CompilerParams(dimension_semantics=("parallel", "parallel"),
                                             vmem_limit_bytes=VMEM_LIMIT),
        name="gqa_attention",
    )(*args)


def _log_sigmoid(x):
    return jnp.minimum(x, 0.0) - jnp.log1p(jnp.exp(-jnp.abs(x)))


def _cumsum_lanes(x, tri):
    hi = x.astype(BF16)
    r1 = x - hi.astype(F32)
    mid = r1.astype(BF16)
    lo = (r1 - mid.astype(F32)).astype(BF16)
    return (jnp.dot(hi, tri, preferred_element_type=F32) + jnp.dot(mid, tri, preferred_element_type=F32)
            + jnp.dot(lo, tri, preferred_element_type=F32))


def _tri_masks():
    ri = lax.broadcasted_iota(jnp.int32, (CHUNK, CHUNK), 0)
    ci = lax.broadcasted_iota(jnp.int32, (CHUNK, CHUNK), 1)
    return ri <= ci, ri >= ci


def _gate_rows(gs, d, le, ge):
    nh = MLSTM_HEADS
    r_in, r_fg = 2 * d * nh, (2 * d + 1) * nh
    lf = _log_sigmoid(jnp.concatenate([g[r_fg:r_fg + nh] for g in gs], axis=0))
    tri = (le if d == 0 else ge).astype(BF16)
    b = _cumsum_lanes(lf, tri)
    return [(lf[nh * j:nh * (j + 1)], b[nh * j:nh * (j + 1)], g[r_in:r_in + nh]) for j, g in enumerate(gs)]


def _scan_kernel(*refs, has_init, emit_state, shared):
    if shared:
        n_in = 3
        kf_ref, vf_ref, gf_ref = kb_ref, vb_ref, gb_ref = refs[:3]
    else:
        n_in = 6
        kf_ref, vf_ref, gf_ref, kb_ref, vb_ref, gb_ref = refs[:6]
    cfin_ref = refs[-1]
    outs = refs[n_in + (3 if has_init else 0):-1]
    cxf_ref, cxb_ref, mf_ref, mb_ref, mfin_ref = outs[:5]

    @pl.when(pl.program_id(1) == 0)
    def _():
        if has_init:
            sc_ref, sn_ref, m0_ref = refs[n_in:n_in + 3]
            for d in range(2):
                for h in range(MLSTM_HEADS):
                    cfin_ref[d, h, 0:MLSTM_DK, :] = sc_ref[0, 0, d, h].T
                    cfin_ref[d, h, MLSTM_DK:MV_ROWS, :] = jnp.broadcast_to(
                        sn_ref[0, 0, d, h:h + 1, :], (MV_ROWS - MLSTM_DK, MLSTM_DK))
            mfin_ref[...] = m0_ref[...]
        else:
            cfin_ref[...] = jnp.zeros(cfin_ref.shape, F32)
            mfin_ref[...] = jnp.zeros(mfin_ref.shape, F32)

    n_ch = gf_ref.shape[0]
    le, ge = _tri_masks()
    dirs = ((kf_ref, vf_ref, gf_ref, cxf_ref, mf_ref), (kb_ref, vb_ref, gb_ref, cxb_ref, mb_ref))
    gate = []
    for d, (_, _, g_ref, _, _) in enumerate(dirs):
        per_chunk = []
        for lf, b, ig in _gate_rows([g_ref[j] for j in range(n_ch)], d, le, ge):
            b_last = jnp.sum(lf, axis=1, keepdims=True)
            gt = b_last - b + ig
            a = jnp.max(gt, axis=1, keepdims=True)
            per_chunk.append((b_last, a, jnp.exp(gt - a)))
        gate.append(per_chunk)
    prods = []
    for d, (k_ref, v_ref, _, _, _) in enumerate(dirs):
        per_chunk = []
        for j in range(n_ch):
            wt = gate[d][j][2]
            us = []
            for h in range(MLSTM_HEADS):
                vw = (v_ref[j, h * MV_ROWS:(h + 1) * MV_ROWS, :].astype(F32) * wt[h:h + 1, :]).astype(BF16)
                kh = k_ref[j * CHUNK:(j + 1) * CHUNK, h * MLSTM_DK:(h + 1) * MLSTM_DK]
                us.append(jnp.dot(vw, kh, preferred_element_type=F32))
            per_chunk.append(us)
        prods.append(per_chunk)
    for d, (_, _, _, cx_ref, mo_ref) in enumerate(dirs):
        local = [gate[d][j][:2] + (prods[d][j],) for j in range(n_ch)]
        m_run = mfin_ref[0, d][:, 0:1]
        state = [cfin_ref[d, h] for h in range(MLSTM_HEADS)]
        for j in (range(n_ch) if d == 0 else reversed(range(n_ch))):
            b_last, a, us = local[j]
            m_new = jnp.maximum(a, b_last + m_run)
            decay = jnp.exp(b_last + m_run - m_new)
            gam = jnp.exp(a - m_new)
            mo_ref[0, j] = jnp.broadcast_to(m_run, (MLSTM_HEADS, CHUNK))
            for h in range(MLSTM_HEADS):
                cx_ref[0, j, h] = state[h].astype(BF16)
                state[h] = decay[h:h + 1, :] * state[h] + gam[h:h + 1, :] * us[h]
            m_run = m_new
        mfin_ref[0, d] = jnp.broadcast_to(m_run, (MLSTM_HEADS, CHUNK))
        for h in range(MLSTM_HEADS):
            cfin_ref[d, h] = state[h]

    if emit_state:
        c_out_ref, n_out_ref = outs[5:7]

        @pl.when(pl.program_id(1) == pl.num_programs(1) - 1)
        def _():
            for d in range(2):
                for h in range(MLSTM_HEADS):
                    c_out_ref[0, 0, d, h] = cfin_ref[d, h, 0:MLSTM_DK, :].T
                    n_out_ref[0, 0, d, h:h + 1, :] = cfin_ref[d, h, MLSTM_DK:MLSTM_DK + 1, :]


def _scan_call(mk, mvT, gates, init, n_batch, seq, emit_state):
    nc = seq // CHUNK
    g = _per_step(nc, SCAN_CHUNKS)
    ns = nc // g
    fwd3 = lambda b, c: (b * ns + c, 0, 0)
    bwd3 = lambda b, c: (b * ns + ns - 1 - c, 0, 0)
    fwd2 = lambda b, c: (b * ns + c, 0)
    bwd2 = lambda b, c: (b * ns + ns - 1 - c, 0)
    k_blk, v_blk, g_blk = (g * CHUNK, MLSTM_WIDTH), (g, MLSTM_HEADS * MV_ROWS, CHUNK), (g, N_GATES, CHUNK)
    cx_blk = (1, g, MLSTM_HEADS, MV_ROWS, MLSTM_DK)
    m_blk = (1, g, MLSTM_HEADS, CHUNK)
    sc_blk = (1, 1, 2, MLSTM_HEADS, MLSTM_DK, MLSTM_DK)
    sn_blk = (1, 1, 2, MLSTM_HEADS, MLSTM_DK)
    sm_blk = (1, 2, MLSTM_HEADS, CHUNK)
    per_request6 = lambda b, c: (b, 0, 0, 0, 0, 0)
    per_request5 = lambda b, c: (b, 0, 0, 0, 0)
    in_specs = [pl.BlockSpec(k_blk, fwd2), pl.BlockSpec(v_blk, fwd3), pl.BlockSpec(g_blk, fwd3)]
    args = [mk, mvT, gates]
    shared = ns == 1
    if not shared:
        in_specs += [pl.BlockSpec(k_blk, bwd2), pl.BlockSpec(v_blk, bwd3), pl.BlockSpec(g_blk, bwd3)]
        args += [mk, mvT, gates]
    if init is not None:
        in_specs += [pl.BlockSpec(sc_blk, per_request6), pl.BlockSpec(sn_blk, per_request5),
                     pl.BlockSpec(sm_blk, lambda b, c: (b, 0, 0, 0))]
        args += list(init)
    out_shape = [jax.ShapeDtypeStruct((n_batch, nc) + cx_blk[2:], BF16),
                 jax.ShapeDtypeStruct((n_batch, nc) + cx_blk[2:], BF16),
                 jax.ShapeDtypeStruct((n_batch, nc) + m_blk[2:], F32),
                 jax.ShapeDtypeStruct((n_batch, nc) + m_blk[2:], F32),
                 jax.ShapeDtypeStruct((n_batch,) + sm_blk[1:], F32)]
    out_specs = [pl.BlockSpec(cx_blk, lambda b, c: (b, c, 0, 0, 0)),
                 pl.BlockSpec(cx_blk, lambda b, c: (b, ns - 1 - c, 0, 0, 0)),
                 pl.BlockSpec(m_blk, lambda b, c: (b, c, 0, 0)),
                 pl.BlockSpec(m_blk, lambda b, c: (b, ns - 1 - c, 0, 0)),
                 pl.BlockSpec(sm_blk, lambda b, c: (b, 0, 0, 0))]
    if emit_state:
        out_shape += [jax.ShapeDtypeStruct((n_batch,) + sc_blk[1:], F32),
                      jax.ShapeDtypeStruct((n_batch,) + sn_blk[1:], F32)]
        out_specs += [pl.BlockSpec(sc_blk, per_request6), pl.BlockSpec(sn_blk, per_request5)]
    return pl.pallas_call(
        functools.partial(_scan_kernel, has_init=init is not None, emit_state=emit_state, shared=shared),
        out_shape=out_shape,
        grid=(n_batch, ns),
        in_specs=in_specs,
        out_specs=out_specs,
        scratch_shapes=[pltpu.VMEM((2, MLSTM_HEADS, MV_ROWS, MLSTM_DK), F32)],
        compiler_params=pltpu.CompilerParams(dimension_semantics=("parallel", "arbitrary"),
                                             vmem_limit_bytes=VMEM_LIMIT),
        name="mlstm_state_scan",
    )(*args)


def _mlstm_out_kernel(qT_ref, k_ref, vT_ref, g_ref, cxf_ref, cxb_ref, mf_ref, mb_ref, ogT_ref, nw_ref, o_ref):
    n_ch = g_ref.shape[0]
    nh = MLSTM_HEADS
    le, ge = _tri_masks()
    gs = [g_ref[j] for j in range(n_ch)]
    gate = [_gate_rows(gs, d, le, ge) for d in range(2)]
    kq = {}
    for j in range(n_ch):
        for h in range(nh):
            sl = slice(h * MLSTM_DK, (h + 1) * MLSTM_DK)
            kq[j, h] = jnp.dot(k_ref[j * CHUNK:(j + 1) * CHUNK, sl], qT_ref[j, sl, :],
                               preferred_element_type=F32)
    for j in range(n_ch):
        tok = slice(j * CHUNK, (j + 1) * CHUNK)
        r2, b2 = [], []
        for d in range(2):
            lf, b, ig = gate[d][j]
            r2.append((ig - b) * LOG2E)
            b2.append(b * LOG2E)
        r_cols = jnp.concatenate(r2, axis=0).T
        for h in range(nh):
            sl = slice(h * MLSTM_DK, (h + 1) * MLSTM_DK)
            qT = qT_ref[j, sl, :]
            vx = vT_ref[j, h * MV_ROWS:(h + 1) * MV_ROWS, :]
            sT = kq[j, h]
            h_sum = jnp.zeros((MLSTM_DK, CHUNK), F32)
            for d, (cx_ref, m_ref) in enumerate(((cxf_ref, mf_ref), (cxb_ref, mb_ref))):
                mask = le if d == 0 else ge
                i = d * nh + h
                rm = jnp.where(mask, jnp.broadcast_to(r_cols[:, i:i + 1], (CHUNK, CHUNK)), NEG_BIG)
                cm = jnp.max(rm, axis=0, keepdims=True)
                m_prev = m_ref[0, j, h:h + 1, :] * LOG2E
                mm = jnp.maximum(cm, m_prev)
                pT = (sT * jnp.exp2(rm - mm)).astype(BF16)
                qi = qT * jnp.exp2(m_prev - mm).astype(BF16)
                lhs = jnp.concatenate([vx, cx_ref[0, j, h]], axis=1)
                mix = jnp.dot(lhs, jnp.concatenate([pT, qi], axis=0), preferred_element_type=F32)
                den = mix[MLSTM_DK:MLSTM_DK + 1]
                clamp = jnp.exp2(-(b2[d][h:h + 1, :] + mm))
                h_sum = h_sum + mix[:MLSTM_DK] * (1.0 / jnp.maximum(jnp.abs(den), clamp))
            ms = jnp.mean(h_sum * h_sum, axis=0, keepdims=True)
            hn = h_sum * lax.rsqrt(ms + EPS) * nw_ref[sl, :] * ogT_ref[j, sl, :].astype(F32)
            o_ref[tok, sl] = hn.T.astype(BF16)


def _mlstm_out_call(mqT, mk, mvT, gates, cxf, cxb, mf, mb, ogT, nwb, n_batch, seq):
    g = _per_step(seq // CHUNK, OUT_CHUNKS)
    nc = seq // (CHUNK * g)
    tok = lambda b, c: (b * nc + c, 0)
    chk = lambda b, c: (b * nc + c, 0, 0)
    cx_blk = (1, g, MLSTM_HEADS, MV_ROWS, MLSTM_DK)
    m_blk = (1, g, MLSTM_HEADS, CHUNK)
    return pl.pallas_call(
        _mlstm_out_kernel,
        out_shape=jax.ShapeDtypeStruct((n_batch * seq, MLSTM_WIDTH), BF16),
        grid=(n_batch, nc),
        in_specs=[pl.BlockSpec((g, MLSTM_WIDTH, CHUNK), chk),
                  pl.BlockSpec((g * CHUNK, MLSTM_WIDTH), tok),
                  pl.BlockSpec((g, MLSTM_HEADS * MV_ROWS, CHUNK), chk),
                  pl.BlockSpec((g, N_GATES, CHUNK), chk),
                  pl.BlockSpec(cx_blk, lambda b, c: (b, c, 0, 0, 0)),
                  pl.BlockSpec(cx_blk, lambda b, c: (b, c, 0, 0, 0)),
                  pl.BlockSpec(m_blk, lambda b, c: (b, c, 0, 0)),
                  pl.BlockSpec(m_blk, lambda b, c: (b, c, 0, 0)),
                  pl.BlockSpec((g, MLSTM_WIDTH, CHUNK), chk),
                  pl.BlockSpec((MLSTM_WIDTH, CHUNK), lambda b, c: (0, 0))],
        out_specs=pl.BlockSpec((g * CHUNK, MLSTM_WIDTH), tok),
        compiler_params=pltpu.CompilerParams(dimension_semantics=("parallel", "parallel"),
                                             vmem_limit_bytes=VMEM_LIMIT),
        name="mlstm_chunk_out",
    )(mqT, mk, mvT, gates, cxf, cxb, mf, mb, ogT, nwb)


def _ffn_kernel(x_ref, a_ref, hm_ref, mods_ref, wo_ref, n2w_ref, wgu_ref, wd_ref, fw_ref, o_ref):
    g1 = mods_ref[0, 2:3, :]
    sh2 = mods_ref[0, 3:4, :]
    sc2 = mods_ref[0, 4:5, :]
    g2 = mods_ref[0, 5:6, :]
    subs = [slice(a, a + FFN_SUB) for a in range(0, x_ref.shape[0], FFN_SUB)]
    mix = [jnp.dot(a_ref[s, :], wo_ref[0:ATTN_WIDTH, :], preferred_element_type=F32)
           + jnp.dot(hm_ref[s, :], wo_ref[ATTN_WIDTH:D_MODEL, :], preferred_element_type=F32) for s in subs]
    x1 = [x_ref[s, :] + g1 * m for s, m in zip(subs, mix)]
    h2 = [(_rms(x) * n2w_ref[...] * (1.0 + sc2) + sh2).astype(BF16) for x in x1]
    acc = [jnp.zeros(x.shape, F32) for x in x1]
    for c0 in range(0, D_FF, FF_CHUNK):
        c1 = min(c0 + FF_CHUNK, D_FF)
        act = []
        for h in h2:
            gg = jnp.dot(h, wgu_ref[:, c0:c1], preferred_element_type=F32)
            uu = jnp.dot(h, wgu_ref[:, D_FF + c0:D_FF + c1], preferred_element_type=F32)
            act.append((gg * jax.nn.sigmoid(gg) * uu).astype(BF16))
        acc = [a + jnp.dot(t, wd_ref[c0:c1, :], preferred_element_type=F32) for a, t in zip(acc, act)]
    for s, x, a in zip(subs, x1, acc):
        o_ref[s, :] = _rms(x + g2 * a) * fw_ref[...]


def _ffn_call(x2d, attn_o, hm, mods, per_request, seq, wo, n2w, wgu, wd, fw):
    T = x2d.shape[0]
    tm = TOKEN_TILE
    mod_map = _mod_map(per_request, seq, tm)
    return pl.pallas_call(
        _ffn_kernel,
        out_shape=jax.ShapeDtypeStruct((T, D_MODEL), F32),
        grid=(T // tm,),
        in_specs=[pl.BlockSpec((tm, D_MODEL), lambda i: (i, 0)),
                  pl.BlockSpec((tm, ATTN_WIDTH), lambda i: (i, 0)),
                  pl.BlockSpec((tm, MLSTM_WIDTH), lambda i: (i, 0)),
                  pl.BlockSpec((1, 6, D_MODEL), mod_map),
                  _resident(wo.shape), _resident((1, D_MODEL)), _resident(wgu.shape),
                  _resident(wd.shape), _resident((1, D_MODEL))],
        out_specs=pl.BlockSpec((tm, D_MODEL), lambda i: (i, 0)),
        compiler_params=pltpu.CompilerParams(dimension_semantics=("parallel",),
                                             vmem_limit_bytes=VMEM_LIMIT),
        name="out_proj_ffn",
    )(x2d, attn_o, hm, mods, wo, n2w, wgu, wd, fw)


def _rope_tables(seq, gain, scale):
    pos = jnp.arange(seq, dtype=jnp.int32)
    row_ids = (pos // GRID_W).astype(F32)
    col_ids = (pos % GRID_W).astype(F32)
    inv = ROPE_THETA ** (-jnp.arange(0, AXIS_DIM, 2, dtype=F32) / AXIS_DIM)
    ang_r = inv[:, None] * row_ids[None, :]
    ang_c = inv[:, None] * col_ids[None, :]
    cos = jnp.concatenate([jnp.cos(ang_r)] * 2 + [jnp.cos(ang_c)] * 2, axis=0)
    sin = jnp.concatenate([-jnp.sin(ang_r), jnp.sin(ang_r), -jnp.sin(ang_c), jnp.sin(ang_c)], axis=0)
    g = gain.astype(F32) * scale
    g_sw = _swap_halves(g, axis=0)
    return g[:, None] * cos, g_sw[:, None] * sin


def _flat_tables(gain, scale, width):
    g = gain.astype(F32) * scale
    return jnp.broadcast_to(g[:, None], (HEAD_DIM, width)), jnp.zeros((HEAD_DIM, width), F32)


def _group(x, mods, per_request, seq, weights, tables, cache, init, emit_cache):
    n_batch = x.shape[0]
    x2d = x.reshape(n_batch * seq, D_MODEL)
    outs = _proj_call(x2d, mods, per_request, seq, weights["n1w"], weights["wt"], weights["wn"],
                      weights["gbias"], tables, emit_cache)
    qT, k, vT, mqT, mk, mvT, ogT, gates = outs[:8]
    q_tiles = _per_step(seq // Q_TILE, Q_TILES_PER_STEP)
    attn_o = _attn_call(qT, k, vT, n_batch, seq, q_tiles, cache)
    cxf, cxb, mf, mb, *final = _scan_call(mk, mvT, gates, init, n_batch, seq, emit_cache)
    hm = _mlstm_out_call(mqT, mk, mvT, gates, cxf, cxb, mf, mb, ogT, weights["mnw"], n_batch, seq)
    y = _ffn_call(x2d, attn_o, hm, mods, per_request, seq, weights["wo"], weights["n2w"], weights["wgu"],
                  weights["wd"], weights["fw"])
    return y.reshape(n_batch, seq, D_MODEL), outs[8:], final


def kernel(x_prompt, x_sample, cache_k, cache_v, state_C, state_n, state_m, c, c_ctx, w_ada, b_ada,
           norm1_w, w_in, gate_bias, q_norm_w, k_norm_w, mlstm_norm_w, w_out, norm2_w, w_gu, w_down,
           final_norm_w):
    depth = w_ada.shape[0]
    assert depth == 1, "single-layer trunk"
    n_ctx, ctx_len, _ = x_prompt.shape
    n_dec, dec_len, _ = x_sample.shape
    past = cache_k.shape[2]
    assert (n_ctx * ctx_len) % PROJ_TILE == 0 and dec_len % PROJ_TILE == 0, "token counts must fill whole tiles"
    assert ctx_len % KEY_BLOCK == 0 and past % KEY_BLOCK == 0 and dec_len % GRID_W == 0
    l = 0

    assert n_dec + 1 <= MOD_ROWS
    cond = jnp.zeros((MOD_ROWS, D_MODEL), F32).at[:n_dec].set(c).at[n_dec].set(c_ctx)
    mods = _mods_call(cond, w_ada[l], b_ada[l]).reshape(MOD_ROWS, 6, D_MODEL)

    wi = w_in[l]
    edges = [sum(_SPLITS[:i]) for i in range(1, len(_SPLITS))]
    aq, ak, av, mq_w, mk_w, mv_w, mo_w, mg_w = jnp.split(wi, edges, axis=1)
    weights = {
        "wt": jnp.concatenate([aq, ak, av, mq_w, mv_w, mo_w, mg_w], axis=1).T.astype(BF16),
        "wn": mk_w.astype(BF16),
        "gbias": gate_bias[l].reshape(N_GATES, 1).astype(F32),
        "n1w": norm1_w[l].reshape(1, D_MODEL),
        "n2w": norm2_w[l].reshape(1, D_MODEL),
        "mnw": jnp.broadcast_to(mlstm_norm_w[l].astype(F32)[:, None], (MLSTM_WIDTH, CHUNK)),
        "wo": w_out[l].astype(BF16),
        "wgu": w_gu[l].astype(BF16),
        "wd": w_down[l].astype(BF16),
        "fw": final_norm_w.reshape(1, D_MODEL),
    }
    q_scale = (HEAD_DIM ** -0.5) * LOG2E

    tabs_ctx = _flat_tables(q_norm_w[l], q_scale, PROJ_TILE) + _flat_tables(k_norm_w[l], 1.0, PROJ_TILE)
    y_prompt, (kc, vc), (mfin, new_state_C, new_state_n) = _group(
        x_prompt, mods[n_dec:n_dec + 1], False, ctx_len, weights, tabs_ctx, None, None, True)
    new_cache_k = kc.reshape(n_ctx, 1, ctx_len, N_KV_HEADS, HEAD_DIM)
    new_cache_v = vc.reshape(n_ctx, 1, ctx_len, N_KV_HEADS, HEAD_DIM)
    new_state_m = mfin[..., 0][:, None]

    tabs_dec = _rope_tables(dec_len, q_norm_w[l], q_scale) + _rope_tables(dec_len, k_norm_w[l], 1.0)
    ck =cache_k[:, l].reshape(n_dec, past, KV_WIDTH).astype(BF16)
    cvT = cache_v[:, l].reshape(n_dec, past // KEY_BLOCK, KEY_BLOCK, KV_WIDTH).transpose(0, 1, 3, 2)
    pad_ones = jnp.ones(cvT.shape[:2] + (V_ROWS - HEAD_DIM, KEY_BLOCK), cvT.dtype)
    cvT = jnp.concatenate([cvT[:, :, :HEAD_DIM], pad_ones, cvT[:, :, HEAD_DIM:], pad_ones], axis=2).astype(BF16)
    m0 = jnp.broadcast_to(state_m[:, l].astype(F32)[..., None], (n_dec, 2, MLSTM_HEADS, CHUNK))
    init = (state_C[:, l:l + 1].astype(F32), state_n[:, l:l + 1].astype(F32), m0)
    y_sample, _, _ = _group(x_sample, mods[:n_dec], True, dec_len, weights, tabs_dec, (ck, cvT), init, False)

    return (y_prompt, y_sample, new_cache_k, new_cache_v, new_state_C, new_state_n, new_state_m)
```

```python
import functools

import jax
import jax.numpy as jnp
from jax import lax
from jax.experimental import pallas as pl
from jax.experimental.pallas import tpu as pltpu

F32 = jnp.float32
BF16 = jnp.bfloat16

D_MODEL = 1024
HEAD_DIM = 64
N_Q_HEADS = 8
N_KV_HEADS = 2
KV_GROUP = N_Q_HEADS // N_KV_HEADS
ATTN_WIDTH = N_Q_HEADS * HEAD_DIM
KV_WIDTH = N_KV_HEADS * HEAD_DIM
AXIS_DIM = HEAD_DIM // 2
GRID_W = 64
ROPE_THETA = 10000.0
MLSTM_HEADS = 4
MLSTM_DK = 128
MLSTM_WIDTH = MLSTM_HEADS * MLSTM_DK
CHUNK = 128
D_FF = 2816
N_GATES = 4 * MLSTM_HEADS
EPS = 1e-6
LOG2E = 1.4426950408889634
NEG_BIG = -1e30

TOKEN_TILE = 1024
FFN_SUB = 256
PROJ_TILE = 1024
PROJ_SUB = 256
Q_TILE = 256
KEY_BLOCK = 256
QK_LOOKAHEAD = 6
S_SLOTS = 8
Q_TILES_PER_STEP = 16
SCAN_CHUNKS = 32
OUT_CHUNKS = 4
FF_CHUNK = 1024
ADA_TILE = 1536
MOD_ROWS = 16
BF16_TILE_ROWS = 16
V_ROWS = HEAD_DIM + BF16_TILE_ROWS
MV_ROWS = MLSTM_DK + BF16_TILE_ROWS
VMEM_CAPACITY_V7X = 64 * 1024 * 1024
VMEM_LIMIT = VMEM_CAPACITY_V7X - 8 * 1024 * 1024

_SPLITS = (ATTN_WIDTH, KV_WIDTH, KV_WIDTH, MLSTM_WIDTH, MLSTM_WIDTH, MLSTM_WIDTH, MLSTM_WIDTH, N_GATES)
_R_Q = 0
_R_K = _R_Q + ATTN_WIDTH
_R_V = _R_K + KV_WIDTH
_R_MQ = _R_V + KV_WIDTH
_R_MV = _R_MQ + MLSTM_WIDTH
_R_MO = _R_MV + MLSTM_WIDTH
_R_G = _R_MO + MLSTM_WIDTH
_R_END = _R_G + N_GATES


def _per_step(n, cap):
    g = cap
    while n % g:
        g //= 2
    return g


def _resident(shape):
    nd = len(shape)
    return pl.BlockSpec(shape, lambda *_: (0,) * nd, pipeline_mode=pl.Buffered(1))


def _mods_kernel(cond_ref, w_ref, b_ref, o_ref):
    c = cond_ref[...]
    s = c * jax.nn.sigmoid(c)
    o_ref[...] = jnp.dot(s, w_ref[...], preferred_element_type=F32) + b_ref[...]


def _mods_call(cond, w_ada, b_ada):
    rows, n_out = cond.shape[0], w_ada.shape[1]
    return pl.pallas_call(
        _mods_kernel,
        out_shape=jax.ShapeDtypeStruct((rows, n_out), F32),
        grid=(n_out // ADA_TILE,),
        in_specs=[pl.BlockSpec((rows, D_MODEL), lambda j: (0, 0)),
                  pl.BlockSpec((D_MODEL, ADA_TILE), lambda j: (0, j)),
                  pl.BlockSpec((1, ADA_TILE), lambda j: (0, j))],
        out_specs=pl.BlockSpec((rows, ADA_TILE), lambda j: (0, j)),
        compiler_params=pltpu.CompilerParams(dimension_semantics=("parallel",),
                                             vmem_limit_bytes=VMEM_LIMIT),
        name="adaln_mods",
    )(cond, w_ada, b_ada.reshape(1, n_out))


def _swap_halves(x, axis):
    hw = AXIS_DIM // 2
    parts = [lax.slice_in_dim(x, s, s + hw, axis=axis) for s in range(0, HEAD_DIM, hw)]
    return jnp.concatenate([parts[1], parts[0], parts[3], parts[2]], axis=axis)


def _rms(x):
    return x * lax.rsqrt(jnp.mean(x * x, axis=-1, keepdims=True) + EPS)


def _proj_kernel(x_ref, mods_ref, n1w_ref, wt_ref, wn_ref, gb_ref, tqa_ref, tqb_ref, tka_ref, tkb_ref,
                 qT_ref, k_ref, vT_ref, mqT_ref, mk_ref, mvT_ref, ogT_ref, g_ref, *cache_refs):
    sh = mods_ref[0, 0:1, :]
    sc = mods_ref[0, 1:2, :]
    sub = PROJ_SUB
    for a in range(0, x_ref.shape[0], sub):
        tok = slice(a, a + sub)
        h = (_rms(x_ref[tok, :]) * n1w_ref[...] * (1.0 + sc) + sh).astype(BF16)

        yt = lax.dot_general(wt_ref[...], h, (((1,), (1,)), ((), ())), preferred_element_type=F32)

        n_qk = N_Q_HEADS + N_KV_HEADS
        qk = yt[_R_Q:_R_V].reshape(n_qk, HEAD_DIM, sub)
        qk = qk * lax.rsqrt(jnp.sum(qk * qk, axis=1, keepdims=True) * (1.0 / HEAD_DIM) + EPS)
        sw = _swap_halves(qk, axis=1)
        q = qk[:N_Q_HEADS] * tqa_ref[:, tok][None] + sw[:N_Q_HEADS] * tqb_ref[:, tok][None]
        q = q.reshape(ATTN_WIDTH, sub).astype(BF16)
        for j in range(sub // Q_TILE):
            qT_ref[a // Q_TILE + j] = q[:, j * Q_TILE:(j + 1) * Q_TILE]
        k = qk[N_Q_HEADS:] * tka_ref[:, tok][None] + sw[N_Q_HEADS:] * tkb_ref[:, tok][None]
        kt = k.reshape(KV_WIDTH, sub).T
        k_ref[tok, :] = kt.astype(BF16)

        vt = yt[_R_V:_R_MQ]
        ones = jnp.ones((V_ROWS - HEAD_DIM, sub), F32)
        vtb = jnp.concatenate([vt[:HEAD_DIM], ones, vt[HEAD_DIM:], ones], axis=0).astype(BF16)
        for j in range(sub // KEY_BLOCK):
            vT_ref[a // KEY_BLOCK + j] = vtb[:, j * KEY_BLOCK:(j + 1) * KEY_BLOCK]
        if cache_refs:
            kc_ref, vc_ref = cache_refs
            kc_ref[tok, :] = kt
            vc_ref[tok, :] = vt.T

        mq = yt[_R_MQ:_R_MV].astype(BF16)
        mv = yt[_R_MV:_R_MO]
        pad = jnp.ones((MV_ROWS - MLSTM_DK, sub), F32)
        mvx = jnp.concatenate(
            [blk for hh in range(MLSTM_HEADS) for blk in (mv[hh * MLSTM_DK:(hh + 1) * MLSTM_DK], pad)],
            axis=0).astype(BF16)
        og = jax.nn.sigmoid(yt[_R_MO:_R_G]).astype(BF16)
        gates = yt[_R_G:_R_END] + gb_ref[...]
        for j in range(sub // CHUNK):
            cs = slice(j * CHUNK, (j + 1) * CHUNK)
            mqT_ref[a // CHUNK + j] = mq[:, cs]
            mvT_ref[a // CHUNK + j] = mvx[:, cs]
            ogT_ref[a // CHUNK + j] = og[:, cs]
            g_ref[a // CHUNK + j] = gates[:, cs]

        mk = jnp.dot(h, wn_ref[...], preferred_element_type=F32)
        mk_ref[tok, :] = (mk * (MLSTM_DK ** -0.5)).astype(BF16)


def _mod_map(per_request, seq, tm):
    if not per_request:
        return lambda i: (0, 0, 0)
    tiles = seq // tm
    return lambda i: (i // tiles, 0, 0)


def _proj_call(x2d, mods, per_request, seq, n1w, wt, wn, gbias, tables, emit_cache):
    T = x2d.shape[0]
    tm = PROJ_TILE
    n_tiles = T // tm
    mod_map = _mod_map(per_request, seq, tm)
    if per_request:
        tiles = seq // tm
        tab_map = lambda i: (0, i % tiles)
    else:
        tab_map = lambda i: (0, 0)
    tab_spec = pl.BlockSpec((HEAD_DIM, tm), tab_map)
    out_shape = [
        jax.ShapeDtypeStruct((T // Q_TILE, ATTN_WIDTH, Q_TILE), BF16),
        jax.ShapeDtypeStruct((T, KV_WIDTH), BF16),
        jax.ShapeDtypeStruct((T // KEY_BLOCK, N_KV_HEADS * V_ROWS, KEY_BLOCK), BF16),
        jax.ShapeDtypeStruct((T // CHUNK, MLSTM_WIDTH, CHUNK), BF16),
        jax.ShapeDtypeStruct((T, MLSTM_WIDTH), BF16),
        jax.ShapeDtypeStruct((T // CHUNK, MLSTM_HEADS * MV_ROWS, CHUNK), BF16),
        jax.ShapeDtypeStruct((T // CHUNK, MLSTM_WIDTH, CHUNK), BF16),
        jax.ShapeDtypeStruct((T // CHUNK, N_GATES, CHUNK), F32),
    ]
    out_specs = [
        pl.BlockSpec((tm // Q_TILE, ATTN_WIDTH, Q_TILE), lambda i: (i, 0, 0)),
        pl.BlockSpec((tm, KV_WIDTH), lambda i: (i, 0)),
        pl.BlockSpec((tm // KEY_BLOCK, N_KV_HEADS * V_ROWS, KEY_BLOCK), lambda i: (i, 0, 0)),
        pl.BlockSpec((tm // CHUNK, MLSTM_WIDTH, CHUNK), lambda i: (i, 0, 0)),
        pl.BlockSpec((tm, MLSTM_WIDTH), lambda i: (i, 0)),
        pl.BlockSpec((tm // CHUNK, MLSTM_HEADS * MV_ROWS, CHUNK), lambda i: (i, 0, 0)),
        pl.BlockSpec((tm // CHUNK, MLSTM_WIDTH, CHUNK), lambda i: (i, 0, 0)),
        pl.BlockSpec((tm // CHUNK, N_GATES, CHUNK), lambda i: (i, 0, 0)),
    ]
    if emit_cache:
        out_shape += [jax.ShapeDtypeStruct((T, KV_WIDTH), F32)] * 2
        out_specs += [pl.BlockSpec((tm, KV_WIDTH), lambda i: (i, 0))] * 2
    return pl.pallas_call(
        _proj_kernel,
        out_shape=out_shape,
        grid=(n_tiles,),
        in_specs=[pl.BlockSpec((tm, D_MODEL), lambda i: (i, 0)),
                  pl.BlockSpec((1, 6, D_MODEL), mod_map),
                  _resident((1, D_MODEL)),
                  _resident(wt.shape),
                  _resident(wn.shape),
                  _resident((N_GATES, 1)),
                  tab_spec, tab_spec, tab_spec, tab_spec],
        out_specs=out_specs,
        compiler_params=pltpu.CompilerParams(dimension_semantics=("parallel",),
                                             vmem_limit_bytes=VMEM_LIMIT),
        name="norm1_in_proj",
    )(x2d, mods, n1w, wt, wn, gbias, *tables)


def _attn_kernel(*refs, n_blocks, has_cache):
    if has_cache:
        qT_ref, k_new_ref, vT_new_ref, ck_ref, cvT_ref, o_ref, qp_scr, m_scr, acc_scr, s_scr, k_ref, vT_ref = refs
        n_new = k_new_ref.shape[0]
        k_ref[0:n_new] = k_new_ref[...]
        k_ref[n_new:] = ck_ref[0]
        vT_ref[0:n_new // KEY_BLOCK] = vT_new_ref[...]
        vT_ref[n_new // KEY_BLOCK:] = cvT_ref[0]
    else:
        qT_ref, k_ref, vT_ref, o_ref, qp_scr, m_scr, acc_scr, s_scr = refs
    n_qt, _, tq = qT_ref.shape
    n_tasks = n_qt * N_Q_HEADS

    for t in range(n_tasks):
        g, hd = divmod(t, N_Q_HEADS)
        qg = qT_ref[g, hd * HEAD_DIM:(hd + 1) * HEAD_DIM, :]
        zero = jnp.zeros_like(qg)
        qp_scr[t] = jnp.concatenate([qg, zero] if hd < KV_GROUP else [zero, qg], axis=0)
    m_scr[...] = jnp.full(m_scr.shape, NEG_BIG, F32)
    acc_scr[...] = jnp.zeros(acc_scr.shape, F32)

    def scores(c, t):
        off = pl.multiple_of(c * KEY_BLOCK, KEY_BLOCK)
        return jnp.dot(k_ref[pl.ds(off, KEY_BLOCK), :], qp_scr[t], preferred_element_type=F32)

    assert n_tasks % S_SLOTS == 0 and QK_LOOKAHEAD < S_SLOTS
    for t in range(QK_LOOKAHEAD):
        s_scr[t] = scores(0, t)

    def body(c, _):
        c_next = jnp.minimum(c + 1, n_blocks - 1)
        for t in range(n_tasks):
            s = s_scr[t % S_SLOTS]
            ahead = t + QK_LOOKAHEAD
            if ahead < n_tasks:
                s_scr[ahead % S_SLOTS] = scores(c, ahead)
            else:
                s_scr[ahead % S_SLOTS] = scores(c_next, ahead - n_tasks)
            r0 = ((t % N_Q_HEADS) // KV_GROUP) * V_ROWS
            m = m_scr[t]
            m_new = jnp.maximum(m, jnp.max(s, axis=0, keepdims=True))
            p = jnp.exp2(s - m_new).astype(BF16)
            alpha = jnp.exp2(m - m_new)
            pv = jnp.dot(vT_ref[c, r0:r0 + V_ROWS, :], p, preferred_element_type=F32)
            acc_scr[t] = alpha * acc_scr[t] + pv
            m_scr[t] = m_new
        return 0

    lax.fori_loop(0, n_blocks, body, 0)
    for g in range(n_qt):
        acc = acc_scr[g * N_Q_HEADS:(g + 1) * N_Q_HEADS]
        out = acc[:, :HEAD_DIM] / acc[:, HEAD_DIM:HEAD_DIM + 1]
        o_ref[g * tq:(g + 1) * tq, :] = out.reshape(ATTN_WIDTH, tq).T.astype(BF16)


def _attn_call(qT, k, vT, n_batch, seq, q_tiles, cache=None):
    nq = seq // (Q_TILE * q_tiles)
    nkb = seq // KEY_BLOCK
    n_tasks = q_tiles * N_Q_HEADS
    v_rows = N_KV_HEADS * V_ROWS
    in_specs = [pl.BlockSpec((q_tiles, ATTN_WIDTH, Q_TILE), lambda b, i: (b * nq + i, 0, 0)),
                pl.BlockSpec((seq, KV_WIDTH), lambda b, i: (b, 0)),
                pl.BlockSpec((nkb, v_rows, KEY_BLOCK), lambda b, i: (b, 0, 0))]
    scratch = [pltpu.VMEM((n_tasks, KV_WIDTH, Q_TILE), BF16),
               pltpu.VMEM((n_tasks, 1, Q_TILE), F32),
               pltpu.VMEM((n_tasks, V_ROWS, Q_TILE), F32),
               pltpu.VMEM((S_SLOTS, KEY_BLOCK, Q_TILE), F32)]
    args = [qT, k, vT]
    if cache is not None:
        ck, cvT = cache
        past = ck.shape[1]
        in_specs += [pl.BlockSpec((1, past, KV_WIDTH), lambda b, i: (b, 0, 0)),
                     pl.BlockSpec((1, past // KEY_BLOCK, v_rows, KEY_BLOCK), lambda b, i: (b, 0, 0, 0))]
        scratch += [pltpu.VMEM((seq + past, KV_WIDTH), BF16),
                    pltpu.VMEM((nkb + past // KEY_BLOCK, v_rows, KEY_BLOCK), BF16)]
        args += [ck, cvT]
        nkb += past // KEY_BLOCK
    return pl.pallas_call(
        functools.partial(_attn_kernel, n_blocks=nkb, has_cache=cache is not None),
        out_shape=jax.ShapeDtypeStruct((n_batch * seq, ATTN_WIDTH), BF16),
        grid=(n_batch, nq),
        in_specs=in_specs,
        out_specs=pl.BlockSpec((q_tiles * Q_TILE, ATTN_WIDTH), lambda b, i: (b * nq + i, 0)),
        scratch_shapes=scratch,
        compiler_params=pltpu.CompilerParams(dimension_semantics=("parallel", "parallel"),
                                             vmem_limit_bytes=VMEM_LIMIT),
        name="gqa_attention",
    )(*args)


def _log_sigmoid(x):
    return jnp.minimum(x, 0.0) - jnp.log1p(jnp.exp(-jnp.abs(x)))


def _cumsum_lanes(x, tri):
    hi = x.astype(BF16)
    r1 = x - hi.astype(F32)
    mid = r1.astype(BF16)
    lo = (r1 - mid.astype(F32)).astype(BF16)
    return (jnp.dot(hi, tri, preferred_element_type=F32) + jnp.dot(mid, tri, preferred_element_type=F32)
            + jnp.dot(lo, tri, preferred_element_type=F32))


def _tri_masks():
    ri = lax.broadcasted_iota(jnp.int32, (CHUNK, CHUNK), 0)
    ci = lax.broadcasted_iota(jnp.int32, (CHUNK, CHUNK), 1)
    return ri <= ci, ri >= ci


def _gate_rows(gs, d, le, ge):
    nh = MLSTM_HEADS
    r_in, r_fg = 2 * d * nh, (2 * d + 1) * nh
    lf = _log_sigmoid(jnp.concatenate([g[r_fg:r_fg + nh] for g in gs], axis=0))
    tri = (le if d == 0 else ge).astype(BF16)
    b = _cumsum_lanes(lf, tri)
    return [(lf[nh * j:nh * (j + 1)], b[nh * j:nh * (j + 1)], g[r_in:r_in + nh]) for j, g in enumerate(gs)]


def _scan_kernel(*refs, has_init, emit_state, shared):
    if shared:
        n_in = 3
        kf_ref, vf_ref, gf_ref = kb_ref, vb_ref, gb_ref = refs[:3]
    else:
        n_in = 6
        kf_ref, vf_ref, gf_ref, kb_ref, vb_ref, gb_ref = refs[:6]
    cfin_ref = refs[-1]
    outs = refs[n_in + (3 if has_init else 0):-1]
    cxf_ref, cxb_ref, mf_ref, mb_ref, mfin_ref = outs[:5]

    @pl.when(pl.program_id(1) == 0)
    def _():
        if has_init:
            sc_ref, sn_ref, m0_ref = refs[n_in:n_in + 3]
            for d in range(2):
                for h in range(MLSTM_HEADS):
                    cfin_ref[d, h, 0:MLSTM_DK, :] = sc_ref[0, 0, d, h].T
                    cfin_ref[d, h, MLSTM_DK:MV_ROWS, :] = jnp.broadcast_to(
                        sn_ref[0, 0, d, h:h + 1, :], (MV_ROWS - MLSTM_DK, MLSTM_DK))
            mfin_ref[...] = m0_ref[...]
        else:
            cfin_ref[...] = jnp.zeros(cfin_ref.shape, F32)
            mfin_ref[...] = jnp.zeros(mfin_ref.shape, F32)

    n_ch = gf_ref.shape[0]
    le, ge = _tri_masks()
    dirs = ((kf_ref, vf_ref, gf_ref, cxf_ref, mf_ref), (kb_ref, vb_ref, gb_ref, cxb_ref, mb_ref))
    gate = []
    for d, (_, _, g_ref, _, _) in enumerate(dirs):
        per_chunk = []
        for lf, b, ig in _gate_rows([g_ref[j] for j in range(n_ch)], d, le, ge):
            b_last = jnp.sum(lf, axis=1, keepdims=True)
            gt = b_last - b + ig
            a = jnp.max(gt, axis=1, keepdims=True)
            per_chunk.append((b_last, a, jnp.exp(gt - a)))
        gate.append(per_chunk)
    prods = []
    for d, (k_ref, v_ref, _, _, _) in enumerate(dirs):
        per_chunk = []
        for j in range(n_ch):
            wt = gate[d][j][2]
            us = []
            for h in range(MLSTM_HEADS):
                vw = (v_ref[j, h * MV_ROWS:(h + 1) * MV_ROWS, :].astype(F32) * wt[h:h + 1, :]).astype(BF16)
                kh = k_ref[j * CHUNK:(j + 1) * CHUNK, h * MLSTM_DK:(h + 1) * MLSTM_DK]
                us.append(jnp.dot(vw, kh, preferred_element_type=F32))
            per_chunk.append(us)
        prods.append(per_chunk)
    for d, (_, _, _, cx_ref, mo_ref) in enumerate(dirs):
        local = [gate[d][j][:2] + (prods[d][j],) for j in range(n_ch)]
        m_run = mfin_ref[0, d][:, 0:1]
        state = [cfin_ref[d, h] for h in range(MLSTM_HEADS)]
        for j in (range(n_ch) if d == 0 else reversed(range(n_ch))):
            b_last, a, us = local[j]
            m_new = jnp.maximum(a, b_last + m_run)
            decay = jnp.exp(b_last + m_run - m_new)
            gam = jnp.exp(a - m_new)
            mo_ref[0, j] = jnp.broadcast_to(m_run, (MLSTM_HEADS, CHUNK))
            for h in range(MLSTM_HEADS):
                cx_ref[0, j, h] = state[h].astype(BF16)
                state[h] = decay[h:h + 1, :] * state[h] + gam[h:h + 1, :] * us[h]
            m_run = m_new
        mfin_ref[0, d] = jnp.broadcast_to(m_run, (MLSTM_HEADS, CHUNK))
        for h in range(MLSTM_HEADS):
            cfin_ref[d, h] = state[h]

    if emit_state:
        c_out_ref, n_out_ref = outs[5:7]

        @pl.when(pl.program_id(1) == pl.num_programs(1) - 1)
        def _():
            for d in range(2):
                for h in range(MLSTM_HEADS):
                    c_out_ref[0, 0, d, h] = cfin_ref[d, h, 0:MLSTM_DK, :].T
                    n_out_ref[0, 0, d, h:h + 1, :] = cfin_ref[d, h, MLSTM_DK:MLSTM_DK + 1, :]


def _scan_call(mk, mvT, gates, init, n_batch, seq, emit_state):
    nc = seq // CHUNK
    g = _per_step(nc, SCAN_CHUNKS)
    ns = nc // g
    fwd3 = lambda b, c: (b * ns + c, 0, 0)
    bwd3 = lambda b, c: (b * ns + ns - 1 - c, 0, 0)
    fwd2 = lambda b, c: (b * ns + c, 0)
    bwd2 = lambda b, c: (b * ns + ns - 1 - c, 0)
    k_blk, v_blk, g_blk = (g * CHUNK, MLSTM_WIDTH), (g, MLSTM_HEADS * MV_ROWS, CHUNK), (g, N_GATES, CHUNK)
    cx_blk = (1, g, MLSTM_HEADS, MV_ROWS, MLSTM_DK)
    m_blk = (1, g, MLSTM_HEADS, CHUNK)
    sc_blk = (1, 1, 2, MLSTM_HEADS, MLSTM_DK, MLSTM_DK)
    sn_blk = (1, 1, 2, MLSTM_HEADS, MLSTM_DK)
    sm_blk = (1, 2, MLSTM_HEADS, CHUNK)
    per_request6 = lambda b, c: (b, 0, 0, 0, 0, 0)
    per_request5 = lambda b, c: (b, 0, 0, 0, 0)
    in_specs = [pl.BlockSpec(k_blk, fwd2), pl.BlockSpec(v_blk, fwd3), pl.BlockSpec(g_blk, fwd3)]
    args = [mk, mvT, gates]
    shared = ns == 1
    if not shared:
        in_specs += [pl.BlockSpec(k_blk, bwd2), pl.BlockSpec(v_blk, bwd3), pl.BlockSpec(g_blk, bwd3)]
        args += [mk, mvT, gates]
    if init is not None:
        in_specs += [pl.BlockSpec(sc_blk, per_request6), pl.BlockSpec(sn_blk, per_request5),
                     pl.BlockSpec(sm_blk, lambda b, c: (b, 0, 0, 0))]
        args += list(init)
    out_shape = [jax.ShapeDtypeStruct((n_batch, nc) + cx_blk[2:], BF16),
                 jax.ShapeDtypeStruct((n_batch, nc) + cx_blk[2:], BF16),
                 jax.ShapeDtypeStruct((n_batch, nc) + m_blk[2:], F32),
                 jax.ShapeDtypeStruct((n_batch, nc) + m_blk[2:], F32),
                 jax.ShapeDtypeStruct((n_batch,) + sm_blk[1:], F32)]
    out_specs = [pl.BlockSpec(cx_blk, lambda b, c: (b, c, 0, 0, 0)),
                 pl.BlockSpec(cx_blk, lambda b, c: (b, ns - 1 - c, 0, 0, 0)),
                 pl.BlockSpec(m_blk, lambda b, c: (b, c, 0, 0)),
                 pl.BlockSpec(m_blk, lambda b, c: (b, ns - 1 - c, 0, 0)),
                 pl.BlockSpec(sm_blk, lambda b, c: (b, 0, 0, 0))]
    if emit_state:
        out_shape += [jax.ShapeDtypeStruct((n_batch,) + sc_blk[1:], F32),
                      jax.ShapeDtypeStruct((n_batch,) + sn_blk[1:], F32)]
        out_specs += [pl.BlockSpec(sc_blk, per_request6), pl.BlockSpec(sn_blk, per_request5)]
    return pl.pallas_call(
        functools.partial(_scan_kernel, has_init=init is not None, emit_state=emit_state, shared=shared),
        out_shape=out_shape,
        grid=(n_batch, ns),
        in_specs=in_specs,
        out_specs=out_specs,
        scratch_shapes=[pltpu.VMEM((2, MLSTM_HEADS, MV_ROWS, MLSTM_DK), F32)],
        compiler_params=pltpu.CompilerParams(dimension_semantics=("parallel", "arbitrary"),
                                             vmem_limit_bytes=VMEM_LIMIT),
        name="mlstm_state_scan",
    )(*args)


def _mlstm_out_kernel(qT_ref, k_ref, vT_ref, g_ref, cxf_ref, cxb_ref, mf_ref, mb_ref, ogT_ref, nw_ref, o_ref):
    n_ch = g_ref.shape[0]
    nh = MLSTM_HEADS
    le, ge = _tri_masks()
    gs = [g_ref[j] for j in range(n_ch)]
    gate = [_gate_rows(gs, d, le, ge) for d in range(2)]
    kq = {}
    for j in range(n_ch):
        for h in range(nh):
            sl = slice(h * MLSTM_DK, (h + 1) * MLSTM_DK)
            kq[j, h] = jnp.dot(k_ref[j * CHUNK:(j + 1) * CHUNK, sl], qT_ref[j, sl, :],
                               preferred_element_type=F32)
    for j in range(n_ch):
        tok = slice(j * CHUNK, (j + 1) * CHUNK)
        r2, b2 = [], []
        for d in range(2):
            lf, b, ig = gate[d][j]
            r2.append((ig - b) * LOG2E)
            b2.append(b * LOG2E)
        r_cols = jnp.concatenate(r2, axis=0).T
        for h in range(nh):
            sl = slice(h * MLSTM_DK, (h + 1) * MLSTM_DK)
            qT = qT_ref[j, sl, :]
            vx = vT_ref[j, h * MV_ROWS:(h + 1) * MV_ROWS, :]
            sT = kq[j, h]
            h_sum = jnp.zeros((MLSTM_DK, CHUNK), F32)
            for d, (cx_ref, m_ref) in enumerate(((cxf_ref, mf_ref), (cxb_ref, mb_ref))):
                mask = le if d == 0 else ge
                i = d * nh + h
                rm = jnp.where(mask, jnp.broadcast_to(r_cols[:, i:i + 1], (CHUNK, CHUNK)), NEG_BIG)
                cm = jnp.max(rm, axis=0, keepdims=True)
                m_prev = m_ref[0, j, h:h + 1, :] * LOG2E
                mm = jnp.maximum(cm, m_prev)
                pT = (sT * jnp.exp2(rm - mm)).astype(BF16)
                qi = qT * jnp.exp2(m_prev - mm).astype(BF16)
                lhs = jnp.concatenate([vx, cx_ref[0, j, h]], axis=1)
                mix = jnp.dot(lhs, jnp.concatenate([pT, qi], axis=0), preferred_element_type=F32)
                den = mix[MLSTM_DK:MLSTM_DK + 1]
                clamp = jnp.exp2(-(b2[d][h:h + 1, :] + mm))
                h_sum = h_sum + mix[:MLSTM_DK] * (1.0 / jnp.maximum(jnp.abs(den), clamp))
            ms = jnp.mean(h_sum * h_sum, axis=0, keepdims=True)
            hn = h_sum * lax.rsqrt(ms + EPS) * nw_ref[sl, :] * ogT_ref[j, sl, :].astype(F32)
            o_ref[tok, sl] = hn.T.astype(BF16)


def _mlstm_out_call(mqT, mk, mvT, gates, cxf, cxb, mf, mb, ogT, nwb, n_batch, seq):
    g = _per_step(seq // CHUNK, OUT_CHUNKS)
    nc = seq // (CHUNK * g)
    tok = lambda b, c: (b * nc + c, 0)
    chk = lambda b, c: (b * nc + c, 0, 0)
    cx_blk = (1, g, MLSTM_HEADS, MV_ROWS, MLSTM_DK)
    m_blk = (1, g, MLSTM_HEADS, CHUNK)
    return pl.pallas_call(
        _mlstm_out_kernel,
        out_shape=jax.ShapeDtypeStruct((n_batch * seq, MLSTM_WIDTH), BF16),
        grid=(n_batch, nc),
        in_specs=[pl.BlockSpec((g, MLSTM_WIDTH, CHUNK), chk),
                  pl.BlockSpec((g * CHUNK, MLSTM_WIDTH), tok),
                  pl.BlockSpec((g, MLSTM_HEADS * MV_ROWS, CHUNK), chk),
                  pl.BlockSpec((g, N_GATES, CHUNK), chk),
                  pl.BlockSpec(cx_blk, lambda b, c: (b, c, 0, 0, 0)),
                  pl.BlockSpec(cx_blk, lambda b, c: (b, c, 0, 0, 0)),
                  pl.BlockSpec(m_blk, lambda b, c: (b, c, 0, 0)),
                  pl.BlockSpec(m_blk, lambda b, c: (b, c, 0, 0)),
                  pl.BlockSpec((g, MLSTM_WIDTH, CHUNK), chk),
                  pl.BlockSpec((MLSTM_WIDTH, CHUNK), lambda b, c: (0, 0))],
        out_specs=pl.BlockSpec((g * CHUNK, MLSTM_WIDTH), tok),
        compiler_params=pltpu.CompilerParams(dimension_semantics=("parallel", "parallel"),
                                             vmem_limit_bytes=VMEM_LIMIT),
        name="mlstm_chunk_out",
    )(mqT, mk, mvT, gates, cxf, cxb, mf, mb, ogT, nwb)


def _ffn_kernel(x_ref, a_ref, hm_ref, mods_ref, wo_ref, n2w_ref, wgu_ref, wd_ref, fw_ref, o_ref):
    g1 = mods_ref[0, 2:3, :]
    sh2 = mods_ref[0, 3:4, :]
    sc2 = mods_ref[0, 4:5, :]
    g2 = mods_ref[0, 5:6, :]
    subs = [slice(a, a + FFN_SUB) for a in range(0, x_ref.shape[0], FFN_SUB)]
    mix = [jnp.dot(a_ref[s, :], wo_ref[0:ATTN_WIDTH, :], preferred_element_type=F32)
           + jnp.dot(hm_ref[s, :], wo_ref[ATTN_WIDTH:D_MODEL, :], preferred_element_type=F32) for s in subs]
    x1 = [x_ref[s, :] + g1 * m for s, m in zip(subs, mix)]
    h2 = [(_rms(x) * n2w_ref[...] * (1.0 + sc2) + sh2).astype(BF16) for x in x1]
    acc = [jnp.zeros(x.shape, F32) for x in x1]
    for c0 in range(0, D_FF, FF_CHUNK):
        c1 = min(c0 + FF_CHUNK, D_FF)
        act = []
        for h in h2:
            gg = jnp.dot(h, wgu_ref[:, c0:c1], preferred_element_type=F32)
            uu = jnp.dot(h, wgu_ref[:, D_FF + c0:D_FF + c1], preferred_element_type=F32)
            act.append((gg * jax.nn.sigmoid(gg) * uu).astype(BF16))
        acc = [a + jnp.dot(t, wd_ref[c0:c1, :], preferred_element_type=F32) for a, t in zip(acc, act)]
    for s, x, a in zip(subs, x1, acc):
        o_ref[s, :] = _rms(x + g2 * a) * fw_ref[...]


def _ffn_call(x2d, attn_o, hm, mods, per_request, seq, wo, n2w, wgu, wd, fw):
    T = x2d.shape[0]
    tm = TOKEN_TILE
    mod_map = _mod_map(per_request, seq, tm)
    return pl.pallas_call(
        _ffn_kernel,
        out_shape=jax.ShapeDtypeStruct((T, D_MODEL), F32),
        grid=(T // tm,),
        in_specs=[pl.BlockSpec((tm, D_MODEL), lambda i: (i, 0)),
                  pl.BlockSpec((tm, ATTN_WIDTH), lambda i: (i, 0)),
                  pl.BlockSpec((tm, MLSTM_WIDTH), lambda i: (i, 0)),
                  pl.BlockSpec((1, 6, D_MODEL), mod_map),
                  _resident(wo.shape), _resident((1, D_MODEL)), _resident(wgu.shape),
                  _resident(wd.shape), _resident((1, D_MODEL))],
        out_specs=pl.BlockSpec((tm, D_MODEL), lambda i: (i, 0)),
        compiler_params=pltpu.CompilerParams(dimension_semantics=("parallel",),
                                             vmem_limit_bytes=VMEM_LIMIT),
        name="out_proj_ffn",
    )(x2d, attn_o, hm, mods, wo, n2w, wgu, wd, fw)


def _rope_tables(seq, gain, scale):
    pos = jnp.arange(seq, dtype=jnp.int32)
    row_ids = (pos // GRID_W).astype(F32)
    col_ids = (pos % GRID_W).astype(F32)
    inv = ROPE_THETA ** (-jnp.arange(0, AXIS_DIM, 2, dtype=F32) / AXIS_DIM)
    ang_r = inv[:, None] * row_ids[None, :]
    ang_c = inv[:, None] * col_ids[None, :]
    cos = jnp.concatenate([jnp.cos(ang_r)] * 2 + [jnp.cos(ang_c)] * 2, axis=0)
    sin = jnp.concatenate([-jnp.sin(ang_r), jnp.sin(ang_r), -jnp.sin(ang_c), jnp.sin(ang_c)], axis=0)
    g = gain.astype(F32) * scale
    g_sw = _swap_halves(g, axis=0)
    return g[:, None] * cos, g_sw[:, None] * sin


def _flat_tables(gain, scale, width):
    g = gain.astype(F32) * scale
    return jnp.broadcast_to(g[:, None], (HEAD_DIM, width)), jnp.zeros((HEAD_DIM, width), F32)


def _group(x, mods, per_request, seq, weights, tables, cache, init, emit_cache):
    n_batch = x.shape[0]
    x2d = x.reshape(n_batch * seq, D_MODEL)
    outs = _proj_call(x2d, mods, per_request, seq, weights["n1w"], weights["wt"], weights["wn"],
                      weights["gbias"], tables, emit_cache)
    qT, k, vT, mqT, mk, mvT, ogT, gates = outs[:8]
    q_tiles = _per_step(seq // Q_TILE, Q_TILES_PER_STEP)
    attn_o = _attn_call(qT, k, vT, n_batch, seq, q_tiles, cache)
    cxf, cxb, mf, mb, *final = _scan_call(mk, mvT, gates, init, n_batch, seq, emit_cache)
    hm = _mlstm_out_call(mqT, mk, mvT, gates, cxf, cxb, mf, mb, ogT, weights["mnw"], n_batch, seq)
    y = _ffn_call(x2d, attn_o, hm, mods, per_request, seq, weights["wo"], weights["n2w"], weights["wgu"],
                  weights["wd"], weights["fw"])
    return y.reshape(n_batch, seq, D_MODEL), outs[8:], final


def kernel(x_prompt, x_sample, cache_k, cache_v, state_C, state_n, state_m, c, c_ctx, w_ada, b_ada,
           norm1_w, w_in, gate_bias, q_norm_w, k_norm_w, mlstm_norm_w, w_out, norm2_w, w_gu, w_down,
           final_norm_w):
    depth = w_ada.shape[0]
    assert depth == 1, "single-layer trunk"
    n_ctx, ctx_len, _ = x_prompt.shape
    n_dec, dec_len, _ = x_sample.shape
    past = cache_k.shape[2]
    assert (n_ctx * ctx_len) % PROJ_TILE == 0 and dec_len % PROJ_TILE == 0, "token counts must fill whole tiles"
    assert ctx_len % KEY_BLOCK == 0 and past % KEY_BLOCK == 0 and dec_len % GRID_W == 0
    l = 0

    assert n_dec + 1 <= MOD_ROWS
    cond = jnp.zeros((MOD_ROWS, D_MODEL), F32).at[:n_dec].set(c).at[n_dec].set(c_ctx)
    mods = _mods_call(cond, w_ada[l], b_ada[l]).reshape(MOD_ROWS, 6, D_MODEL)

    wi = w_in[l]
    edges = [sum(_SPLITS[:i]) for i in range(1, len(_SPLITS))]
    aq, ak, av, mq_w, mk_w, mv_w, mo_w, mg_w = jnp.split(wi, edges, axis=1)
    weights = {
        "wt": jnp.concatenate([aq, ak, av, mq_w, mv_w, mo_w, mg_w], axis=1).T.astype(BF16),
        "wn": mk_w.astype(BF16),
        "gbias": gate_bias[l].reshape(N_GATES, 1).astype(F32),
        "n1w": norm1_w[l].reshape(1, D_MODEL),
        "n2w": norm2_w[l].reshape(1, D_MODEL),
        "mnw": jnp.broadcast_to(mlstm_norm_w[l].astype(F32)[:, None], (MLSTM_WIDTH, CHUNK)),
        "wo": w_out[l].astype(BF16),
        "wgu": w_gu[l].astype(BF16),
        "wd": w_down[l].astype(BF16),
        "fw": final_norm_w.reshape(1, D_MODEL),
    }
    q_scale = (HEAD_DIM ** -0.5) * LOG2E

    tabs_ctx = _flat_tables(q_norm_w[l], q_scale, PROJ_TILE) + _flat_tables(k_norm_w[l], 1.0, PROJ_TILE)
    y_prompt, (kc, vc), (mfin, new_state_C, new_state_n) = _group(
        x_prompt, mods[n_dec:n_dec + 1], False, ctx_len, weights, tabs_ctx, None, None, True)
    new_cache_k = kc.reshape(n_ctx, 1, ctx_len, N_KV_HEADS, HEAD_DIM)
    new_cache_v = vc.reshape(n_ctx, 1, ctx_len, N_KV_HEADS, HEAD_DIM)
    new_state_m = mfin[..., 0][:, None]

    tabs_dec = _rope_tables(dec_len, q_norm_w[l], q_scale) + _rope_tables(dec_len, k_norm_w[l], 1.0)
    ck =cache_k[:, l].reshape(n_dec, past, KV_WIDTH).astype(BF16)
    cvT = cache_v[:, l].reshape(n_dec, past // KEY_BLOCK, KEY_BLOCK, KV_WIDTH).transpose(0, 1, 3, 2)
    pad_ones = jnp.ones(cvT.shape[:2] + (V_ROWS - HEAD_DIM, KEY_BLOCK), cvT.dtype)
    cvT = jnp.concatenate([cvT[:, :, :HEAD_DIM], pad_ones, cvT[:, :, HEAD_DIM:], pad_ones], axis=2).astype(BF16)
    m0 = jnp.broadcast_to(state_m[:, l].astype(F32)[..., None], (n_dec, 2, MLSTM_HEADS, CHUNK))
    init = (state_C[:, l:l + 1].astype(F32), state_n[:, l:l + 1].astype(F32), m0)
    y_sample, _, _ = _group(x_sample, mods[:n_dec], True, dec_len, weights, tabs_dec, (ck, cvT), init, False)

    return (y_prompt, y_sample, new_cache_k, new_cache_v, new_state_C, new_state_n, new_state_m)
```
